```python
import math
import jax, jax.numpy as jnp
from jax import lax
import numpy as np

D_MODEL = 1024
BATCH = 8
SEQ = 4096
DEPTH = 4

GRID_W = 64
CTX_LEN = 256
N_MIXERS = 3
BLOCK = 128
ROPE_THETA = 10000.0
LN_EPS = 1e-5
RMS_EPS = 1e-6
DN_ALPHA = (2.0 * DEPTH) ** 0.25
DN_BETA = (8.0 * DEPTH) ** -0.25

A_HEADS = 16
A_KV_HEADS = 4
A_GROUP = A_HEADS // A_KV_HEADS
A_HEAD_DIM = D_MODEL // A_HEADS
A_WINDOW = 128
A_Q_COLS = A_HEADS * A_HEAD_DIM
A_KV_COLS = A_KV_HEADS * A_HEAD_DIM
A_IN_COLS = A_Q_COLS + 2 * A_KV_COLS

B_HEADS = 16
B_NOPE = 64
B_ROPE = 32
B_V = 64
B_Q_RANK = 512
B_KV_RANK = 256
B_IN_COLS = B_Q_RANK + B_KV_RANK + B_ROPE

C_HEADS = 8
C_HEAD_DIM = D_MODEL // (2 * C_HEADS)
C_V_DIM = 2 * C_HEAD_DIM
C_IN_COLS = 3 * C_HEADS * 2 * C_HEAD_DIM

N_GROUPS = 4
EXPERTS_PER_GROUP = 8
N_EXPERTS = N_GROUPS * EXPERTS_PER_GROUP
TOP_K = 2
D_EXPERT = 512
MOE_BLOCK = 128

N_A = len(range(0, DEPTH, N_MIXERS))
N_B = len(range(1, DEPTH, N_MIXERS))
N_C = len(range(2, DEPTH, N_MIXERS))

kernel_name = "hybrid_interleaved_diffusion_block"


def layer_norm(x, g, b):
    xf = x.astype(jnp.float32)
    mu = jnp.mean(xf, -1, keepdims=True)
    var = jnp.mean(jnp.square(xf - mu), -1, keepdims=True)
    y = (xf - mu) * lax.rsqrt(var + LN_EPS)
    return (y * g + b).astype(x.dtype)


def rms_norm(x, g):
    xf = x.astype(jnp.float32)
    y = xf * lax.rsqrt(jnp.mean(xf * xf, -1, keepdims=True) + RMS_EPS)
    return (y * g).astype(x.dtype)


def modulate(x, shift, scale):
    return x * (1 + scale) + shift


def axial_rope_tables(n_tokens, rot_dim):
    rows = n_tokens // GRID_W
    row = jnp.broadcast_to(jnp.arange(rows, dtype=jnp.float32)[:, None], (rows, GRID_W)).reshape(-1)
    col = jnp.broadcast_to(jnp.arange(GRID_W, dtype=jnp.float32)[None, :], (rows, GRID_W)).reshape(-1)
    axis_dim = rot_dim // 2
    inv_freq = ROPE_THETA ** (-jnp.arange(0, axis_dim, 2, dtype=jnp.float32) / axis_dim)
    ang_r = row[:, None] * inv_freq
    ang_c = col[:, None] * inv_freq
    return (jnp.cos(ang_r), jnp.sin(ang_r), jnp.cos(ang_c), jnp.sin(ang_c))


def _rotate(x, cos, sin):
    x1, x2 = jnp.split(x, 2, axis=-1)
    return jnp.concatenate([x1 * cos - x2 * sin, x2 * cos + x1 * sin], axis=-1)


def apply_axial_rope(x, tabs):
    cos_r, sin_r, cos_c, sin_c = [t.reshape((t.shape[0],) + (1,) * (x.ndim - 3) + (t.shape[1],)) for t in tabs]
    xr, xc = jnp.split(x.astype(jnp.float32), 2, axis=-1)
    return jnp.concatenate([_rotate(xr, cos_r, sin_r), _rotate(xc, cos_c, sin_c)], -1).astype(x.dtype)


def softmax_with_sink(s, sink):
    m = jnp.maximum(jnp.max(s, -1, keepdims=True), sink)
    p = jnp.exp(s - m)
    return p / (jnp.sum(p, -1, keepdims=True) + jnp.exp(sink - m))


def sweep_query_blocks(fn, q):
    B, S = q.shape[:2]
    qb = jnp.swapaxes(q.reshape((B, S // BLOCK, BLOCK) + q.shape[2:]), 0, 1)
    o = lax.map(fn, qb)
    return jnp.swapaxes(o, 0, 1).reshape(B, S, -1)


def window_gqa_mixer(h, hc, w_in, w_out, sink, rope, with_ctx_out):
    B, S, _ = h.shape
    L = hc.shape[1]
    scale = A_HEAD_DIM ** -0.5

    def project(t):
        n = t.shape[1]
        q, k, v = jnp.split(t @ w_in, [A_Q_COLS, A_Q_COLS + A_KV_COLS], axis=-1)
        return (q.reshape(B, n, A_KV_HEADS, A_GROUP, A_HEAD_DIM),
                k.reshape(B, n, A_KV_HEADS, A_HEAD_DIM),
                v.reshape(B, n, A_KV_HEADS, A_HEAD_DIM))

    q, k, v = project(h)
    qc, kc, vc = project(hc)
    q = apply_axial_rope(q, rope)
    k = apply_axial_rope(k, rope)
    sink_b = sink.astype(jnp.float32).reshape(1, A_KV_HEADS, A_GROUP, 1, 1)
    pad = ((0, 0), (BLOCK, BLOCK), (0, 0), (0, 0))
    k_pad = jnp.pad(k, pad)
    v_pad = jnp.pad(v, pad)
    q_off = jnp.arange(BLOCK)
    k_off = jnp.arange(3 * BLOCK)

    def attend_block(blk):
        start = blk * BLOCK
        qb = lax.dynamic_slice_in_dim(q, start, BLOCK, axis=1)
        kb = lax.dynamic_slice_in_dim(k_pad, start, 3 * BLOCK, axis=1)
        vb = lax.dynamic_slice_in_dim(v_pad, start, 3 * BLOCK, axis=1)
        q_pos = start + q_off
        k_pos = start - BLOCK + k_off
        valid = (jnp.abs(q_pos[:, None] - k_pos[None, :]) <= A_WINDOW) & (k_pos >= 0) & (k_pos < S)
        s_win = jnp.einsum('bqhgd,bkhd->bhgqk', qb, kb).astype(jnp.float32) * scale
        s_win = jnp.where(valid, s_win, -jnp.inf)
        s_ctx = jnp.einsum('bqhgd,bkhd->bhgqk', qb, kc).astype(jnp.float32) * scale
        p = softmax_with_sink(jnp.concatenate([s_win, s_ctx], -1), sink_b).astype(v.dtype)
        o = (jnp.einsum('bhgqk,bkhd->bqhgd', p[..., :3 * BLOCK], vb)
             + jnp.einsum('bhgqk,bkhd->bqhgd', p[..., 3 * BLOCK:], vc))
        return o.reshape(B, BLOCK, A_Q_COLS)

    o = lax.map(attend_block, jnp.arange(S // BLOCK))
    y = jnp.swapaxes(o, 0, 1).reshape(B, S, A_Q_COLS) @ w_out
    yc = None
    if with_ctx_out:
        s = jnp.einsum('bqhgd,bkhd->bhgqk', qc, kc).astype(jnp.float32) * scale
        p = softmax_with_sink(s, sink_b).astype(vc.dtype)
        yc = jnp.einsum('bhgqk,bkhd->bqhgd', p, vc).reshape(B, L, A_Q_COLS) @ w_out
    return y, yc


def mla_mixer(h, hc, w_in, q_norm, kv_norm, w_uq, w_ukv, w_out, rope, with_ctx_out):
    B = h.shape[0]
    scale = (B_NOPE + B_ROPE) ** -0.5

    def project(t, tabs):
        n = t.shape[1]
        c_q, c_kv, k_rope = jnp.split(t @ w_in, [B_Q_RANK, B_Q_RANK + B_KV_RANK], axis=-1)
        q = (rms_norm(c_q, q_norm) @ w_uq).reshape(B, n, B_HEADS, B_NOPE + B_ROPE)
        kv = (rms_norm(c_kv, kv_norm) @ w_ukv).reshape(B, n, B_HEADS, B_NOPE + B_V)
        q_nope, q_rope = jnp.split(q, [B_NOPE], axis=-1)
        k_nope, v = jnp.split(kv, [B_NOPE], axis=-1)
        k_rope = k_rope[:, :, None, :]
        if tabs is not None:
            q_rope = apply_axial_rope(q_rope, tabs)
            k_rope = apply_axial_rope(k_rope, tabs)
        k_rope = jnp.broadcast_to(k_rope, (B, n, B_HEADS, B_ROPE))
        return (jnp.concatenate([q_nope, q_rope], -1), jnp.concatenate([k_nope, k_rope], -1), v)

    def attend(qb, kk, vv):
        s = jnp.einsum('bqhd,bkhd->bhqk', qb, kk).astype(jnp.float32) * scale
        p = jax.nn.softmax(s, axis=-1).astype(vv.dtype)
        o = jnp.einsum('bhqk,bkhd->bqhd', p, vv)
        return o.reshape(o.shape[0], o.shape[1], B_HEADS * B_V)

    q, k, v = project(h, rope)
    qc, kc, vc = project(hc, None)
    k_all = jnp.concatenate([k, kc], axis=1)
    v_all = jnp.concatenate([v, vc], axis=1)
    y = sweep_query_blocks(lambda qb: attend(qb, k_all, v_all), q) @ w_out
    yc = attend(qc, kc, vc) @ w_out if with_ctx_out else None
    return y, yc


def diff_mixer(h, hc, w_in, lam_params, subln, w_out, rope, lambda_init, with_ctx_out):
    B = h.shape[0]
    scale = C_HEAD_DIM ** -0.5

    def project(t):
        n = t.shape[1]
        q, k, v = jnp.split(t @ w_in, 3, axis=-1)
        return (q.reshape(B, n, C_HEADS, 2, C_HEAD_DIM),
                k.reshape(B, n, C_HEADS, 2, C_HEAD_DIM),
                v.reshape(B, n, C_HEADS, C_V_DIM))

    q, k, v = project(h)
    qc, kc, vc = project(hc)
    q = apply_axial_rope(q, rope)
    k = apply_axial_rope(k, rope)
    lp = lam_params.astype(jnp.float32)
    lam = jnp.exp(jnp.sum(lp[0] * lp[1])) - jnp.exp(jnp.sum(lp[2] * lp[3])) + lambda_init

    def attend(qb, kk, vv):
        s = jnp.einsum('bqhid,bkhid->bhiqk', qb, kk).astype(jnp.float32) * scale
        p = jax.nn.softmax(s, axis=-1)
        a = (p[:, :, 0] - lam * p[:, :, 1]).astype(vv.dtype)
        o = jnp.einsum('bhqk,bkhe->bqhe', a, vv)
        o = rms_norm(o, subln) * (1.0 - lambda_init)
        return o.reshape(o.shape[0], o.shape[1], C_HEADS * C_V_DIM)

    k_all = jnp.concatenate([k, kc], axis=1)
    v_all = jnp.concatenate([v, vc], axis=1)
    y = sweep_query_blocks(lambda qb: attend(qb, k_all, v_all), q) @ w_out
    yc = attend(qc, kc, vc) @ w_out if with_ctx_out else None
    return y, yc


def hier_moe(h, w_grp, b_grp, w_rt, b_rt, w_gate, w_up, w_down):
    T, D = h.shape
    logits_g = (h @ w_grp).astype(jnp.float32) + b_grp.astype(jnp.float32)
    p_grp = jax.nn.softmax(logits_g, axis=-1)
    g_sel = jnp.argmax(logits_g, axis=-1).astype(jnp.int32)
    p_g = jnp.take_along_axis(p_grp, g_sel[:, None], axis=-1)
    logits_e = ((h @ w_rt).astype(jnp.float32) + b_rt.astype(jnp.float32)).reshape(T, N_GROUPS, EXPERTS_PER_GROUP)
    logits_in = jnp.take_along_axis(logits_e, g_sel[:, None, None], axis=1)[:, 0]
    top_v, top_i = lax.top_k(logits_in, TOP_K)
    weights = p_g * jax.nn.softmax(top_v, axis=-1)
    expert = g_sel[:, None] * EXPERTS_PER_GROUP + top_i.astype(jnp.int32)

    TK = T * TOP_K
    flat_e = expert.reshape(TK)
    flat_w = weights.reshape(TK)
    flat_t = jnp.repeat(jnp.arange(T, dtype=jnp.int32), TOP_K)
    order = jnp.argsort(flat_e)
    se, st, sw = flat_e[order], flat_t[order], flat_w[order]
    counts = jnp.bincount(flat_e, length=N_EXPERTS)
    padded = (counts + MOE_BLOCK - 1) // MOE_BLOCK * MOE_BLOCK
    pad_end = jnp.cumsum(padded)
    pad_start = pad_end - padded
    start = jnp.cumsum(counts) - counts
    dest = pad_start[se] + jnp.arange(TK, dtype=jnp.int32) - start[se]
    n_blocks = -(-TK // MOE_BLOCK) + N_EXPERTS
    buf_len = n_blocks * MOE_BLOCK
    buf_tok = jnp.full((buf_len,), T, jnp.int32).at[dest].set(st)
    buf_w = jnp.zeros((buf_len,), h.dtype).at[dest].set(sw.astype(h.dtype))
    block_e = jnp.minimum(jnp.searchsorted(pad_end, jnp.arange(n_blocks) * MOE_BLOCK, side='right'),
                          N_EXPERTS - 1).astype(jnp.int32)
    h_pad = jnp.concatenate([h, jnp.zeros((1, D), h.dtype)], axis=0)
    xb = h_pad[buf_tok].reshape(n_blocks, MOE_BLOCK, D)

    def expert_block(args):
        xblk, e = args
        return (jax.nn.silu(xblk @ w_gate[e]) * (xblk @ w_up[e])) @ w_down[e]

    yb = lax.map(expert_block, (xb, block_e)).reshape(buf_len, D)
    out = jnp.zeros((T + 1, D), h.dtype).at[buf_tok].add(yb * buf_w[:, None])
    return out[:T]


def setup_inputs(seed: int = 0) -> dict:
    key = jax.random.key(seed)
    counter = [0]

    def nrm(shape, scale):
        counter[0] += 1
        return jax.random.normal(jax.random.fold_in(key, counter[0]), shape, jnp.float32) * scale

    D = D_MODEL
    return {
        "x": nrm((BATCH, SEQ, D), 1.0),
        "c": nrm((BATCH, D), 1.0),
        "ctx": nrm((BATCH, CTX_LEN, D), 1.0),
        "c_ctx": nrm((D,), 1.0),
        "ada_w": nrm((DEPTH, D, 6 * D), 0.5 * D ** -0.5),
        "ada_b": nrm((DEPTH, 6 * D), 0.02),
        "ln_mix_g": 1.0 + nrm((DEPTH, D), 0.02),
        "ln_mix_b": nrm((DEPTH, D), 0.02),
        "ln_ffn_g": 1.0 + nrm((DEPTH, D), 0.02),
        "ln_ffn_b": nrm((DEPTH, D), 0.02),
        "win_w_in": nrm((N_A, D, A_IN_COLS), D ** -0.5),
        "win_w_out": nrm((N_A, A_Q_COLS, D), A_Q_COLS ** -0.5 * DN_BETA),
        "win_sink": nrm((N_A, A_HEADS), 0.5),
        "mla_w_in": nrm((N_B, D, B_IN_COLS), D ** -0.5),
        "mla_q_norm": 1.0 + nrm((N_B, B_Q_RANK), 0.02),
        "mla_kv_norm": 1.0 + nrm((N_B, B_KV_RANK), 0.02),
        "mla_w_uq": nrm((N_B, B_Q_RANK, B_HEADS * (B_NOPE + B_ROPE)), B_Q_RANK ** -0.5),
        "mla_w_ukv": nrm((N_B, B_KV_RANK, B_HEADS * (B_NOPE + B_V)), B_KV_RANK ** -0.5),
        "mla_w_out": nrm((N_B, B_HEADS * B_V, D), (B_HEADS * B_V) ** -0.5 * DN_BETA),
        "dif_w_in": nrm((N_C, D, C_IN_COLS), D ** -0.5),
        "dif_lambda": nrm((N_C, 4, C_HEAD_DIM), 0.1),
        "dif_subln": 1.0 + nrm((N_C, C_V_DIM), 0.02),
        "dif_w_out": nrm((N_C, C_HEADS * C_V_DIM, D), (C_HEADS * C_V_DIM) ** -0.5 * DN_BETA),
        "moe_w_grp": nrm((DEPTH, D, N_GROUPS), D ** -0.5),
        "moe_b_grp": nrm((DEPTH, N_GROUPS), 0.01),
        "moe_w_rt": nrm((DEPTH, D, N_EXPERTS), D ** -0.5),
        "moe_b_rt": nrm((DEPTH, N_EXPERTS), 0.01),
        "moe_w_gate": nrm((DEPTH, N_EXPERTS, D, D_EXPERT), D ** -0.5),
        "moe_w_up": nrm((DEPTH, N_EXPERTS, D, D_EXPERT), D ** -0.5),
        "moe_w_down": nrm((DEPTH, N_EXPERTS, D_EXPERT, D), D_EXPERT ** -0.5 * DN_BETA),
    }


def reference(x, c, ctx, c_ctx, ada_w, ada_b, ln_mix_g, ln_mix_b, ln_ffn_g, ln_ffn_b,
              win_w_in, win_w_out, win_sink,
              mla_w_in, mla_q_norm, mla_kv_norm, mla_w_uq, mla_w_ukv, mla_w_out,
              dif_w_in, dif_lambda, dif_subln, dif_w_out,
              moe_w_grp, moe_b_grp, moe_w_rt, moe_b_rt, moe_w_gate, moe_w_up, moe_w_down):
    B, S, D = x.shape
    L = ctx.shape[1]
    rope64 = axial_rope_tables(S, A_HEAD_DIM)
    rope32 = axial_rope_tables(S, B_ROPE)
    xc = ctx
    silu_c = jax.nn.silu(c)
    silu_cc = jax.nn.silu(c_ctx)
    for i in range(DEPTH):
        last = i == DEPTH - 1
        kind = i % N_MIXERS
        slot = i // N_MIXERS
        mod = silu_c @ ada_w[i] + ada_b[i]
        mod_c = silu_cc @ ada_w[i] + ada_b[i]
        sh_m, sc_m, g_m, sh_f, sc_f, g_f = jnp.split(mod[:, None, :], 6, axis=-1)
        csh_m, csc_m, cg_m, csh_f, csc_f, cg_f = jnp.split(mod_c, 6, axis=-1)

        h = modulate(x, sh_m, sc_m)
        hc = modulate(xc, csh_m, csc_m)
        if kind == 0:
            y, yc = window_gqa_mixer(h, hc, win_w_in[slot], win_w_out[slot], win_sink[slot], rope64, not last)
        elif kind == 1:
            y, yc = mla_mixer(h, hc, mla_w_in[slot], mla_q_norm[slot], mla_kv_norm[slot], mla_w_uq[slot],
                              mla_w_ukv[slot], mla_w_out[slot], rope32, not last)
        else:
            lambda_init = 0.8 - 0.6 * math.exp(-0.3 * i)
            y, yc = diff_mixer(h, hc, dif_w_in[slot], dif_lambda[slot], dif_subln[slot], dif_w_out[slot],
                               rope64, lambda_init, not last)
        x = layer_norm(DN_ALPHA * x + g_m * y, ln_mix_g[i], ln_mix_b[i])

        h = modulate(x, sh_f, sc_f)
        moe_args = (moe_w_grp[i], moe_b_grp[i], moe_w_rt[i], moe_b_rt[i], moe_w_gate[i], moe_w_up[i], moe_w_down[i])
        if not last:
            xc = layer_norm(DN_ALPHA * xc + cg_m * yc, ln_mix_g[i], ln_mix_b[i])
            hc = modulate(xc, csh_f, csc_f)
            y_all = hier_moe(jnp.concatenate([h.reshape(B * S, D), hc.reshape(B * L, D)], axis=0), *moe_args)
            y = y_all[:B * S].reshape(B, S, D)
            yc = y_all[B * S:].reshape(B, L, D)
            xc = layer_norm(DN_ALPHA * xc + cg_f * yc, ln_ffn_g[i], ln_ffn_b[i])
        else:
            y = hier_moe(h.reshape(B * S, D), *moe_args).reshape(B, S, D)
        x = layer_norm(DN_ALPHA * x + g_f * y, ln_ffn_g[i], ln_ffn_b[i])
    return x
```

```python
import functools
import math

import jax
import jax.numpy as jnp
from jax import lax
from jax.experimental import pallas as pl
from jax.experimental.pallas import tpu as pltpu

LANES = 128
GRID_W = 64
ROPE_THETA = 10000.0
LN_EPS = 1e-5
RMS_EPS = 1e-6

A_HEADS, A_KV_HEADS, A_HEAD_DIM, A_WINDOW = 16, 4, 64, 128
A_GROUP = A_HEADS // A_KV_HEADS
B_HEADS, B_NOPE, B_ROPE, B_V, B_Q_RANK, B_KV_RANK = 16, 64, 32, 64, 512, 256
C_HEADS, C_HEAD_DIM = 8, 64
C_V_DIM = 2 * C_HEAD_DIM
N_GROUPS, EXPERTS_PER_GROUP, TOP_K = 4, 8, 2
N_EXPERTS = N_GROUPS * EXPERTS_PER_GROUP
N_MIXERS = 3

ROW_TILE = 256
ATTN_Q_TILE = 256
WIN_BLOCK = 128
MOE_ROWS = 256
VMEM_LIMIT = 56 * 1024 * 1024

_F32 = jnp.float32
_BF16 = jnp.bfloat16
_NT = (((1,), (1,)), ((), ()))


def _cparams(*sem):
    return pltpu.CompilerParams(dimension_semantics=sem, vmem_limit_bytes=VMEM_LIMIT)


def _lane_iota(shape):
    return lax.broadcasted_iota(jnp.int32, shape, len(shape) - 1)


def _rope(a, cos, sin_signed, half):
    lane = _lane_iota(a.shape)
    swapped = jnp.where(lane % (2 * half) < half,
                        pltpu.roll(a, a.shape[-1] - half, axis=1),
                        pltpu.roll(a, half, axis=1))
    return a * cos + swapped * sin_signed


def _modulated(x_ref, mod_ref, shift_row):
    m = mod_ref[0]
    return x_ref[0] * (1.0 + m[shift_row + 1:shift_row + 2, :]) + m[shift_row:shift_row + 1, :]


def _ada_kernel(c_ref, w_ref, b_ref, o_ref):
    c = c_ref[...]
    s = c * jax.nn.sigmoid(c)
    o_ref[0] = jnp.dot(s, w_ref[0], preferred_element_type=_F32,
                       precision=lax.Precision.HIGHEST) + b_ref[0]


def _ada_mods(cs, ada_w, ada_b):
    depth, d, n = ada_w.shape
    r = cs.shape[0]
    tn = 1024
    return pl.pallas_call(
        _ada_kernel,
        grid=(depth, n // tn),
        in_specs=[pl.BlockSpec((r, d), lambda i, j: (0, 0)),
                  pl.BlockSpec((1, d, tn), lambda i, j: (i, 0, j)),
                  pl.BlockSpec((1, 1, tn), lambda i, j: (i, 0, j))],
        out_specs=pl.BlockSpec((1, r, tn), lambda i, j: (i, 0, j)),
        out_shape=jax.ShapeDtypeStruct((depth, r, n), _F32),
        compiler_params=_cparams("parallel", "parallel"),
        name="ada_mods",
    )(cs, ada_w, ada_b.reshape(depth, 1, n))


def _row_specs(b, p, d, n_lat_tiles, tm):
    x_spec = pl.BlockSpec((1, tm, d), lambda i, j: (i, j, 0))
    mod_spec = pl.BlockSpec((1, 6, d), lambda i, j: (jnp.where(j < n_lat_tiles, i, b), 0, 0))
    return x_spec, mod_spec


def _proj_rope_kernel(x_ref, mod_ref, w_ref, cos_ref, sin_ref, o_ref, *, n_rope):
    h = _modulated(x_ref, mod_ref, 0).astype(_BF16)
    acc = jnp.dot(h, w_ref[...], preferred_element_type=_F32)
    cos, sin = cos_ref[...], sin_ref[...]
    n = acc.shape[1]
    for j in range(n // LANES):
        a = acc[:, j * LANES:(j + 1) * LANES]
        if j * LANES < n_rope:
            a = _rope(a, cos, sin, 16)
        o_ref[0, :, j * LANES:(j + 1) * LANES] = a.astype(_BF16)


def _proj_rope(x, mods, w, cos, sin, n_rope, n_lat_tiles):
    b, p, d = x.shape
    n = w.shape[1]
    tm = ROW_TILE
    x_spec, mod_spec = _row_specs(b, p, d, n_lat_tiles, tm)
    return pl.pallas_call(
        functools.partial(_proj_rope_kernel, n_rope=n_rope),
        grid=(b, p // tm),
        in_specs=[x_spec, mod_spec,
                  pl.BlockSpec((d, n), lambda i, j: (0, 0)),
                  pl.BlockSpec((tm, LANES), lambda i, j: (j, 0)),
                  pl.BlockSpec((tm, LANES), lambda i, j: (j, 0))],
        out_specs=pl.BlockSpec((1, tm, n), lambda i, j: (i, j, 0)),
        out_shape=jax.ShapeDtypeStruct((b, p, n), _BF16),
        compiler_params=_cparams("parallel", "parallel"),
        name="proj_rope",
    )(x, mods, w, cos, sin)


def _rms(x, g):
    return x * lax.rsqrt(jnp.mean(x * x, axis=-1, keepdims=True) + RMS_EPS) * g


def _proj_mla_kernel(x_ref, mod_ref, win_ref, qn_ref, kvn_ref, wuq_ref, wuk_ref, wv_ref,
                     cos_ref, sin_ref, q_ref, k_ref, v_ref, *, scale):
    h = _modulated(x_ref, mod_ref, 0).astype(_BF16)
    c = jnp.dot(h, win_ref[...], preferred_element_type=_F32)
    cq = _rms(c[:, :B_Q_RANK], qn_ref[...]).astype(_BF16)
    ckv = _rms(c[:, B_Q_RANK:B_Q_RANK + B_KV_RANK], kvn_ref[...]).astype(_BF16)
    cos, sin = cos_ref[...], sin_ref[...]
    k_rope = _rope(c[:, B_Q_RANK + B_KV_RANK:], cos, sin, 8)
    q = jnp.dot(cq, wuq_ref[...], preferred_element_type=_F32)
    k_nope = jnp.dot(ckv, wuk_ref[...], preferred_element_type=_F32)
    for hd in range(B_HEADS):
        sl = slice(hd * LANES, (hd + 1) * LANES)
        q_ref[0, :, sl] = (_rope(q[:, sl], cos, sin, 8) * scale).astype(_BF16)
        k_ref[0, :, sl] = (k_nope[:, sl] + k_rope).astype(_BF16)
    v_ref[0] = jnp.dot(ckv, wv_ref[...], preferred_element_type=_F32).astype(_BF16)


def _proj_mla(x, mods, w_in, q_norm, kv_norm, w_uq, w_uk, w_v, cos, sin, n_lat_tiles):
    b, p, d = x.shape
    tm = ROW_TILE
    x_spec, mod_spec = _row_specs(b, p, d, n_lat_tiles, tm)
    full = lambda a: pl.BlockSpec(a.shape, lambda i, j: (0,) * a.ndim)
    nq, nv = w_uq.shape[1], w_v.shape[1]
    tile = lambda n: pl.BlockSpec((1, tm, n), lambda i, j: (i, j, 0))
    return pl.pallas_call(
        functools.partial(_proj_mla_kernel, scale=(B_NOPE + B_ROPE) ** -0.5),
        grid=(b, p // tm),
        in_specs=[x_spec, mod_spec, full(w_in), full(q_norm), full(kv_norm), full(w_uq), full(w_uk),
                  full(w_v),
                  pl.BlockSpec((tm, LANES), lambda i, j: (j, 0)),
                  pl.BlockSpec((tm, LANES), lambda i, j: (j, 0))],
        out_specs=[tile(nq), tile(nq), tile(nv)],
        out_shape=[jax.ShapeDtypeStruct((b, p, nq), _BF16), jax.ShapeDtypeStruct((b, p, nq), _BF16),
                   jax.ShapeDtypeStruct((b, p, nv), _BF16)],
        compiler_params=_cparams("parallel", "parallel"),
        name="proj_mla",
    )(x, mods, w_in, q_norm, kv_norm, w_uq, w_uk, w_v, cos, sin)


def _attn_win_kernel(sink_ref, q_ref, k0_ref, k1_ref, k2_ref, kc_ref, v0_ref, v1_ref, v2_ref, vc_ref,
                     o_ref, *, n_lat_tiles):
    j = pl.program_id(1)
    w = WIN_BLOCK
    n_ctx = kc_ref.shape[1]
    n_keys = 3 * w + n_ctx
    q_pos = j * w + lax.broadcasted_iota(jnp.int32, (A_GROUP * w, n_keys), 0) % w
    col = lax.broadcasted_iota(jnp.int32, (A_GROUP * w, n_keys), 1)
    k_pos = (j - 1) * w + col
    n_latent_keys = jnp.where(j < n_lat_tiles, n_lat_tiles * w, 0)
    valid = (col >= 3 * w) | ((jnp.abs(q_pos - k_pos) <= A_WINDOW) & (k_pos >= 0) & (k_pos < n_latent_keys))
    lane = _lane_iota((w, LANES))
    low = lane < A_HEAD_DIM
    for kvh in range(A_KV_HEADS):
        cs = slice(kvh * LANES, (kvh + 1) * LANES)
        kk = jnp.concatenate([k0_ref[0, :, cs], k1_ref[0, :, cs], k2_ref[0, :, cs], kc_ref[0, :, cs]], axis=0)
        vv = jnp.concatenate([v0_ref[0, :, cs], v1_ref[0, :, cs], v2_ref[0, :, cs], vc_ref[0, :, cs]], axis=0)
        qa = q_ref[0, :, 2 * kvh * LANES:(2 * kvh + 1) * LANES]
        qb = q_ref[0, :, (2 * kvh + 1) * LANES:(2 * kvh + 2) * LANES]
        zero = jnp.zeros_like(qa)
        q4 = jnp.concatenate([jnp.where(low, qa, zero), jnp.where(low, zero, qa),
                              jnp.where(low, qb, zero), jnp.where(low, zero, qb)], axis=0)
        sink = jnp.concatenate([jnp.full((w, 1), sink_ref[kvh * A_GROUP + g], _F32) for g in range(A_GROUP)],
                               axis=0)
        s = lax.dot_general(q4, kk, _NT, preferred_element_type=_F32)
        s = jnp.where(valid, s, -jnp.inf)
        m = jnp.maximum(jnp.max(s, axis=-1, keepdims=True), sink)
        e = jnp.exp(s - m)
        denom = jnp.sum(e, axis=-1, keepdims=True) + jnp.exp(sink - m)
        pr = (e / denom).astype(_BF16)
        o4 = jnp.dot(pr, vv, preferred_element_type=_F32)
        o_ref[0, :, 2 * kvh * LANES:(2 * kvh + 1) * LANES] = jnp.where(low, o4[0:w], o4[w:2 * w]).astype(_BF16)
        o_ref[0, :, (2 * kvh + 1) * LANES:(2 * kvh + 2) * LANES] = (
            jnp.where(low, o4[2 * w:3 * w], o4[3 * w:4 * w]).astype(_BF16))


def _attn_win(qkv, sink, s_len, l_ctx):
    b, p, _ = qkv.shape
    w = WIN_BLOCK
    n_lat = s_len // w
    nq = A_HEADS * A_HEAD_DIM
    kvw = A_KV_HEADS * LANES
    kcol, vcol = nq // kvw, nq // kvw + 1
    ctx_blk = s_len // l_ctx

    def nb(col, off):
        return pl.BlockSpec((1, w, kvw), lambda i, j: (i, jnp.clip(j + off, 0, n_lat - 1), col))

    def cb(col):
        return pl.BlockSpec((1, l_ctx, kvw), lambda i, j: (i, ctx_blk, col))

    return pl.pallas_call(
        functools.partial(_attn_win_kernel, n_lat_tiles=n_lat),
        grid=(b, p // w),
        in_specs=[pl.BlockSpec(memory_space=pltpu.SMEM),
                  pl.BlockSpec((1, w, nq), lambda i, j: (i, j, 0)),
                  nb(kcol, -1), nb(kcol, 0), nb(kcol, 1), cb(kcol),
                  nb(vcol, -1), nb(vcol, 0), nb(vcol, 1), cb(vcol)],
        out_specs=pl.BlockSpec((1, w, nq), lambda i, j: (i, j, 0)),
        out_shape=jax.ShapeDtypeStruct((b, p, nq), _BF16),
        compiler_params=_cparams("parallel", "parallel"),
        name="attn_win",
    )(sink, qkv, qkv, qkv, qkv, qkv, qkv, qkv, qkv, qkv)


def _softmax_pv(q, k, v):
    s = lax.dot_general(q, k, _NT, preferred_element_type=_F32)
    m = jnp.max(s, axis=-1, keepdims=True)
    e = jnp.exp(s - m)
    l = jnp.sum(e, axis=-1, keepdims=True)
    return jnp.dot(e.astype(_BF16), v, preferred_element_type=_F32), l


def _attn_mla_kernel(q_ref, k_ref, v_ref, o_ref, *, n_lat_tiles, s_len):
    j = pl.program_id(2)
    low = _lane_iota(o_ref.shape[1:]) < B_V

    def run(k_rows):
        outs = []
        for hh in range(2):
            cs = slice(hh * LANES, (hh + 1) * LANES)
            o, l = _softmax_pv(q_ref[0, :, cs], k_ref[0, k_rows, cs], v_ref[0, k_rows, :])
            outs.append(o / l)
        o_ref[0] = jnp.where(low, outs[0], outs[1]).astype(_BF16)

    @pl.when(j < n_lat_tiles)
    def _():
        run(slice(None))

    @pl.when(j >= n_lat_tiles)
    def _():
        run(slice(s_len, None))


def _attn_mla(q, k, v, s_len):
    b, p, _ = q.shape
    tq = ATTN_Q_TILE
    pairs = B_HEADS // 2
    return pl.pallas_call(
        functools.partial(_attn_mla_kernel, n_lat_tiles=s_len // tq, s_len=s_len),
        grid=(b, pairs, p // tq),
        in_specs=[pl.BlockSpec((1, tq, 2 * LANES), lambda i, c, j: (i, j, c)),
                  pl.BlockSpec((1, p, 2 * LANES), lambda i, c, j: (i, 0, c)),
                  pl.BlockSpec((1, p, LANES), lambda i, c, j: (i, 0, c))],
        out_specs=pl.BlockSpec((1, tq, LANES), lambda i, c, j: (i, j, c)),
        out_shape=jax.ShapeDtypeStruct((b, p, B_HEADS * B_V), _BF16),
        compiler_params=_cparams("parallel", "parallel", "parallel"),
        name="attn_mla",
    )(q, k, v)


def _attn_diff_kernel(lam_ref, subln_ref, q_ref, k_ref, v_ref, o_ref, *, n_lat_tiles, s_len, lambda_init):
    j = pl.program_id(2)
    tq = q_ref.shape[1]
    lp = lam_ref[...]
    lam = (jnp.exp(jnp.sum(lp[0:1] * lp[1:2], axis=-1, keepdims=True))
           - jnp.exp(jnp.sum(lp[2:3] * lp[3:4], axis=-1, keepdims=True)) + lambda_init)
    q = q_ref[0]
    low = _lane_iota(q.shape) < C_HEAD_DIM
    zero = jnp.zeros_like(q)
    q2 = jnp.concatenate([jnp.where(low, q, zero), jnp.where(low, zero, q)], axis=0)

    def run(k_rows):
        o2, l2 = _softmax_pv(q2, k_ref[0, k_rows, :], v_ref[0, k_rows, :])
        o2 = o2 / l2
        o = o2[:tq] - lam * o2[tq:]
        o_ref[0] = (_rms(o, subln_ref[...]) * (1.0 - lambda_init)).astype(_BF16)

    @pl.when(j < n_lat_tiles)
    def _():
        run(slice(None))

    @pl.when(j >= n_lat_tiles)
    def _():
        run(slice(s_len, None))


def _attn_diff(qkv, lam_params, subln, s_len, lambda_init):
    b, p, _ = qkv.shape
    tq = ATTN_Q_TILE
    return pl.pallas_call(
        functools.partial(_attn_diff_kernel, n_lat_tiles=s_len // tq, s_len=s_len, lambda_init=lambda_init),
        grid=(b, C_HEADS, p // tq),
        in_specs=[pl.BlockSpec(lam_params.shape, lambda i, h, j: (0, 0)),
                  pl.BlockSpec((1, C_V_DIM), lambda i, h, j: (0, 0)),
                  pl.BlockSpec((1, tq, LANES), lambda i, h, j: (i, j, h)),
                  pl.BlockSpec((1, p, LANES), lambda i, h, j: (i, 0, C_HEADS + h)),
                  pl.BlockSpec((1, p, LANES), lambda i, h, j: (i, 0, 2 * C_HEADS + h))],
        out_specs=pl.BlockSpec((1, tq, LANES), lambda i, h, j: (i, j, h)),
        out_shape=jax.ShapeDtypeStruct((b, p, C_HEADS * C_V_DIM), _BF16),
        compiler_params=_cparams("parallel", "parallel", "parallel"),
        name="attn_diff",
    )(lam_params, subln.reshape(1, C_V_DIM), qkv, qkv, qkv)


def _layer_norm(z, g, b):
    mu = jnp.mean(z, axis=-1, keepdims=True)
    zc = z - mu
    var = jnp.mean(zc * zc, axis=-1, keepdims=True)
    return zc * lax.rsqrt(var + LN_EPS) * g + b


def _out_proj_kernel(o_ref, w_ref, x_ref, mod_ref, g_ref, b_ref, wr_ref, br_ref,
                     xo_ref, h_ref, lg_ref, *, alpha):
    y = jnp.dot(o_ref[0], w_ref[...], preferred_element_type=_F32)
    m = mod_ref[0]
    x = _layer_norm(alpha * x_ref[0] + m[2:3, :] * y, g_ref[...], b_ref[...])
    xo_ref[0] = x
    h = x * (1.0 + m[4:5, :]) + m[3:4, :]
    h_ref[0] = h.astype(_BF16)
    lg_ref[0] = jnp.dot(h, wr_ref[...], preferred_element_type=_F32,
                        precision=lax.Precision.HIGHEST) + br_ref[...]


def _out_proj(o, w_out, x, mods, ln_g, ln_b, w_router, b_router, alpha, n_lat_tiles):
    b, p, d = x.shape
    tm = ROW_TILE
    x_spec, mod_spec = _row_specs(b, p, d, n_lat_tiles, tm)
    full = lambda a: pl.BlockSpec(a.shape, lambda i, j: (0,) * a.ndim)
    n_o = o.shape[2]
    return pl.pallas_call(
        functools.partial(_out_proj_kernel, alpha=alpha),
        grid=(b, p // tm),
        in_specs=[pl.BlockSpec((1, tm, n_o), lambda i, j: (i, j, 0)), full(w_out), x_spec, mod_spec,
                  full(ln_g), full(ln_b), full(w_router), full(b_router)],
        out_specs=[x_spec, x_spec, pl.BlockSpec((1, tm, LANES), lambda i, j: (i, j, 0))],
        out_shape=[jax.ShapeDtypeStruct((b, p, d), _F32), jax.ShapeDtypeStruct((b, p, d), _BF16),
                   jax.ShapeDtypeStruct((b, p, LANES), _F32)],
        compiler_params=_cparams("parallel", "parallel"),
        name="out_proj_ln",
    )(o, w_out, x, mods, ln_g, ln_b, w_router, b_router)


def _ffn_kernel(be_ref, nb_ref, x_ref, wg_ref, wu_ref, wd_ref, y_ref):
    i = pl.program_id(0)

    @pl.when(i < nb_ref[0])
    def _():
        x = x_ref[...]
        g = jnp.dot(x, wg_ref[0], preferred_element_type=_F32)
        u = jnp.dot(x, wu_ref[0], preferred_element_type=_F32)
        a = (g * jax.nn.sigmoid(g) * u).astype(_BF16)
        y_ref[...] = jnp.dot(a, wd_ref[0], preferred_element_type=_F32)

    @pl.when(i >= nb_ref[0])
    def _():
        y_ref[...] = jnp.zeros_like(y_ref)


def _expert_ffn(xb, block_e, n_used, w_gate, w_up, w_down):
    rows, d = xb.shape
    de = w_gate.shape[2]
    bm = MOE_ROWS
    grid_spec = pltpu.PrefetchScalarGridSpec(
        num_scalar_prefetch=2,
        grid=(rows // bm,),
        in_specs=[pl.BlockSpec((bm, d), lambda i, be, nb: (i, 0)),
                  pl.BlockSpec((1, d, de), lambda i, be, nb: (be[i], 0, 0)),
                  pl.BlockSpec((1, d, de), lambda i, be, nb: (be[i], 0, 0)),
                  pl.BlockSpec((1, de, d), lambda i, be, nb: (be[i], 0, 0))],
        out_specs=pl.BlockSpec((bm, d), lambda i, be, nb: (i, 0)),
    )
    return pl.pallas_call(
        _ffn_kernel,
        grid_spec=grid_spec,
        out_shape=jax.ShapeDtypeStruct((rows, d), _F32),
        compiler_params=_cparams("arbitrary"),
        name="expert_ffn",
    )(block_e, n_used, xb, w_gate, w_up, w_down)


def _ffn_ln_kernel(y_ref, x_ref, mod_ref, g_ref, b_ref, xo_ref, *, alpha):
    m = mod_ref[0]
    xo_ref[0] = _layer_norm(alpha * x_ref[0] + m[5:6, :] * y_ref[0], g_ref[...], b_ref[...])


def _ffn_ln(y, x, mods, ln_g, ln_b, alpha, n_lat_tiles):
    b, p, d = x.shape
    tm = ROW_TILE
    x_spec, mod_spec = _row_specs(b, p, d, n_lat_tiles, tm)
    full = lambda a: pl.BlockSpec(a.shape, lambda i, j: (0,) * a.ndim)
    return pl.pallas_call(
        functools.partial(_ffn_ln_kernel, alpha=alpha),
        grid=(b, p // tm),
        in_specs=[x_spec, x_spec, mod_spec, full(ln_g), full(ln_b)],
        out_specs=x_spec,
        out_shape=jax.ShapeDtypeStruct((b, p, d), _F32),
        compiler_params=_cparams("parallel", "parallel"),
        name="ffn_ln",
    )(y, x, mods, ln_g, ln_b)


def _route(logits):
    t = logits.shape[0]
    lg = logits[:, :N_GROUPS]
    le = logits[:, N_GROUPS:N_GROUPS + N_EXPERTS].reshape(t, N_GROUPS, EXPERTS_PER_GROUP)
    p_grp = jax.nn.softmax(lg, axis=-1)
    g_sel = jnp.argmax(lg, axis=-1).astype(jnp.int32)
    p_g = jnp.take_along_axis(p_grp, g_sel[:, None], axis=-1)
    l_in = jnp.take_along_axis(le, g_sel[:, None, None], axis=1)[:, 0]
    top_v, top_i = lax.top_k(l_in, TOP_K)
    weights = p_g * jax.nn.softmax(top_v, axis=-1)
    expert = g_sel[:, None] * EXPERTS_PER_GROUP + top_i.astype(jnp.int32)
    return expert, weights


def _moe(h, logits, w_gate, w_up, w_down):
    t, d = h.shape
    bm = MOE_ROWS
    expert, weights = _route(logits)
    onehot = (expert[:, :, None] == jnp.arange(N_EXPERTS)[None, None, :]).sum(axis=1).astype(jnp.int32)
    csum = jnp.cumsum(onehot, axis=0)
    counts = csum[-1]
    rank = jnp.take_along_axis(csum - onehot, expert, axis=1)
    padded = (counts + bm - 1) // bm * bm
    pad_end = jnp.cumsum(padded)
    pad_start = pad_end - padded
    dest = pad_start[expert] + rank
    n_blocks = (t * TOP_K) // bm + N_EXPERTS
    rows = n_blocks * bm
    block_e = jnp.minimum(jnp.searchsorted(pad_end, jnp.arange(n_blocks) * bm, side="right"),
                          N_EXPERTS - 1).astype(jnp.int32)
    n_used = (pad_end[-1:] // bm).astype(jnp.int32)
    tok = jnp.full((rows,), t, jnp.int32).at[dest.reshape(-1)].set(
        jnp.repeat(jnp.arange(t, dtype=jnp.int32), TOP_K))
    xb = jnp.concatenate([h, jnp.zeros((1, d), h.dtype)], axis=0)[tok]
    yb = _expert_ffn(xb, block_e, n_used, w_gate, w_up, w_down)
    return weights[:, 0:1] * yb[dest[:, 0]] + weights[:, 1:2] * yb[dest[:, 1]]


def _rope_tables(s_len, l_ctx, rot_dim, lane_lo, period):
    rows = s_len // GRID_W
    row = jnp.repeat(jnp.arange(rows, dtype=_F32), GRID_W)
    col = jnp.tile(jnp.arange(GRID_W, dtype=_F32), rows)
    axis_dim = rot_dim // 2
    half = axis_dim // 2
    inv_freq = ROPE_THETA ** (-jnp.arange(0, axis_dim, 2, dtype=_F32) / axis_dim)
    lane = jnp.arange(LANES)
    rel = (lane % period) - lane_lo
    active = (rel >= 0) & (rel < rot_dim)
    rel = jnp.clip(rel, 0, rot_dim - 1)
    use_col = rel >= axis_dim
    f = inv_freq[(rel % axis_dim) % half]
    ang = jnp.where(use_col[None, :], col[:, None], row[:, None]) * f[None, :]
    sign = jnp.where((rel % axis_dim) < half, -1.0, 1.0)
    cos = jnp.where(active[None, :], jnp.cos(ang), 1.0)
    sin = jnp.where(active[None, :], jnp.sin(ang) * sign[None, :], 0.0)
    ident = jnp.ones((l_ctx, LANES), _F32)
    return (jnp.concatenate([cos, ident], axis=0), jnp.concatenate([sin, 0.0 * ident], axis=0))


def _win_weights(w_in):
    d = w_in.shape[0]
    nq, nkv = A_HEADS * A_HEAD_DIM, A_KV_HEADS * A_HEAD_DIM
    wq = w_in[:, :nq] * (A_HEAD_DIM ** -0.5)
    dup = lambda w: jnp.concatenate([w.reshape(d, A_KV_HEADS, 1, A_HEAD_DIM)] * 2, axis=2).reshape(d, 2 * nkv)
    return jnp.concatenate([wq, dup(w_in[:, nq:nq + nkv]), dup(w_in[:, nq + nkv:])], axis=1).astype(_BF16)


def _diff_weights(w_in):
    n = w_in.shape[1] // 3
    return jnp.concatenate([w_in[:, :n] * (C_HEAD_DIM ** -0.5), w_in[:, n:]], axis=1).astype(_BF16)


def _mla_weights(w_in, w_uq, w_ukv):
    d = w_in.shape[0]
    pad = LANES - B_NOPE - B_ROPE
    kr = jnp.concatenate([jnp.zeros((d, B_NOPE), _F32), w_in[:, B_Q_RANK + B_KV_RANK:],
                          jnp.zeros((d, pad), _F32)], axis=1)
    w_in_p = jnp.concatenate([w_in[:, :B_Q_RANK + B_KV_RANK], kr], axis=1).astype(_BF16)
    uq = w_uq.reshape(B_Q_RANK, B_HEADS, B_NOPE + B_ROPE)
    uq = jnp.pad(uq, ((0, 0), (0, 0), (0, pad))).reshape(B_Q_RANK, B_HEADS * LANES).astype(_BF16)
    ukv = w_ukv.reshape(B_KV_RANK, B_HEADS, B_NOPE + B_V)
    uk = jnp.pad(ukv[:, :, :B_NOPE], ((0, 0), (0, 0), (0, LANES - B_NOPE)))
    uk = uk.reshape(B_KV_RANK, B_HEADS * LANES).astype(_BF16)
    uv = ukv[:, :, B_NOPE:].reshape(B_KV_RANK, B_HEADS * B_V).astype(_BF16)
    return w_in_p, uq, uk, uv


def kernel(x, c, ctx, c_ctx, ada_w, ada_b, ln_mix_g, ln_mix_b, ln_ffn_g, ln_ffn_b, win_w_in, win_w_out, win_sink, mla_w_in, mla_q_norm, mla_kv_norm, mla_w_uq, mla_w_ukv, mla_w_out, dif_w_in, dif_lambda, dif_subln, dif_w_out, moe_w_grp, moe_b_grp, moe_w_rt, moe_b_rt, moe_w_gate, moe_w_up, moe_w_down):
    b, s_len, d = x.shape
    l_ctx = ctx.shape[1]
    depth = ada_w.shape[0]
    p = s_len + l_ctx
    assert s_len % ROW_TILE == 0 and l_ctx % ROW_TILE == 0 and s_len % GRID_W == 0
    assert s_len % ATTN_Q_TILE == 0 and l_ctx == ATTN_Q_TILE and s_len % l_ctx == 0
    alpha = (2.0 * depth) ** 0.25
    n_lat_tiles = s_len // ROW_TILE

    xs = jnp.concatenate([x, ctx], axis=1)
    cs = jnp.concatenate([c, c_ctx[None, :]], axis=0)
    mods = _ada_mods(cs, ada_w, ada_b).reshape(depth, b + 1, 6, d)
    cos64, sin64 = _rope_tables(s_len, l_ctx, A_HEAD_DIM, 0, A_HEAD_DIM)
    cos32, sin32 = _rope_tables(s_len, l_ctx, B_ROPE, B_NOPE, LANES)
    row = lambda v: v.reshape(1, -1)

    for i in range(depth):
        kind, slot = i % N_MIXERS, i // N_MIXERS
        if kind == 0:
            qkv = _proj_rope(xs, mods[i], _win_weights(win_w_in[slot]), cos64, sin64,
                             (A_HEADS + 2 * A_KV_HEADS) * A_HEAD_DIM, n_lat_tiles)
            o = _attn_win(qkv, win_sink[slot], s_len, l_ctx)
            w_out = win_w_out[slot]
        elif kind == 1:
            w_in_p, uq, uk, uv = _mla_weights(mla_w_in[slot], mla_w_uq[slot], mla_w_ukv[slot])
            q, k, v = _proj_mla(xs, mods[i], w_in_p, row(mla_q_norm[slot]), row(mla_kv_norm[slot]),
                                uq, uk, uv, cos32, sin32, n_lat_tiles)
            o = _attn_mla(q, k, v, s_len)
            w_out = mla_w_out[slot]
        else:
            lambda_init = 0.8 - 0.6 * math.exp(-0.3 * i)
            qkv = _proj_rope(xs, mods[i], _diff_weights(dif_w_in[slot]), cos64, sin64,
                             2 * C_HEADS * 2 * C_HEAD_DIM, n_lat_tiles)
            o = _attn_diff(qkv, dif_lambda[slot], dif_subln[slot], s_len, lambda_init)
            w_out = dif_w_out[slot]

        w_router = jnp.concatenate([moe_w_grp[i], moe_w_rt[i],
                                    jnp.zeros((d, LANES - N_GROUPS - N_EXPERTS), _F32)], axis=1)
        b_router = jnp.concatenate([moe_b_grp[i], moe_b_rt[i],
                                    jnp.zeros((LANES - N_GROUPS - N_EXPERTS,), _F32)])[None, :]
        xs, h, logits = _out_proj(o, w_out.astype(_BF16), xs, mods[i], row(ln_mix_g[i]), row(ln_mix_b[i]),
                                  w_router, b_router, alpha, n_lat_tiles)
        y = _moe(h.reshape(b * p, d), logits.reshape(b * p, LANES), moe_w_gate[i].astype(_BF16),
                 moe_w_up[i].astype(_BF16), moe_w_down[i].astype(_BF16))
        xs = _ffn_ln(y.reshape(b, p, d), xs, mods[i], row(ln_ffn_g[i]), row(ln_ffn_b[i]), alpha, n_lat_tiles)
    return xs[:, :s_len]
```

```python
import functools
import math

import jax
import jax.numpy as jnp
from jax import lax
from jax.experimental import pallas as pl
from jax.experimental.pallas import tpu as pltpu

LANES = 128
GRID_W = 64
ROPE_THETA = 10000.0
LN_EPS = 1e-5
RMS_EPS = 1e-6
LOG2E = math.log2(math.e)

A_HEADS, A_KV_HEADS, A_HEAD_DIM, A_WINDOW = 16, 4, 64, 128
A_GROUP = A_HEADS // A_KV_HEADS
B_HEADS, B_NOPE, B_ROPE, B_V, B_Q_RANK, B_KV_RANK = 16, 64, 32, 64, 512, 256
C_HEADS, C_HEAD_DIM = 8, 64
C_V_DIM = 2 * C_HEAD_DIM
N_GROUPS, EXPERTS_PER_GROUP, TOP_K = 4, 8, 2
N_EXPERTS = N_GROUPS * EXPERTS_PER_GROUP
N_MIXERS = 3

ROW_TILE = 256
ATTN_Q_TILE = 256
WIN_BLOCK = 128
KEY_CHUNK = 512
MOE_ROWS = 256
VMEM_LIMIT = 56 * 1024 * 1024

_F32 = jnp.float32
_BF16 = jnp.bfloat16
_NT = (((1,), (1,)), ((), ()))


def _cparams(*sem):
    return pltpu.CompilerParams(dimension_semantics=sem, vmem_limit_bytes=VMEM_LIMIT)


def _lane_iota(shape):
    return lax.broadcasted_iota(jnp.int32, shape, len(shape) - 1)


def _rope(a, cos, sin_signed, half):
    lane = _lane_iota(a.shape)
    swapped = jnp.where(lane % (2 * half) < half,
                        pltpu.roll(a, a.shape[-1] - half, axis=1),
                        pltpu.roll(a, half, axis=1))
    return a * cos + swapped * sin_signed


def _modulated(x_ref, mod_ref, shift_row):
    m = mod_ref[0]
    return x_ref[0] * (1.0 + m[shift_row + 1:shift_row + 2, :]) + m[shift_row:shift_row + 1, :]


def _ada_kernel(c_ref, w_ref, b_ref, o_ref):
    c = c_ref[...]
    s = c * jax.nn.sigmoid(c)
    o_ref[0] = jnp.dot(s, w_ref[0], preferred_element_type=_F32,
                       precision=lax.Precision.HIGHEST) + b_ref[0]


def _ada_mods(cs, ada_w, ada_b):
    depth, d, n = ada_w.shape
    r = cs.shape[0]
    tn = 1024
    return pl.pallas_call(
        _ada_kernel,
        grid=(depth, n // tn),
        in_specs=[pl.BlockSpec((r, d), lambda i, j: (0, 0)),
                  pl.BlockSpec((1, d, tn), lambda i, j: (i, 0, j)),
                  pl.BlockSpec((1, 1, tn), lambda i, j: (i, 0, j))],
        out_specs=pl.BlockSpec((1, r, tn), lambda i, j: (i, 0, j)),
        out_shape=jax.ShapeDtypeStruct((depth, r, n), _F32),
        compiler_params=_cparams("parallel", "parallel"),
        name="ada_mods",
    )(cs, ada_w, ada_b.reshape(depth, 1, n))


def _row_specs(b, p, d, n_lat_tiles, tm):
    x_spec = pl.BlockSpec((1, tm, d), lambda i, j: (i, j, 0))
    mod_spec = pl.BlockSpec((1, 6, d), lambda i, j: (jnp.where(j < n_lat_tiles, i, b), 0, 0))
    return x_spec, mod_spec


def _proj_rope_kernel(x_ref, mod_ref, w_ref, cos_ref, sin_ref, o_ref, *vt_ref, n_rope, n_q, q_scale):
    h = _modulated(x_ref, mod_ref, 0).astype(_BF16)
    acc = jnp.dot(h, w_ref[...], preferred_element_type=_F32)
    cos, sin = cos_ref[...], sin_ref[...]
    n = o_ref.shape[2]
    for j in range(n // LANES):
        a = acc[:, j * LANES:(j + 1) * LANES]
        if j * LANES < n_rope:
            a = _rope(a, cos, sin, 16)
        if j * LANES < n_q and q_scale != 1.0:
            a = a * q_scale
        o_ref[0, :, j * LANES:(j + 1) * LANES] = a.astype(_BF16)
    if vt_ref:
        vt_ref[0][0] = acc[:, n:].T.astype(_BF16)


def _proj_rope(x, mods, w, cos, sin, n_rope, n_t, n_lat_tiles, n_q=0, q_scale=1.0):
    b, p, d = x.shape
    n_all = w.shape[1]
    n = n_all - n_t
    tm = ROW_TILE
    x_spec, mod_spec = _row_specs(b, p, d, n_lat_tiles, tm)
    out_specs = [pl.BlockSpec((1, tm, n), lambda i, j: (i, j, 0))]
    out_shape = [jax.ShapeDtypeStruct((b, p, n), _BF16)]
    if n_t:
        out_specs.append(pl.BlockSpec((1, n_t, tm), lambda i, j: (i, 0, j)))
        out_shape.append(jax.ShapeDtypeStruct((b, n_t, p), _BF16))
    return pl.pallas_call(
        functools.partial(_proj_rope_kernel, n_rope=n_rope, n_q=n_q, q_scale=q_scale),
        grid=(b, p // tm),
        in_specs=[x_spec, mod_spec,
                  pl.BlockSpec((d, n_all), lambda i, j: (0, 0)),
                  pl.BlockSpec((tm, LANES), lambda i, j: (j, 0)),
                  pl.BlockSpec((tm, LANES), lambda i, j: (j, 0))],
        out_specs=out_specs,
        out_shape=out_shape,
        compiler_params=_cparams("parallel", "parallel"),
        name="proj_rope",
    )(x, mods, w, cos, sin)


def _rms(x, g):
    return x * lax.rsqrt(jnp.mean(x * x, axis=-1, keepdims=True) + RMS_EPS) * g


def _proj_mla_kernel(x_ref, mod_ref, win_ref, qn_ref, kvn_ref, wuq_ref, wuk_ref, wv_ref,
                     cos_ref, sin_ref, q_ref, k_ref, vt_ref, *, scale):
    h = _modulated(x_ref, mod_ref, 0).astype(_BF16)
    c = jnp.dot(h, win_ref[...], preferred_element_type=_F32)
    cq = _rms(c[:, :B_Q_RANK], qn_ref[...]).astype(_BF16)
    ckv = _rms(c[:, B_Q_RANK:B_Q_RANK + B_KV_RANK], kvn_ref[...]).astype(_BF16)
    cos, sin = cos_ref[...], sin_ref[...]
    k_rope = _rope(c[:, B_Q_RANK + B_KV_RANK:], cos, sin, 8)
    q = jnp.dot(cq, wuq_ref[...], preferred_element_type=_F32)
    k_nope = jnp.dot(ckv, wuk_ref[...], preferred_element_type=_F32)
    for hd in range(B_HEADS):
        sl = slice(hd * LANES, (hd + 1) * LANES)
        q_ref[0, :, sl] = (_rope(q[:, sl], cos, sin, 8) * scale).astype(_BF16)
        k_ref[0, :, sl] = (k_nope[:, sl] + k_rope).astype(_BF16)
    vt_ref[0] = jnp.dot(ckv, wv_ref[...], preferred_element_type=_F32).T.astype(_BF16)


def _proj_mla(x, mods, w_in, q_norm, kv_norm, w_uq, w_uk, w_v, cos, sin, n_lat_tiles):
    b, p, d = x.shape
    tm = ROW_TILE
    x_spec, mod_spec = _row_specs(b, p, d, n_lat_tiles, tm)
    full = lambda a: pl.BlockSpec(a.shape, lambda i, j: (0,) * a.ndim)
    nq, nv = w_uq.shape[1], w_v.shape[1]
    tile = lambda n: pl.BlockSpec((1, tm, n), lambda i, j: (i, j, 0))
    return pl.pallas_call(
        functools.partial(_proj_mla_kernel, scale=(B_NOPE + B_ROPE) ** -0.5 * LOG2E),
        grid=(b, p // tm),
        in_specs=[x_spec, mod_spec, full(w_in), full(q_norm), full(kv_norm), full(w_uq), full(w_uk),
                  full(w_v),
                  pl.BlockSpec((tm, LANES), lambda i, j: (j, 0)),
                  pl.BlockSpec((tm, LANES), lambda i, j: (j, 0))],
        out_specs=[tile(nq), tile(nq), pl.BlockSpec((1, nv, tm), lambda i, j: (i, 0, j))],
        out_shape=[jax.ShapeDtypeStruct((b, p, nq), _BF16), jax.ShapeDtypeStruct((b, p, nq), _BF16),
                   jax.ShapeDtypeStruct((b, nv, p), _BF16)],
        compiler_params=_cparams("parallel", "parallel"),
        name="proj_mla",
    )(x, mods, w_in, q_norm, kv_norm, w_uq, w_uk, w_v, cos, sin)


def _attn_win_kernel(sink_ref, q_ref, k0_ref, k1_ref, k2_ref, kc_ref, v0_ref, v1_ref, v2_ref, vc_ref,
                     o_ref, *, n_lat_tiles):
    j = pl.program_id(1)
    w = WIN_BLOCK
    n_ctx = kc_ref.shape[1]
    n_keys = 3 * w + n_ctx
    q_pos = j * w + lax.broadcasted_iota(jnp.int32, (A_GROUP * w, n_keys), 0) % w
    col = lax.broadcasted_iota(jnp.int32, (A_GROUP * w, n_keys), 1)
    k_pos = (j - 1) * w + col
    n_latent_keys = jnp.where(j < n_lat_tiles, n_lat_tiles * w, 0)
    valid = (col >= 3 * w) | ((jnp.abs(q_pos - k_pos) <= A_WINDOW) & (k_pos >= 0) & (k_pos < n_latent_keys))
    lane = _lane_iota((w, LANES))
    low = lane < A_HEAD_DIM
    for kvh in range(A_KV_HEADS):
        cs = slice(kvh * LANES, (kvh + 1) * LANES)
        kk = jnp.concatenate([k0_ref[0, :, cs], k1_ref[0, :, cs], k2_ref[0, :, cs], kc_ref[0, :, cs]], axis=0)
        vv = jnp.concatenate([v0_ref[0, :, cs], v1_ref[0, :, cs], v2_ref[0, :, cs], vc_ref[0, :, cs]], axis=0)
        qa = q_ref[0, :, 2 * kvh * LANES:(2 * kvh + 1) * LANES]
        qb = q_ref[0, :, (2 * kvh + 1) * LANES:(2 * kvh + 2) * LANES]
        zero = jnp.zeros_like(qa)
        q4 = jnp.concatenate([jnp.where(low, qa, zero), jnp.where(low, zero, qa),
                              jnp.where(low, qb, zero), jnp.where(low, zero, qb)], axis=0)
        sink = jnp.concatenate([jnp.full((w, 1), sink_ref[kvh * A_GROUP + g], _F32) for g in range(A_GROUP)],
                               axis=0)
        s = lax.dot_general(q4, kk, _NT, preferred_element_type=_F32)
        s = jnp.where(valid, s, -jnp.inf)
        m = jnp.maximum(jnp.max(s, axis=-1, keepdims=True), sink)
        e = jnp.exp(s - m)
        denom = jnp.sum(e, axis=-1, keepdims=True) + jnp.exp(sink - m)
        pr = (e / denom).astype(_BF16)
        o4 = jnp.dot(pr, vv, preferred_element_type=_F32)
        o_ref[0, :, 2 * kvh * LANES:(2 * kvh + 1) * LANES] = jnp.where(low, o4[0:w], o4[w:2 * w]).astype(_BF16)
        o_ref[0, :, (2 * kvh + 1) * LANES:(2 * kvh + 2) * LANES] = (
            jnp.where(low, o4[2 * w:3 * w], o4[3 * w:4 * w]).astype(_BF16))


def _attn_win(qkv, sink, s_len, l_ctx):
    b, p, _ = qkv.shape
    w = WIN_BLOCK
    n_lat = s_len // w
    nq = A_HEADS * A_HEAD_DIM
    kvw = A_KV_HEADS * LANES
    kcol, vcol = nq // kvw, nq // kvw + 1
    ctx_blk = s_len // l_ctx

    def nb(col, off):
        return pl.BlockSpec((1, w, kvw), lambda i, j: (i, jnp.clip(j + off, 0, n_lat - 1), col))

    def cb(col):
        return pl.BlockSpec((1, l_ctx, kvw), lambda i, j: (i, ctx_blk, col))

    return pl.pallas_call(
        functools.partial(_attn_win_kernel, n_lat_tiles=n_lat),
        grid=(b, p // w),
        in_specs=[pl.BlockSpec(memory_space=pltpu.SMEM),
                  pl.BlockSpec((1, w, nq), lambda i, j: (i, j, 0)),
                  nb(kcol, -1), nb(kcol, 0), nb(kcol, 1), cb(kcol),
                  nb(vcol, -1), nb(vcol, 0), nb(vcol, 1), cb(vcol)],
        out_specs=pl.BlockSpec((1, w, nq), lambda i, j: (i, j, 0)),
        out_shape=jax.ShapeDtypeStruct((b, p, nq), _BF16),
        compiler_params=_cparams("parallel", "parallel"),
        name="attn_win",
    )(sink, qkv, qkv, qkv, qkv, qkv, qkv, qkv, qkv, qkv)


def _key_chunks(s_len, l_ctx, latent_query):
    chunks = [(lo, KEY_CHUNK) for lo in range(0, s_len, KEY_CHUNK)] if latent_query else []
    return chunks + [(s_len, l_ctx)]


def _flash_keys_major(q, k_chunk, vt_chunk, chunks):
    def scores(c):
        return lax.dot_general(k_chunk(*chunks[c]), q, _NT, preferred_element_type=_F32)

    m = acc = l = None
    s_next = scores(0)
    for c, (lo, n) in enumerate(chunks):
        s = s_next
        if c + 1 < len(chunks):
            s_next = scores(c + 1)
        m_chunk = jnp.max(s, axis=0, keepdims=True)
        m_new = m_chunk if m is None else jnp.maximum(m, m_chunk)
        e = jnp.exp2(s - m_new)
        pv = jnp.dot(vt_chunk(lo, n), e.astype(_BF16), preferred_element_type=_F32)
        l_chunk = jnp.sum(e, axis=0, keepdims=True)
        if m is None:
            acc, l = pv, l_chunk
        else:
            corr = jnp.exp2(m - m_new)
            acc, l = acc * corr + pv, l * corr + l_chunk
        m = m_new
    return acc, l


def _attn_mla_kernel(q_ref, k_ref, vt_ref, o_ref, *, n_lat_tiles, s_len, l_ctx):
    j = pl.program_id(2)

    tq = q_ref.shape[1]
    q = q_ref[0]
    first = _lane_iota(q.shape) < LANES
    zero = jnp.zeros_like(q)
    q2 = jnp.concatenate([jnp.where(first, q, zero), jnp.where(first, zero, q)], axis=0)

    def run(latent_query):
        acc, l = _flash_keys_major(q2, lambda lo, n: k_ref[0, lo:lo + n, :],
                                   lambda lo, n: vt_ref[0, :, lo:lo + n],
                                   _key_chunks(s_len, l_ctx, latent_query))
        o2 = acc / l
        o = jnp.concatenate([o2[:B_V, :tq], o2[B_V:, tq:]], axis=0)
        o_ref[0] = o.T.astype(_BF16)

    @pl.when(j < n_lat_tiles)
    def _():
        run(True)

    @pl.when(j >= n_lat_tiles)
    def _():
        run(False)


def _attn_mla(q, k, vt, s_len):
    b, p, _ = q.shape
    tq = ATTN_Q_TILE
    pairs = B_HEADS // 2
    return pl.pallas_call(
        functools.partial(_attn_mla_kernel, n_lat_tiles=s_len // tq, s_len=s_len, l_ctx=p - s_len),
        grid=(b, pairs, p // tq),
        in_specs=[pl.BlockSpec((1, tq, 2 * LANES), lambda i, c, j: (i, j, c)),
                  pl.BlockSpec((1, p, 2 * LANES), lambda i, c, j: (i, 0, c)),
                  pl.BlockSpec((1, 2 * B_V, p), lambda i, c, j: (i, c, 0))],
        out_specs=pl.BlockSpec((1, tq, 2 * B_V), lambda i, c, j: (i, j, c)),
        out_shape=jax.ShapeDtypeStruct((b, p, B_HEADS * B_V), _BF16),
        compiler_params=_cparams("parallel", "parallel", "parallel"),
        name="attn_mla",
    )(q, k, vt)


def _attn_diff_kernel(lam_ref, subln_ref, q_ref, k_ref, vt_ref, o_ref, *, n_lat_tiles, s_len, l_ctx,
                      lambda_init):
    j = pl.program_id(2)
    tq = q_ref.shape[1]
    lp = lam_ref[...]
    lam = (jnp.exp(jnp.sum(lp[0:1] * lp[1:2], axis=-1, keepdims=True))
           - jnp.exp(jnp.sum(lp[2:3] * lp[3:4], axis=-1, keepdims=True)) + lambda_init)
    q = q_ref[0]
    low = _lane_iota(q.shape) < C_HEAD_DIM
    zero = jnp.zeros_like(q)
    q2 = jnp.concatenate([jnp.where(low, q, zero), jnp.where(low, zero, q)], axis=0)

    def run(latent_query):
        acc, l = _flash_keys_major(q2, lambda lo, n: k_ref[0, lo:lo + n, :],
                                   lambda lo, n: vt_ref[0, :, lo:lo + n],
                                   _key_chunks(s_len, l_ctx, latent_query))
        o2 = acc / l
        o = o2[:, :tq] - lam * o2[:, tq:]
        o = o * lax.rsqrt(jnp.mean(o * o, axis=0, keepdims=True) + RMS_EPS) * subln_ref[...]
        o_ref[0] = (o * (1.0 - lambda_init)).T.astype(_BF16)

    @pl.when(j < n_lat_tiles)
    def _():
        run(True)

    @pl.when(j >= n_lat_tiles)
    def _():
        run(False)


def _attn_diff(qk, vt, lam_params, subln, s_len, lambda_init):
    b, p, _ = qk.shape
    tq = ATTN_Q_TILE
    return pl.pallas_call(
        functools.partial(_attn_diff_kernel, n_lat_tiles=s_len // tq, s_len=s_len, l_ctx=p - s_len,
                          lambda_init=lambda_init),
        grid=(b, C_HEADS, p // tq),
        in_specs=[pl.BlockSpec(lam_params.shape, lambda i, h, j: (0, 0)),
                  pl.BlockSpec((C_V_DIM, 1), lambda i, h, j: (0, 0)),
                  pl.BlockSpec((1, tq, LANES), lambda i, h, j: (i, j, h)),
                  pl.BlockSpec((1, p, LANES), lambda i, h, j: (i, 0, C_HEADS + h)),
                  pl.BlockSpec((1, C_V_DIM, p), lambda i, h, j: (i, h, 0))],
        out_specs=pl.BlockSpec((1, tq, LANES), lambda i, h, j: (i, j, h)),
        out_shape=jax.ShapeDtypeStruct((b, p, C_HEADS * C_V_DIM), _BF16),
        compiler_params=_cparams("parallel", "parallel", "parallel"),
        name="attn_diff",
    )(lam_params, subln.reshape(C_V_DIM, 1), qk, qk, vt)


def _layer_norm(z, g, b):
    mu = jnp.mean(z, axis=-1, keepdims=True)
    zc = z - mu
    var = jnp.mean(zc * zc, axis=-1, keepdims=True)
    return zc * lax.rsqrt(var + LN_EPS) * g + b


def _out_proj_kernel(o_ref, w_ref, x_ref, mod_ref, g_ref, b_ref, wr_ref, br_ref,
                     xo_ref, h_ref, lg_ref, *, alpha):
    y = jnp.dot(o_ref[0], w_ref[...], preferred_element_type=_F32)
    m = mod_ref[0]
    x = _layer_norm(alpha * x_ref[0] + m[2:3, :] * y, g_ref[...], b_ref[...])
    xo_ref[0] = x
    h = x * (1.0 + m[4:5, :]) + m[3:4, :]
    h_ref[0] = h.astype(_BF16)
    lg_ref[0] = jnp.dot(h, wr_ref[...], preferred_element_type=_F32,
                        precision=lax.Precision.HIGHEST) + br_ref[...]


def _out_proj(o, w_out, x, mods, ln_g, ln_b, w_router, b_router, alpha, n_lat_tiles):
    b, p, d = x.shape
    tm = ROW_TILE
    x_spec, mod_spec = _row_specs(b, p, d, n_lat_tiles, tm)
    full = lambda a: pl.BlockSpec(a.shape, lambda i, j: (0,) * a.ndim)
    n_o = o.shape[2]
    return pl.pallas_call(
        functools.partial(_out_proj_kernel, alpha=alpha),
        grid=(b, p // tm),
        in_specs=[pl.BlockSpec((1, tm, n_o), lambda i, j: (i, j, 0)), full(w_out), x_spec, mod_spec,
                  full(ln_g), full(ln_b), full(w_router), full(b_router)],
        out_specs=[x_spec, x_spec, pl.BlockSpec((1, tm, LANES), lambda i, j: (i, j, 0))],
        out_shape=[jax.ShapeDtypeStruct((b, p, d), _F32), jax.ShapeDtypeStruct((b, p, d), _BF16),
                   jax.ShapeDtypeStruct((b, p, LANES), _F32)],
        compiler_params=_cparams("parallel", "parallel"),
        name="out_proj_ln",
    )(o, w_out, x, mods, ln_g, ln_b, w_router, b_router)


def _ffn_kernel(be_ref, nb_ref, x_ref, wg_ref, wu_ref, wd_ref, y_ref):
    i = pl.program_id(0)

    @pl.when(i < nb_ref[0])
    def _():
        x = x_ref[...]
        g = jnp.dot(x, wg_ref[0], preferred_element_type=_F32)
        u = jnp.dot(x, wu_ref[0], preferred_element_type=_F32)
        a = (g * jax.nn.sigmoid(g) * u).astype(_BF16)
        y_ref[...] = jnp.dot(a, wd_ref[0], preferred_element_type=_F32)

    @pl.when(i >= nb_ref[0])
    def _():
        y_ref[...] = jnp.zeros_like(y_ref)


def _expert_ffn(xb, block_e, n_used, w_gate, w_up, w_down):
    rows, d = xb.shape
    de = w_gate.shape[2]
    bm = MOE_ROWS
    grid_spec = pltpu.PrefetchScalarGridSpec(
        num_scalar_prefetch=2,
        grid=(rows // bm,),
        in_specs=[pl.BlockSpec((bm, d), lambda i, be, nb: (i, 0)),
                  pl.BlockSpec((1, d, de), lambda i, be, nb: (be[i], 0, 0)),
                  pl.BlockSpec((1, d, de), lambda i, be, nb: (be[i], 0, 0)),
                  pl.BlockSpec((1, de, d), lambda i, be, nb: (be[i], 0, 0))],
        out_specs=pl.BlockSpec((bm, d), lambda i, be, nb: (i, 0)),
    )
    return pl.pallas_call(
        _ffn_kernel,
        grid_spec=grid_spec,
        out_shape=jax.ShapeDtypeStruct((rows, d), _F32),
        compiler_params=_cparams("arbitrary"),
        name="expert_ffn",
    )(block_e, n_used, xb, w_gate, w_up, w_down)


def _ffn_ln_kernel(y_ref, x_ref, mod_ref, g_ref, b_ref, xo_ref, *, alpha):
    m = mod_ref[0]
    xo_ref[0] = _layer_norm(alpha * x_ref[0] + m[5:6, :] * y_ref[0], g_ref[...], b_ref[...])


def _ffn_ln(y, x, mods, ln_g, ln_b, alpha, n_lat_tiles):
    b, p, d = x.shape
    tm = ROW_TILE
    x_spec, mod_spec = _row_specs(b, p, d, n_lat_tiles, tm)
    full = lambda a: pl.BlockSpec(a.shape, lambda i, j: (0,) * a.ndim)
    return pl.pallas_call(
        functools.partial(_ffn_ln_kernel, alpha=alpha),
        grid=(b, p // tm),
        in_specs=[x_spec, x_spec, mod_spec, full(ln_g), full(ln_b)],
        out_specs=x_spec,
        out_shape=jax.ShapeDtypeStruct((b, p, d), _F32),
        compiler_params=_cparams("parallel", "parallel"),
        name="ffn_ln",
    )(y, x, mods, ln_g, ln_b)


def _route(logits):
    t = logits.shape[0]
    lg = logits[:, :N_GROUPS]
    le = logits[:, N_GROUPS:N_GROUPS + N_EXPERTS].reshape(t, N_GROUPS, EXPERTS_PER_GROUP)
    p_grp = jax.nn.softmax(lg, axis=-1)
    g_sel = jnp.argmax(lg, axis=-1).astype(jnp.int32)
    p_g = jnp.take_along_axis(p_grp, g_sel[:, None], axis=-1)
    l_in = jnp.take_along_axis(le, g_sel[:, None, None], axis=1)[:, 0]
    top_v, top_i = lax.top_k(l_in, TOP_K)
    weights = p_g * jax.nn.softmax(top_v, axis=-1)
    expert = g_sel[:, None] * EXPERTS_PER_GROUP + top_i.astype(jnp.int32)
    return expert, weights


def _moe(h, logits, w_gate, w_up, w_down):
    t, d = h.shape
    bm = MOE_ROWS
    expert, weights = _route(logits)
    onehot = (expert[:, :, None] == jnp.arange(N_EXPERTS)[None, None, :]).sum(axis=1).astype(jnp.int32)
    csum = jnp.cumsum(onehot, axis=0)
    counts = csum[-1]
    rank = jnp.take_along_axis(csum - onehot, expert, axis=1)
    padded = (counts + bm - 1) // bm * bm
    pad_end = jnp.cumsum(padded)
    pad_start = pad_end - padded
    dest = pad_start[expert] + rank
    n_blocks = (t * TOP_K) // bm + N_EXPERTS
    rows = n_blocks * bm
    block_e = jnp.minimum(jnp.searchsorted(pad_end, jnp.arange(n_blocks) * bm, side="right"),
                          N_EXPERTS - 1).astype(jnp.int32)
    n_used = (pad_end[-1:] // bm).astype(jnp.int32)
    tok = jnp.full((rows,), t, jnp.int32).at[dest.reshape(-1)].set(
        jnp.repeat(jnp.arange(t, dtype=jnp.int32), TOP_K))
    xb = jnp.concatenate([h, jnp.zeros((1, d), h.dtype)], axis=0)[tok]
    yb = _expert_ffn(xb, block_e, n_used, w_gate, w_up, w_down)
    return weights[:, 0:1] * yb[dest[:, 0]] + weights[:, 1:2] * yb[dest[:, 1]]


def _rope_tables(s_len, l_ctx, rot_dim, lane_lo, period):
    rows = s_len // GRID_W
    row = jnp.repeat(jnp.arange(rows, dtype=_F32), GRID_W)
    col = jnp.tile(jnp.arange(GRID_W, dtype=_F32), rows)
    axis_dim = rot_dim // 2
    half = axis_dim // 2
    inv_freq = ROPE_THETA ** (-jnp.arange(0, axis_dim, 2, dtype=_F32) / axis_dim)
    lane = jnp.arange(LANES)
    rel = (lane % period) - lane_lo
    active = (rel >= 0) & (rel < rot_dim)
    rel = jnp.clip(rel, 0, rot_dim - 1)
    use_col = rel >= axis_dim
    f = inv_freq[(rel % axis_dim) % half]
    ang = jnp.where(use_col[None, :], col[:, None], row[:, None]) * f[None, :]
    sign = jnp.where((rel % axis_dim) < half, -1.0, 1.0)
    cos = jnp.where(active[None, :], jnp.cos(ang), 1.0)
    sin = jnp.where(active[None, :], jnp.sin(ang) * sign[None, :], 0.0)
    ident = jnp.ones((l_ctx, LANES), _F32)
    return (jnp.concatenate([cos, ident], axis=0), jnp.concatenate([sin, 0.0 * ident], axis=0))


def _win_weights(w_in):
    d = w_in.shape[0]
    nq, nkv = A_HEADS * A_HEAD_DIM, A_KV_HEADS * A_HEAD_DIM
    wq = w_in[:, :nq] * (A_HEAD_DIM ** -0.5)
    dup = lambda w: jnp.concatenate([w.reshape(d, A_KV_HEADS, 1, A_HEAD_DIM)] * 2, axis=2).reshape(d, 2 * nkv)
    return jnp.concatenate([wq, dup(w_in[:, nq:nq + nkv]), dup(w_in[:, nq + nkv:])], axis=1).astype(_BF16)


def _mla_weights(w_in, w_uq, w_ukv):
    d = w_in.shape[0]
    pad = LANES - B_NOPE - B_ROPE
    kr = jnp.concatenate([jnp.zeros((d, B_NOPE), _F32), w_in[:, B_Q_RANK + B_KV_RANK:],
                          jnp.zeros((d, pad), _F32)], axis=1)
    w_in_p = jnp.concatenate([w_in[:, :B_Q_RANK + B_KV_RANK], kr], axis=1).astype(_BF16)
    uq = w_uq.reshape(B_Q_RANK, B_HEADS, B_NOPE + B_ROPE)
    uq = jnp.pad(uq, ((0, 0), (0, 0), (0, pad))).reshape(B_Q_RANK, B_HEADS * LANES).astype(_BF16)
    ukv = w_ukv.reshape(B_KV_RANK, B_HEADS, B_NOPE + B_V)
    uk = jnp.pad(ukv[:, :, :B_NOPE], ((0, 0), (0, 0), (0, LANES - B_NOPE)))
    uk = uk.reshape(B_KV_RANK, B_HEADS * LANES).astype(_BF16)
    uv = ukv[:, :, B_NOPE:].reshape(B_KV_RANK, B_HEADS * B_V).astype(_BF16)
    return w_in_p, uq, uk, uv


def kernel(x, c, ctx, c_ctx, ada_w, ada_b, ln_mix_g, ln_mix_b, ln_ffn_g, ln_ffn_b, win_w_in, win_w_out, win_sink, mla_w_in, mla_q_norm, mla_kv_norm, mla_w_uq, mla_w_ukv, mla_w_out, dif_w_in, dif_lambda, dif_subln, dif_w_out, moe_w_grp, moe_b_grp, moe_w_rt, moe_b_rt, moe_w_gate, moe_w_up, moe_w_down):
    b, s_len, d = x.shape
    l_ctx = ctx.shape[1]
    depth = ada_w.shape[0]
    p = s_len + l_ctx
    assert s_len % ROW_TILE == 0 and l_ctx % ROW_TILE == 0 and s_len % GRID_W == 0
    assert s_len % ATTN_Q_TILE == 0 and l_ctx == ATTN_Q_TILE and s_len % l_ctx == 0
    alpha = (2.0 * depth) ** 0.25
    n_lat_tiles = s_len // ROW_TILE

    xs = jnp.concatenate([x, ctx], axis=1)
    cs = jnp.concatenate([c, c_ctx[None, :]], axis=0)
    mods = _ada_mods(cs, ada_w, ada_b).reshape(depth, b + 1, 6, d)
    cos64, sin64 = _rope_tables(s_len, l_ctx, A_HEAD_DIM, 0, A_HEAD_DIM)
    cos32, sin32 = _rope_tables(s_len, l_ctx, B_ROPE, B_NOPE, LANES)
    row = lambda v: v.reshape(1, -1)

    for i in range(depth):
        kind, slot = i % N_MIXERS, i // N_MIXERS
        if kind == 0:
            (qkv,) = _proj_rope(xs, mods[i], _win_weights(win_w_in[slot]), cos64, sin64,
                                (A_HEADS + 2 * A_KV_HEADS) * A_HEAD_DIM, 0, n_lat_tiles)
            o = _attn_win(qkv, win_sink[slot], s_len, l_ctx)
            w_out = win_w_out[slot]
        elif kind == 1:
            w_in_p, uq, uk, uv = _mla_weights(mla_w_in[slot], mla_w_uq[slot], mla_w_ukv[slot])
            q, k, vt = _proj_mla(xs, mods[i], w_in_p, row(mla_q_norm[slot]), row(mla_kv_norm[slot]),
                                uq, uk, uv, cos32, sin32, n_lat_tiles)
            o = _attn_mla(q, k, vt, s_len)
            w_out = mla_w_out[slot]
        else:
            lambda_init = 0.8 - 0.6 * math.exp(-0.3 * i)
            qk, vt = _proj_rope(xs, mods[i], dif_w_in[slot].astype(_BF16), cos64, sin64,
                                2 * C_HEADS * 2 * C_HEAD_DIM, C_HEADS * C_V_DIM, n_lat_tiles,
                                n_q=C_HEADS * 2 * C_HEAD_DIM, q_scale=C_HEAD_DIM ** -0.5 * LOG2E)
            o = _attn_diff(qk, vt, dif_lambda[slot], dif_subln[slot], s_len, lambda_init)
            w_out = dif_w_out[slot]

        w_router = jnp.concatenate([moe_w_grp[i], moe_w_rt[i],
                                    jnp.zeros((d, LANES - N_GROUPS - N_EXPERTS), _F32)], axis=1)
        b_router = jnp.concatenate([moe_b_grp[i], moe_b_rt[i],
                                    jnp.zeros((LANES - N_GROUPS - N_EXPERTS,), _F32)])[None, :]
        xs, h, logits = _out_proj(o, w_out.astype(_BF16), xs, mods[i], row(ln_mix_g[i]), row(ln_mix_b[i]),
                                  w_router, b_router, alpha, n_lat_tiles)
        y = _moe(h.reshape(b * p, d), logits.reshape(b * p, LANES), moe_w_gate[i].astype(_BF16),
                 moe_w_up[i].astype(_BF16), moe_w_down[i].astype(_BF16))
        xs = _ffn_ln(y.reshape(b, p, d), xs, mods[i], row(ln_ffn_g[i]), row(ln_ffn_b[i]), alpha, n_lat_tiles)
    return xs[:, :s_len]
```

```python
import functools
import math

import jax
import jax.numpy as jnp
from jax import lax
from jax.experimental import pallas as pl
from jax.experimental.pallas import tpu as pltpu

LANES = 128
GRID_W = 64
ROPE_THETA = 10000.0
LN_EPS = 1e-5
RMS_EPS = 1e-6
LOG2E = math.log2(math.e)

A_HEADS, A_KV_HEADS, A_HEAD_DIM, A_WINDOW = 16, 4, 64, 128
A_GROUP = A_HEADS // A_KV_HEADS
B_HEADS, B_NOPE, B_ROPE, B_V, B_Q_RANK, B_KV_RANK = 16, 64, 32, 64, 512, 256
C_HEADS, C_HEAD_DIM = 8, 64
C_V_DIM = 2 * C_HEAD_DIM
N_GROUPS, EXPERTS_PER_GROUP, TOP_K = 4, 8, 2
N_EXPERTS = N_GROUPS * EXPERTS_PER_GROUP
N_MIXERS = 3

ROW_TILE = 256
ATTN_Q_TILE = 256
WIN_BLOCK = 128
KEY_CHUNK = 512
MOE_ROWS = 256
ROUTE_TILE = 512
VMEM_LIMIT = 56 * 1024 * 1024

_F32 = jnp.float32
_BF16 = jnp.bfloat16
_NT = (((1,), (1,)), ((), ()))


def _cparams(*sem):
    return pltpu.CompilerParams(dimension_semantics=sem, vmem_limit_bytes=VMEM_LIMIT)


def _lane_iota(shape):
    return lax.broadcasted_iota(jnp.int32, shape, len(shape) - 1)


def _rope(a, cos, sin_signed, half):
    lane = _lane_iota(a.shape)
    swapped = jnp.where(lane % (2 * half) < half,
                        pltpu.roll(a, a.shape[-1] - half, axis=1),
                        pltpu.roll(a, half, axis=1))
    return a * cos + swapped * sin_signed


def _modulated(x_ref, mod_ref, shift_row):
    m = mod_ref[0]
    return x_ref[0] * (1.0 + m[shift_row + 1:shift_row + 2, :]) + m[shift_row:shift_row + 1, :]


def _ada_kernel(c_ref, w_ref, b_ref, o_ref):
    c = c_ref[...]
    s = c * jax.nn.sigmoid(c)
    o_ref[0] = jnp.dot(s, w_ref[0], preferred_element_type=_F32,
                       precision=lax.Precision.HIGHEST) + b_ref[0]


def _ada_mods(cs, ada_w, ada_b):
    depth, d, n = ada_w.shape
    r = cs.shape[0]
    tn = 1024
    return pl.pallas_call(
        _ada_kernel,
        grid=(depth, n // tn),
        in_specs=[pl.BlockSpec((r, d), lambda i, j: (0, 0)),
                  pl.BlockSpec((1, d, tn), lambda i, j: (i, 0, j)),
                  pl.BlockSpec((1, 1, tn), lambda i, j: (i, 0, j))],
        out_specs=pl.BlockSpec((1, r, tn), lambda i, j: (i, 0, j)),
        out_shape=jax.ShapeDtypeStruct((depth, r, n), _F32),
        compiler_params=_cparams("parallel", "parallel"),
        name="ada_mods",
    )(cs, ada_w, ada_b.reshape(depth, 1, n))


def _row_specs(b, p, d, n_lat_tiles, tm):
    x_spec = pl.BlockSpec((1, tm, d), lambda i, j: (i, j, 0))
    mod_spec = pl.BlockSpec((1, 6, d), lambda i, j: (jnp.where(j < n_lat_tiles, i, b), 0, 0))
    return x_spec, mod_spec


def _proj_rope_kernel(x_ref, mod_ref, w_ref, cos_ref, sin_ref, o_ref, *vt_ref, n_rope, n_q, q_scale):
    h = _modulated(x_ref, mod_ref, 0).astype(_BF16)
    acc = jnp.dot(h, w_ref[...], preferred_element_type=_F32)
    cos, sin = cos_ref[...], sin_ref[...]
    n = o_ref.shape[2]
    for j in range(n // LANES):
        a = acc[:, j * LANES:(j + 1) * LANES]
        if j * LANES < n_rope:
            a = _rope(a, cos, sin, 16)
        if j * LANES < n_q and q_scale != 1.0:
            a = a * q_scale
        o_ref[0, :, j * LANES:(j + 1) * LANES] = a.astype(_BF16)
    if vt_ref:
        vt_ref[0][0] = acc[:, n:].T.astype(_BF16)


def _proj_rope(x, mods, w, cos, sin, n_rope, n_t, n_lat_tiles, n_q=0, q_scale=1.0):
    b, p, d = x.shape
    n_all = w.shape[1]
    n = n_all - n_t
    tm = ROW_TILE
    x_spec, mod_spec = _row_specs(b, p, d, n_lat_tiles, tm)
    out_specs = [pl.BlockSpec((1, tm, n), lambda i, j: (i, j, 0))]
    out_shape = [jax.ShapeDtypeStruct((b, p, n), _BF16)]
    if n_t:
        out_specs.append(pl.BlockSpec((1, n_t, tm), lambda i, j: (i, 0, j)))
        out_shape.append(jax.ShapeDtypeStruct((b, n_t, p), _BF16))
    return pl.pallas_call(
        functools.partial(_proj_rope_kernel, n_rope=n_rope, n_q=n_q, q_scale=q_scale),
        grid=(b, p // tm),
        in_specs=[x_spec, mod_spec,
                  pl.BlockSpec((d, n_all), lambda i, j: (0, 0)),
                  pl.BlockSpec((tm, LANES), lambda i, j: (j, 0)),
                  pl.BlockSpec((tm, LANES), lambda i, j: (j, 0))],
        out_specs=out_specs,
        out_shape=out_shape,
        compiler_params=_cparams("parallel", "parallel"),
        name="proj_rope",
    )(x, mods, w, cos, sin)


def _rms(x, g):
    return x * lax.rsqrt(jnp.mean(x * x, axis=-1, keepdims=True) + RMS_EPS) * g


def _proj_mla_kernel(x_ref, mod_ref, win_ref, qn_ref, kvn_ref, wuq_ref, wuk_ref, wv_ref,
                     cos_ref, sin_ref, q_ref, k_ref, vt_ref, *, scale):
    h = _modulated(x_ref, mod_ref, 0).astype(_BF16)
    c = jnp.dot(h, win_ref[...], preferred_element_type=_F32)
    cq = _rms(c[:, :B_Q_RANK], qn_ref[...]).astype(_BF16)
    ckv = _rms(c[:, B_Q_RANK:B_Q_RANK + B_KV_RANK], kvn_ref[...]).astype(_BF16)
    cos, sin = cos_ref[...], sin_ref[...]
    k_rope = _rope(c[:, B_Q_RANK + B_KV_RANK:], cos, sin, 8)
    q = jnp.dot(cq, wuq_ref[...], preferred_element_type=_F32)
    k_nope = jnp.dot(ckv, wuk_ref[...], preferred_element_type=_F32)
    for hd in range(B_HEADS):
        sl = slice(hd * LANES, (hd + 1) * LANES)
        q_ref[0, :, sl] = (_rope(q[:, sl], cos, sin, 8) * scale).astype(_BF16)
        k_ref[0, :, sl] = (k_nope[:, sl] + k_rope).astype(_BF16)
    vt_ref[0] = jnp.dot(ckv, wv_ref[...], preferred_element_type=_F32).T.astype(_BF16)


def _proj_mla(x, mods, w_in, q_norm, kv_norm, w_uq, w_uk, w_v, cos, sin, n_lat_tiles):
    b, p, d = x.shape
    tm = ROW_TILE
    x_spec, mod_spec = _row_specs(b, p, d, n_lat_tiles, tm)
    full = lambda a: pl.BlockSpec(a.shape, lambda i, j: (0,) * a.ndim)
    nq, nv = w_uq.shape[1], w_v.shape[1]
    tile = lambda n: pl.BlockSpec((1, tm, n), lambda i, j: (i, j, 0))
    return pl.pallas_call(
        functools.partial(_proj_mla_kernel, scale=(B_NOPE + B_ROPE) ** -0.5 * LOG2E),
        grid=(b, p // tm),
        in_specs=[x_spec, mod_spec, full(w_in), full(q_norm), full(kv_norm), full(w_uq), full(w_uk),
                  full(w_v),
                  pl.BlockSpec((tm, LANES), lambda i, j: (j, 0)),
                  pl.BlockSpec((tm, LANES), lambda i, j: (j, 0))],
        out_specs=[tile(nq), tile(nq), pl.BlockSpec((1, nv, tm), lambda i, j: (i, 0, j))],
        out_shape=[jax.ShapeDtypeStruct((b, p, nq), _BF16), jax.ShapeDtypeStruct((b, p, nq), _BF16),
                   jax.ShapeDtypeStruct((b, nv, p), _BF16)],
        compiler_params=_cparams("parallel", "parallel"),
        name="proj_mla",
    )(x, mods, w_in, q_norm, kv_norm, w_uq, w_uk, w_v, cos, sin)


def _attn_win_kernel(sink_ref, q_ref, k0_ref, k1_ref, k2_ref, kc_ref, v0_ref, v1_ref, v2_ref, vc_ref,
                     o_ref, *, n_lat_tiles):
    j = pl.program_id(1)
    w = WIN_BLOCK
    n_ctx = kc_ref.shape[1]
    n_keys = 3 * w + n_ctx
    q_pos = j * w + lax.broadcasted_iota(jnp.int32, (A_GROUP * w, n_keys), 0) % w
    col = lax.broadcasted_iota(jnp.int32, (A_GROUP * w, n_keys), 1)
    k_pos = (j - 1) * w + col
    n_latent_keys = jnp.where(j < n_lat_tiles, n_lat_tiles * w, 0)
    valid = (col >= 3 * w) | ((jnp.abs(q_pos - k_pos) <= A_WINDOW) & (k_pos >= 0) & (k_pos < n_latent_keys))
    lane = _lane_iota((w, LANES))
    low = lane < A_HEAD_DIM
    for kvh in range(A_KV_HEADS):
        cs = slice(kvh * LANES, (kvh + 1) * LANES)
        kk = jnp.concatenate([k0_ref[0, :, cs], k1_ref[0, :, cs], k2_ref[0, :, cs], kc_ref[0, :, cs]], axis=0)
        vv = jnp.concatenate([v0_ref[0, :, cs], v1_ref[0, :, cs], v2_ref[0, :, cs], vc_ref[0, :, cs]], axis=0)
        qa = q_ref[0, :, 2 * kvh * LANES:(2 * kvh + 1) * LANES]
        qb = q_ref[0, :, (2 * kvh + 1) * LANES:(2 * kvh + 2) * LANES]
        zero = jnp.zeros_like(qa)
        q4 = jnp.concatenate([jnp.where(low, qa, zero), jnp.where(low, zero, qa),
                              jnp.where(low, qb, zero), jnp.where(low, zero, qb)], axis=0)
        sink = jnp.concatenate([jnp.full((w, 1), sink_ref[kvh * A_GROUP + g], _F32) for g in range(A_GROUP)],
                               axis=0)
        s = lax.dot_general(q4, kk, _NT, preferred_element_type=_F32)
        s = jnp.where(valid, s, -jnp.inf)
        m = jnp.maximum(jnp.max(s, axis=-1, keepdims=True), sink)
        e = jnp.exp(s - m)
        denom = jnp.sum(e, axis=-1, keepdims=True) + jnp.exp(sink - m)
        pr = (e / denom).astype(_BF16)
        o4 = jnp.dot(pr, vv, preferred_element_type=_F32)
        o_ref[0, :, 2 * kvh * LANES:(2 * kvh + 1) * LANES] = jnp.where(low, o4[0:w], o4[w:2 * w]).astype(_BF16)
        o_ref[0, :, (2 * kvh + 1) * LANES:(2 * kvh + 2) * LANES] = (
            jnp.where(low, o4[2 * w:3 * w], o4[3 * w:4 * w]).astype(_BF16))


def _attn_win(qkv, sink, s_len, l_ctx):
    b, p, _ = qkv.shape
    w = WIN_BLOCK
    n_lat = s_len // w
    nq = A_HEADS * A_HEAD_DIM
    kvw = A_KV_HEADS * LANES
    kcol, vcol = nq // kvw, nq // kvw + 1
    ctx_blk = s_len // l_ctx

    def nb(col, off):
        return pl.BlockSpec((1, w, kvw), lambda i, j: (i, jnp.clip(j + off, 0, n_lat - 1), col))

    def cb(col):
        return pl.BlockSpec((1, l_ctx, kvw), lambda i, j: (i, ctx_blk, col))

    return pl.pallas_call(
        functools.partial(_attn_win_kernel, n_lat_tiles=n_lat),
        grid=(b, p // w),
        in_specs=[pl.BlockSpec(memory_space=pltpu.SMEM),
                  pl.BlockSpec((1, w, nq), lambda i, j: (i, j, 0)),
                  nb(kcol, -1), nb(kcol, 0), nb(kcol, 1), cb(kcol),
                  nb(vcol, -1), nb(vcol, 0), nb(vcol, 1), cb(vcol)],
        out_specs=pl.BlockSpec((1, w, nq), lambda i, j: (i, j, 0)),
        out_shape=jax.ShapeDtypeStruct((b, p, nq), _BF16),
        compiler_params=_cparams("parallel", "parallel"),
        name="attn_win",
    )(sink, qkv, qkv, qkv, qkv, qkv, qkv, qkv, qkv, qkv)


def _key_chunks(s_len, l_ctx, latent_query):
    chunks = [(lo, KEY_CHUNK) for lo in range(0, s_len, KEY_CHUNK)] if latent_query else []
    return chunks + [(s_len, l_ctx)]


def _flash_keys_major(q, k_chunk, vt_chunk, chunks):
    def scores(c):
        return lax.dot_general(k_chunk(*chunks[c]), q, _NT, preferred_element_type=_F32)

    m = acc = l = None
    pending = None
    s_next = scores(0)

    def flush(acc):
        (lo, n), e, corr = pending
        pv = jnp.dot(vt_chunk(lo, n), e, preferred_element_type=_F32)
        return pv if acc is None else acc * corr + pv

    for c in range(len(chunks)):
        s = s_next
        if c + 1 < len(chunks):
            s_next = scores(c + 1)
        if pending is not None:
            acc = flush(acc)
        m_chunk = jnp.max(s, axis=0, keepdims=True)
        m_new = m_chunk if m is None else jnp.maximum(m, m_chunk)
        e = jnp.exp2(s - m_new)
        l_chunk = jnp.sum(e, axis=0, keepdims=True)
        corr = None if m is None else jnp.exp2(m - m_new)
        l = l_chunk if m is None else l * corr + l_chunk
        pending = (chunks[c], e.astype(_BF16), corr)
        m = m_new
    return flush(acc), l


def _attn_mla_kernel(q_ref, k_ref, vt_ref, o_ref, *, n_lat_tiles, s_len, l_ctx):
    j = pl.program_id(2)

    tq = q_ref.shape[1]
    q = q_ref[0]
    first = _lane_iota(q.shape) < LANES
    zero = jnp.zeros_like(q)
    q2 = jnp.concatenate([jnp.where(first, q, zero), jnp.where(first, zero, q)], axis=0)

    def run(latent_query):
        acc, l = _flash_keys_major(q2, lambda lo, n: k_ref[0, lo:lo + n, :],
                                   lambda lo, n: vt_ref[0, :, lo:lo + n],
                                   _key_chunks(s_len, l_ctx, latent_query))
        o2 = acc / l
        o = jnp.concatenate([o2[:B_V, :tq], o2[B_V:, tq:]], axis=0)
        o_ref[0] = o.T.astype(_BF16)

    @pl.when(j < n_lat_tiles)
    def _():
        run(True)

    @pl.when(j >= n_lat_tiles)
    def _():
        run(False)


def _attn_mla(q, k, vt, s_len):
    b, p, _ = q.shape
    tq = ATTN_Q_TILE
    pairs = B_HEADS // 2
    return pl.pallas_call(
        functools.partial(_attn_mla_kernel, n_lat_tiles=s_len // tq, s_len=s_len, l_ctx=p - s_len),
        grid=(b, pairs, p // tq),
        in_specs=[pl.BlockSpec((1, tq, 2 * LANES), lambda i, c, j: (i, j, c)),
                  pl.BlockSpec((1, p, 2 * LANES), lambda i, c, j: (i, 0, c)),
                  pl.BlockSpec((1, 2 * B_V, p), lambda i, c, j: (i, c, 0))],
        out_specs=pl.BlockSpec((1, tq, 2 * B_V), lambda i, c, j: (i, j, c)),
        out_shape=jax.ShapeDtypeStruct((b, p, B_HEADS * B_V), _BF16),
        compiler_params=_cparams("parallel", "parallel", "parallel"),
        name="attn_mla",
    )(q, k, vt)


def _attn_diff_kernel(lam_ref, subln_ref, q_ref, k_ref, vt_ref, o_ref, *, n_lat_tiles, s_len, l_ctx,
                      lambda_init):
    j = pl.program_id(2)
    tq = q_ref.shape[1]
    lp = lam_ref[...]
    lam = (jnp.exp(jnp.sum(lp[0:1] * lp[1:2], axis=-1, keepdims=True))
           - jnp.exp(jnp.sum(lp[2:3] * lp[3:4], axis=-1, keepdims=True)) + lambda_init)
    q = q_ref[0]
    low = _lane_iota(q.shape) < C_HEAD_DIM
    zero = jnp.zeros_like(q)
    q2 = jnp.concatenate([jnp.where(low, q, zero), jnp.where(low, zero, q)], axis=0)

    def run(latent_query):
        acc, l = _flash_keys_major(q2, lambda lo, n: k_ref[0, lo:lo + n, :],
                                   lambda lo, n: vt_ref[0, :, lo:lo + n],
                                   _key_chunks(s_len, l_ctx, latent_query))
        o2 = acc / l
        o = o2[:, :tq] - lam * o2[:, tq:]
        o = o * lax.rsqrt(jnp.mean(o * o, axis=0, keepdims=True) + RMS_EPS) * subln_ref[...]
        o_ref[0] = (o * (1.0 - lambda_init)).T.astype(_BF16)

    @pl.when(j < n_lat_tiles)
    def _():
        run(True)

    @pl.when(j >= n_lat_tiles)
    def _():
        run(False)


def _attn_diff(qk, vt, lam_params, subln, s_len, lambda_init):
    b, p, _ = qk.shape
    tq = ATTN_Q_TILE
    return pl.pallas_call(
        functools.partial(_attn_diff_kernel, n_lat_tiles=s_len // tq, s_len=s_len, l_ctx=p - s_len,
                          lambda_init=lambda_init),
        grid=(b, C_HEADS, p // tq),
        in_specs=[pl.BlockSpec(lam_params.shape, lambda i, h, j: (0, 0)),
                  pl.BlockSpec((C_V_DIM, 1), lambda i, h, j: (0, 0)),
                  pl.BlockSpec((1, tq, LANES), lambda i, h, j: (i, j, h)),
                  pl.BlockSpec((1, p, LANES), lambda i, h, j: (i, 0, C_HEADS + h)),
                  pl.BlockSpec((1, C_V_DIM, p), lambda i, h, j: (i, h, 0))],
        out_specs=pl.BlockSpec((1, tq, LANES), lambda i, h, j: (i, j, h)),
        out_shape=jax.ShapeDtypeStruct((b, p, C_HEADS * C_V_DIM), _BF16),
        compiler_params=_cparams("parallel", "parallel", "parallel"),
        name="attn_diff",
    )(lam_params, subln.reshape(C_V_DIM, 1), qk, qk, vt)


def _layer_norm(z, g, b):
    mu = jnp.mean(z, axis=-1, keepdims=True)
    zc = z - mu
    var = jnp.mean(zc * zc, axis=-1, keepdims=True)
    return zc * lax.rsqrt(var + LN_EPS) * g + b


def _out_proj_kernel(o_ref, w_ref, x_ref, mod_ref, g_ref, b_ref, wr_ref, br_ref,
                     xo_ref, h_ref, lg_ref, *, alpha):
    y = jnp.dot(o_ref[0], w_ref[...], preferred_element_type=_F32)
    m = mod_ref[0]
    x = _layer_norm(alpha * x_ref[0] + m[2:3, :] * y, g_ref[...], b_ref[...])
    xo_ref[0] = x
    h = x * (1.0 + m[4:5, :]) + m[3:4, :]
    h_ref[0] = h
    lg_ref[0] = jnp.dot(h, wr_ref[...], preferred_element_type=_F32,
                        precision=lax.Precision.HIGHEST) + br_ref[...]


def _out_proj(o, w_out, x, mods, ln_g, ln_b, w_router, b_router, alpha, n_lat_tiles):
    b, p, d = x.shape
    tm = ROW_TILE
    x_spec, mod_spec = _row_specs(b, p, d, n_lat_tiles, tm)
    full = lambda a: pl.BlockSpec(a.shape, lambda i, j: (0,) * a.ndim)
    n_o = o.shape[2]
    return pl.pallas_call(
        functools.partial(_out_proj_kernel, alpha=alpha),
        grid=(b, p // tm),
        in_specs=[pl.BlockSpec((1, tm, n_o), lambda i, j: (i, j, 0)), full(w_out), x_spec, mod_spec,
                  full(ln_g), full(ln_b), full(w_router), full(b_router)],
        out_specs=[x_spec, x_spec, pl.BlockSpec((1, tm, LANES), lambda i, j: (i, j, 0))],
        out_shape=[jax.ShapeDtypeStruct((b, p, d), _F32), jax.ShapeDtypeStruct((b, p, d), _F32),
                   jax.ShapeDtypeStruct((b, p, LANES), _F32)],
        compiler_params=_cparams("parallel", "parallel"),
        name="out_proj_ln",
    )(o, w_out, x, mods, ln_g, ln_b, w_router, b_router)


def _route_kernel(lg_ref, info_ref, btab_ref, cnt_ref, start_ref, run_ref, *, block_rows):
    sweep, i = pl.program_id(0), pl.program_id(1)
    x = lg_ref[...]
    tm = x.shape[0]
    lane = _lane_iota(x.shape)
    xg = jnp.where(lane < N_GROUPS, x, -jnp.inf)
    g_max = jnp.max(xg, axis=-1, keepdims=True)
    g_sel = jnp.min(jnp.where(xg == g_max, lane, LANES), axis=-1, keepdims=True)
    p_g = 1.0 / jnp.sum(jnp.exp(xg - g_max), axis=-1, keepdims=True)
    lo = N_GROUPS + g_sel * EXPERTS_PER_GROUP
    xe = jnp.where((lane >= lo) & (lane < lo + EXPERTS_PER_GROUP), x, -jnp.inf)
    v1 = jnp.max(xe, axis=-1, keepdims=True)
    i1 = jnp.min(jnp.where(xe == v1, lane, LANES), axis=-1, keepdims=True)
    xe = jnp.where(lane == i1, -jnp.inf, xe)
    v2 = jnp.max(xe, axis=-1, keepdims=True)
    i2 = jnp.min(jnp.where(xe == v2, lane, LANES), axis=-1, keepdims=True)
    hit1, hit2 = lane == i1, lane == i2
    onehot = (hit1 | hit2).astype(_F32)

    @pl.when((sweep == 0) & (i == 0))
    def _():
        cnt_ref[...] = jnp.zeros_like(cnt_ref)

    @pl.when(sweep == 0)
    def _():
        cnt_ref[...] += jnp.sum(onehot, axis=0, keepdims=True)

    @pl.when((sweep == 1) & (i == 0))
    def _():
        cnt = cnt_ref[...]
        padded = jnp.floor((cnt + (block_rows - 1)) * (1.0 / block_rows)) * block_rows
        r = lax.broadcasted_iota(jnp.int32, (LANES, LANES), 0)
        c = lax.broadcasted_iota(jnp.int32, (LANES, LANES), 1)
        before = (r < c).astype(_F32)
        start = jnp.dot(jnp.broadcast_to(padded, (8, LANES)), before, preferred_element_type=_F32,
                        precision=lax.Precision.HIGHEST)[0:1]
        start_ref[...] = start
        run_ref[...] = jnp.zeros_like(run_ref)
        nb = btab_ref.shape[0]
        row0 = (lax.broadcasted_iota(jnp.int32, (nb, LANES), 0) * block_rows).astype(_F32)
        is_e = (_lane_iota((nb, LANES)) >= N_GROUPS) & (_lane_iota((nb, LANES)) < N_GROUPS + N_EXPERTS)
        end = start + padded
        owner = jnp.sum(jnp.where(is_e & (end <= row0), 1.0, 0.0), axis=-1, keepdims=True)
        inside = is_e & (start <= row0) & (row0 < end)
        real = jnp.sum(jnp.where(inside, jnp.clip(cnt - (row0 - start), 0.0, block_rows), 0.0),
                       axis=-1, keepdims=True)
        bl = _lane_iota((nb, LANES))
        btab_ref[...] = jnp.where(bl == 0, jnp.minimum(owner, N_EXPERTS - 1.0),
                                  jnp.where(bl == 1, real, 0.0)).astype(jnp.int32)

    @pl.when(sweep == 1)
    def _():
        row = lax.broadcasted_iota(jnp.int32, (tm, tm), 0)
        col = lax.broadcasted_iota(jnp.int32, (tm, tm), 1)
        earlier = (col < row).astype(_BF16)
        rank = jnp.dot(earlier, onehot.astype(_BF16), preferred_element_type=_F32) + run_ref[...]
        pos = rank + start_ref[...]
        d1 = jnp.sum(jnp.where(hit1, pos, 0.0), axis=-1, keepdims=True)
        d2 = jnp.sum(jnp.where(hit2, pos, 0.0), axis=-1, keepdims=True)
        run_ref[...] += jnp.sum(onehot, axis=0, keepdims=True)
        t = jnp.exp(v2 - v1)
        w1 = p_g / (1.0 + t)
        info_ref[...] = jnp.where(lane == 0, d1, jnp.where(lane == 1, d2, jnp.where(lane == 2, w1,
                                  jnp.where(lane == 3, w1 * t, 0.0))))


def _route(logits, n_blocks):
    t = logits.shape[0]
    tm = ROUTE_TILE
    nb_pad = -(-n_blocks // 8) * 8
    return pl.pallas_call(
        functools.partial(_route_kernel, block_rows=MOE_ROWS),
        grid=(2, t // tm),
        in_specs=[pl.BlockSpec((tm, LANES), lambda s, i: (i, 0))],
        out_specs=[pl.BlockSpec((tm, LANES), lambda s, i: (i * s, 0)),
                   pl.BlockSpec((nb_pad, LANES), lambda s, i: (0, 0))],
        out_shape=[jax.ShapeDtypeStruct((t, LANES), _F32), jax.ShapeDtypeStruct((nb_pad, LANES), jnp.int32)],
        scratch_shapes=[pltpu.VMEM((1, LANES), _F32)] * 3,
        compiler_params=_cparams("arbitrary", "arbitrary"),
        name="moe_route",
    )(logits)


def _row_copy_wait(src_rows, dst_rows, sem):
    pltpu.make_async_copy(src_rows, dst_rows, sem).wait()


def _dispatch_kernel(dest_ref, h_ref, xb_in_ref, xb_ref, sem):
    del xb_in_ref
    tm = h_ref.shape[0]
    base = pl.program_id(0) * tm

    def body(r, carry):
        for k in range(TOP_K):
            d = dest_ref[(base + r) * TOP_K + k]
            pltpu.make_async_copy(h_ref.at[pl.ds(r, 1)], xb_ref.at[pl.ds(d, 1)], sem).start()
        return carry

    lax.fori_loop(0, tm, body, 0)
    for k in range(TOP_K):
        _row_copy_wait(h_ref, xb_ref.at[pl.ds(0, tm)], sem)


def _dispatch(dest, h, rows):
    t, d = h.shape
    tm = ROW_TILE
    grid_spec = pltpu.PrefetchScalarGridSpec(
        num_scalar_prefetch=1,
        grid=(t // tm,),
        in_specs=[pl.BlockSpec((tm, d), lambda i, dest: (i, 0)),
                  pl.BlockSpec(memory_space=pl.ANY)],
        out_specs=pl.BlockSpec(memory_space=pl.ANY),
        scratch_shapes=[pltpu.SemaphoreType.DMA(())],
    )
    return pl.pallas_call(
        _dispatch_kernel,
        grid_spec=grid_spec,
        out_shape=jax.ShapeDtypeStruct((rows, d), h.dtype),
        input_output_aliases={2: 0},
        compiler_params=_cparams("arbitrary"),
        name="moe_dispatch",
    )(dest, h, jnp.zeros((rows, d), h.dtype))


def _ffn_kernel(btab_ref, x_ref, wg_ref, wu_ref, wd_ref, y_ref, wg_bf, wu_bf, wd_bf):
    i = pl.program_id(0)
    expert, real = btab_ref[i, 0], btab_ref[i, 1]
    prev = btab_ref[jnp.maximum(i - 1, 0), 0]

    @pl.when((i == 0) | (expert != prev))
    def _():
        wg_bf[...] = wg_ref[0].astype(_BF16)
        wu_bf[...] = wu_ref[0].astype(_BF16)
        wd_bf[...] = wd_ref[0].astype(_BF16)

    @pl.when(real > 0)
    def _():
        x = x_ref[...].astype(_BF16)
        g = jnp.dot(x, wg_bf[...], preferred_element_type=_F32)
        u = jnp.dot(x, wu_bf[...], preferred_element_type=_F32)
        a = (g * jax.nn.sigmoid(g) * u).astype(_BF16)
        y_ref[...] = jnp.dot(a, wd_bf[...], preferred_element_type=_F32)

    @pl.when(real <= 0)
    def _():
        y_ref[...] = jnp.zeros_like(y_ref)


def _expert_ffn(btab, xb, w_gate, w_up, w_down):
    rows, d = xb.shape
    de = w_gate.shape[2]
    bm = MOE_ROWS
    grid_spec = pltpu.PrefetchScalarGridSpec(
        num_scalar_prefetch=1,
        grid=(rows // bm,),
        in_specs=[pl.BlockSpec((bm, d), lambda i, bt: (i, 0)),
                  pl.BlockSpec((1, d, de), lambda i, bt: (bt[i, 0], 0, 0)),
                  pl.BlockSpec((1, d, de), lambda i, bt: (bt[i, 0], 0, 0)),
                  pl.BlockSpec((1, de, d), lambda i, bt: (bt[i, 0], 0, 0))],
        out_specs=pl.BlockSpec((bm, d), lambda i, bt: (i, 0)),
        scratch_shapes=[pltpu.VMEM((d, de), _BF16), pltpu.VMEM((d, de), _BF16), pltpu.VMEM((de, d), _BF16)],
    )
    return pl.pallas_call(
        _ffn_kernel,
        grid_spec=grid_spec,
        out_shape=jax.ShapeDtypeStruct((rows, d), _F32),
        compiler_params=_cparams("arbitrary"),
        name="expert_ffn",
    )(btab, xb, w_gate, w_up, w_down)


def _combine_kernel(dest_ref, info_ref, x_ref, mod_ref, g_ref, b_ref, yb_ref, xo_ref, buf, sem, *, alpha):
    tm = x_ref.shape[1]
    nj = pl.num_programs(1)
    step = pl.program_id(0) * nj + pl.program_id(1)
    n_steps = pl.num_programs(0) * nj

    def issue(at_step, slot):
        base = at_step * tm

        def body(r, carry):
            for k in range(TOP_K):
                d = dest_ref[(base + r) * TOP_K + k]
                pltpu.make_async_copy(yb_ref.at[pl.ds(d, 1)], buf.at[slot, k, pl.ds(r, 1)], sem.at[slot]).start()
            return carry

        lax.fori_loop(0, tm, body, 0)

    @pl.when(step == 0)
    def _():
        issue(0, 0)

    @pl.when(step + 1 < n_steps)
    def _():
        issue(step + 1, (step + 1) % 2)

    slot = step % 2
    for k in range(TOP_K):
        _row_copy_wait(yb_ref.at[pl.ds(0, tm)], buf.at[slot, k], sem.at[slot])
    info = info_ref[0]
    y = info[:, 2:3] * buf[slot, 0] + info[:, 3:4] * buf[slot, 1]
    m = mod_ref[0]
    xo_ref[0] = _layer_norm(alpha * x_ref[0] + m[5:6, :] * y, g_ref[...], b_ref[...])


def _combine(dest, info, yb, x, mods, ln_g, ln_b, alpha, n_lat_tiles):
    b, p, d = x.shape
    tm = ROW_TILE
    grid_spec = pltpu.PrefetchScalarGridSpec(
        num_scalar_prefetch=1,
        grid=(b, p // tm),
        in_specs=[pl.BlockSpec((1, tm, LANES), lambda i, j, dest: (i, j, 0)),
                  pl.BlockSpec((1, tm, d), lambda i, j, dest: (i, j, 0)),
                  pl.BlockSpec((1, 6, d), lambda i, j, dest: (jnp.where(j < n_lat_tiles, i, b), 0, 0)),
                  pl.BlockSpec(ln_g.shape, lambda i, j, dest: (0, 0)),
                  pl.BlockSpec(ln_b.shape, lambda i, j, dest: (0, 0)),
                  pl.BlockSpec(memory_space=pl.ANY)],
        out_specs=pl.BlockSpec((1, tm, d), lambda i, j, dest: (i, j, 0)),
        scratch_shapes=[pltpu.VMEM((2, TOP_K, tm, d), _F32), pltpu.SemaphoreType.DMA((2,))],
    )
    return pl.pallas_call(
        functools.partial(_combine_kernel, alpha=alpha),
        grid_spec=grid_spec,
        out_shape=jax.ShapeDtypeStruct((b, p, d), _F32),
        compiler_params=_cparams("arbitrary", "arbitrary"),
        name="moe_combine_ln",
    )(dest, info.reshape(b, p, LANES), x, mods, ln_g, ln_b, yb)


def _moe_layer(h, logits, x, mods, ln_g, ln_b, w_gate, w_up, w_down, alpha, n_lat_tiles):
    t, d = h.shape
    n_blocks = (t * TOP_K) // MOE_ROWS + N_EXPERTS
    info, btab = _route(logits, n_blocks)
    dest = info[:, :TOP_K].astype(jnp.int32).reshape(t * TOP_K)
    xb = _dispatch(dest, h, n_blocks * MOE_ROWS)
    yb = _expert_ffn(btab[:n_blocks], xb, w_gate, w_up, w_down)
    return _combine(dest, info, yb, x, mods, ln_g, ln_b, alpha, n_lat_tiles)


def _rope_tables(s_len, l_ctx, rot_dim, lane_lo, period):
    rows = s_len // GRID_W
    row = jnp.repeat(jnp.arange(rows, dtype=_F32), GRID_W)
    col = jnp.tile(jnp.arange(GRID_W, dtype=_F32), rows)
    axis_dim = rot_dim // 2
    half = axis_dim // 2
    inv_freq = ROPE_THETA ** (-jnp.arange(0, axis_dim, 2, dtype=_F32) / axis_dim)
    lane = jnp.arange(LANES)
    rel = (lane % period) - lane_lo
    active = (rel >= 0) & (rel < rot_dim)
    rel = jnp.clip(rel, 0, rot_dim - 1)
    use_col = rel >= axis_dim
    f = inv_freq[(rel % axis_dim) % half]
    ang = jnp.where(use_col[None, :], col[:, None], row[:, None]) * f[None, :]
    sign = jnp.where((rel % axis_dim) < half, -1.0, 1.0)
    cos = jnp.where(active[None, :], jnp.cos(ang), 1.0)
    sin = jnp.where(active[None, :], jnp.sin(ang) * sign[None, :], 0.0)
    ident = jnp.ones((l_ctx, LANES), _F32)
    return (jnp.concatenate([cos, ident], axis=0), jnp.concatenate([sin, 0.0 * ident], axis=0))


def _win_weights(w_in):
    d = w_in.shape[0]
    nq, nkv = A_HEADS * A_HEAD_DIM, A_KV_HEADS * A_HEAD_DIM
    wq = w_in[:, :nq] * (A_HEAD_DIM ** -0.5)
    dup = lambda w: jnp.concatenate([w.reshape(d, A_KV_HEADS, 1, A_HEAD_DIM)] * 2, axis=2).reshape(d, 2 * nkv)
    return jnp.concatenate([wq, dup(w_in[:, nq:nq + nkv]), dup(w_in[:, nq + nkv:])], axis=1).astype(_BF16)


def _mla_weights(w_in, w_uq, w_ukv):
    d = w_in.shape[0]
    pad = LANES - B_NOPE - B_ROPE
    kr = jnp.concatenate([jnp.zeros((d, B_NOPE), _F32), w_in[:, B_Q_RANK + B_KV_RANK:],
                          jnp.zeros((d, pad), _F32)], axis=1)
    w_in_p = jnp.concatenate([w_in[:, :B_Q_RANK + B_KV_RANK], kr], axis=1).astype(_BF16)
    uq = w_uq.reshape(B_Q_RANK, B_HEADS, B_NOPE + B_ROPE)
    uq = jnp.pad(uq, ((0, 0), (0, 0), (0, pad))).reshape(B_Q_RANK, B_HEADS * LANES).astype(_BF16)
    ukv = w_ukv.reshape(B_KV_RANK, B_HEADS, B_NOPE + B_V)
    uk = jnp.pad(ukv[:, :, :B_NOPE], ((0, 0), (0, 0), (0, LANES - B_NOPE)))
    uk = uk.reshape(B_KV_RANK, B_HEADS * LANES).astype(_BF16)
    uv = ukv[:, :, B_NOPE:].reshape(B_KV_RANK, B_HEADS * B_V).astype(_BF16)
    return w_in_p, uq, uk, uv


def kernel(x, c, ctx, c_ctx, ada_w, ada_b, ln_mix_g, ln_mix_b, ln_ffn_g, ln_ffn_b, win_w_in, win_w_out, win_sink, mla_w_in, mla_q_norm, mla_kv_norm, mla_w_uq, mla_w_ukv, mla_w_out, dif_w_in, dif_lambda, dif_subln, dif_w_out, moe_w_grp, moe_b_grp, moe_w_rt, moe_b_rt, moe_w_gate, moe_w_up, moe_w_down):
    b, s_len, d = x.shape
    l_ctx = ctx.shape[1]
    depth = ada_w.shape[0]
    p = s_len + l_ctx
    assert s_len % ROW_TILE == 0 and l_ctx % ROW_TILE == 0 and s_len % GRID_W == 0
    assert s_len % ATTN_Q_TILE == 0 and l_ctx == ATTN_Q_TILE and s_len % l_ctx == 0
    alpha = (2.0 * depth) ** 0.25
    n_lat_tiles = s_len // ROW_TILE

    xs = jnp.concatenate([x, ctx], axis=1)
    cs = jnp.concatenate([c, c_ctx[None, :]], axis=0)
    mods = _ada_mods(cs, ada_w, ada_b).reshape(depth, b + 1, 6, d)
    cos64, sin64 = _rope_tables(s_len, l_ctx, A_HEAD_DIM, 0, A_HEAD_DIM)
    cos32, sin32 = _rope_tables(s_len, l_ctx, B_ROPE, B_NOPE, LANES)
    row = lambda v: v.reshape(1, -1)

    for i in range(depth):
        kind, slot = i % N_MIXERS, i // N_MIXERS
        if kind == 0:
            (qkv,) = _proj_rope(xs, mods[i], _win_weights(win_w_in[slot]), cos64, sin64,
                                (A_HEADS + 2 * A_KV_HEADS) * A_HEAD_DIM, 0, n_lat_tiles)
            o = _attn_win(qkv, win_sink[slot], s_len, l_ctx)
            w_out = win_w_out[slot]
        elif kind == 1:
            w_in_p, uq, uk, uv = _mla_weights(mla_w_in[slot], mla_w_uq[slot], mla_w_ukv[slot])
            q, k, vt = _proj_mla(xs, mods[i], w_in_p, row(mla_q_norm[slot]), row(mla_kv_norm[slot]),
                                uq, uk, uv, cos32, sin32, n_lat_tiles)
            o = _attn_mla(q, k, vt, s_len)
            w_out = mla_w_out[slot]
        else:
            lambda_init = 0.8 - 0.6 * math.exp(-0.3 * i)
            qk, vt = _proj_rope(xs, mods[i], dif_w_in[slot].astype(_BF16), cos64, sin64,
                                2 * C_HEADS * 2 * C_HEAD_DIM, C_HEADS * C_V_DIM, n_lat_tiles,
                                n_q=C_HEADS * 2 * C_HEAD_DIM, q_scale=C_HEAD_DIM ** -0.5 * LOG2E)
            o = _attn_diff(qk, vt, dif_lambda[slot], dif_subln[slot], s_len, lambda_init)
            w_out = dif_w_out[slot]

        w_router = jnp.concatenate([moe_w_grp[i], moe_w_rt[i],
                                    jnp.zeros((d, LANES - N_GROUPS - N_EXPERTS), _F32)], axis=1)
        b_router = jnp.concatenate([moe_b_grp[i], moe_b_rt[i],
                                    jnp.zeros((LANES - N_GROUPS - N_EXPERTS,), _F32)])[None, :]
        xs, h, logits = _out_proj(o, w_out.astype(_BF16), xs, mods[i], row(ln_mix_g[i]), row(ln_mix_b[i]),
                                  w_router, b_router, alpha, n_lat_tiles)
        xs = _moe_layer(h.reshape(b * p, d), logits.reshape(b * p, LANES), xs, mods[i], row(ln_ffn_g[i]),
                        row(ln_ffn_b[i]), moe_w_gate[i], moe_w_up[i], moe_w_down[i], alpha, n_lat_tiles)
    return xs[:, :s_len]
```

```python
import functools
import math

import jax
import jax.numpy as jnp
from jax import lax
from jax.experimental import pallas as pl
from jax.experimental.pallas import tpu as pltpu

LANES = 128
GRID_W = 64
ROPE_THETA = 10000.0
LN_EPS = 1e-5
RMS_EPS = 1e-6
LOG2E = math.log2(math.e)

A_HEADS, A_KV_HEADS, A_HEAD_DIM, A_WINDOW = 16, 4, 64, 128
A_GROUP = A_HEADS // A_KV_HEADS
B_HEADS, B_NOPE, B_ROPE, B_V, B_Q_RANK, B_KV_RANK = 16, 64, 32, 64, 512, 256
C_HEADS, C_HEAD_DIM = 8, 64
C_V_DIM = 2 * C_HEAD_DIM
N_GROUPS, EXPERTS_PER_GROUP, TOP_K = 4, 8, 2
N_EXPERTS = N_GROUPS * EXPERTS_PER_GROUP
N_MIXERS = 3

ROW_TILE = 256
ATTN_Q_TILE = 256
WIN_BLOCK = 128
KEY_CHUNK = 512
MOE_ROWS = 256
ROUTE_TILE = 512
VMEM_LIMIT = 56 * 1024 * 1024

_F32 = jnp.float32
_BF16 = jnp.bfloat16
_NT = (((1,), (1,)), ((), ()))


def _cparams(*sem, **kw):
    return pltpu.CompilerParams(dimension_semantics=sem, vmem_limit_bytes=VMEM_LIMIT, **kw)


def _lane_iota(shape):
    return lax.broadcasted_iota(jnp.int32, shape, len(shape) - 1)


def _rope(a, cos, sin_signed, half):
    lane = _lane_iota(a.shape)
    swapped = jnp.where(lane % (2 * half) < half,
                        pltpu.roll(a, a.shape[-1] - half, axis=1),
                        pltpu.roll(a, half, axis=1))
    return a * cos + swapped * sin_signed


def _modulated(x_ref, mod_ref, shift_row):
    m = mod_ref[0]
    return x_ref[0] * (1.0 + m[shift_row + 1:shift_row + 2, :]) + m[shift_row:shift_row + 1, :]


def _ada_kernel(c_ref, w_ref, b_ref, o_ref):
    c = c_ref[...]
    s = c * jax.nn.sigmoid(c)
    o_ref[0] = jnp.dot(s, w_ref[0], preferred_element_type=_F32,
                       precision=lax.Precision.HIGHEST) + b_ref[0]


def _ada_mods(cs, ada_w, ada_b):
    depth, d, n = ada_w.shape
    r = cs.shape[0]
    tn = 1024
    return pl.pallas_call(
        _ada_kernel,
        grid=(depth, n // tn),
        in_specs=[pl.BlockSpec((r, d), lambda i, j: (0, 0)),
                  pl.BlockSpec((1, d, tn), lambda i, j: (i, 0, j)),
                  pl.BlockSpec((1, 1, tn), lambda i, j: (i, 0, j))],
        out_specs=pl.BlockSpec((1, r, tn), lambda i, j: (i, 0, j)),
        out_shape=jax.ShapeDtypeStruct((depth, r, n), _F32),
        compiler_params=_cparams("parallel", "parallel"),
        name="ada_mods",
    )(cs, ada_w, ada_b.reshape(depth, 1, n))


def _row_specs(b, p, d, n_lat_tiles, tm):
    x_spec = pl.BlockSpec((1, tm, d), lambda i, j: (i, j, 0))
    mod_spec = pl.BlockSpec((1, 6, d), lambda i, j: (jnp.where(j < n_lat_tiles, i, b), 0, 0))
    return x_spec, mod_spec


def _proj_rope_kernel(x_ref, mod_ref, w_ref, cos_ref, sin_ref, o_ref, *vt_ref, n_rope, n_q, q_scale):
    h = _modulated(x_ref, mod_ref, 0).astype(_BF16)
    acc = jnp.dot(h, w_ref[...], preferred_element_type=_F32)
    cos, sin = cos_ref[...], sin_ref[...]
    n = o_ref.shape[2]
    for j in range(n // LANES):
        a = acc[:, j * LANES:(j + 1) * LANES]
        if j * LANES < n_rope:
            a = _rope(a, cos, sin, 16)
        if j * LANES < n_q and q_scale != 1.0:
            a = a * q_scale
        o_ref[0, :, j * LANES:(j + 1) * LANES] = a.astype(_BF16)
    if vt_ref:
        vt_ref[0][0] = acc[:, n:].T.astype(_BF16)


def _proj_rope(x, mods, w, cos, sin, n_rope, n_t, n_lat_tiles, n_q=0, q_scale=1.0):
    b, p, d = x.shape
    n_all = w.shape[1]
    n = n_all - n_t
    tm = ROW_TILE
    x_spec, mod_spec = _row_specs(b, p, d, n_lat_tiles, tm)
    out_specs = [pl.BlockSpec((1, tm, n), lambda i, j: (i, j, 0))]
    out_shape = [jax.ShapeDtypeStruct((b, p, n), _BF16)]
    if n_t:
        out_specs.append(pl.BlockSpec((1, n_t, tm), lambda i, j: (i, 0, j)))
        out_shape.append(jax.ShapeDtypeStruct((b, n_t, p), _BF16))
    return pl.pallas_call(
        functools.partial(_proj_rope_kernel, n_rope=n_rope, n_q=n_q, q_scale=q_scale),
        grid=(b, p // tm),
        in_specs=[x_spec, mod_spec,
                  pl.BlockSpec((d, n_all), lambda i, j: (0, 0)),
                  pl.BlockSpec((tm, LANES), lambda i, j: (j, 0)),
                  pl.BlockSpec((tm, LANES), lambda i, j: (j, 0))],
        out_specs=out_specs,
        out_shape=out_shape,
        compiler_params=_cparams("parallel", "parallel"),
        name="proj_rope",
    )(x, mods, w, cos, sin)


def _rms(x, g):
    return x * lax.rsqrt(jnp.mean(x * x, axis=-1, keepdims=True) + RMS_EPS) * g


def _proj_mla_kernel(x_ref, mod_ref, win_ref, qn_ref, kvn_ref, wuq_ref, wuk_ref, wv_ref,
                     cos_ref, sin_ref, q_ref, k_ref, vt_ref, *, scale):
    h = _modulated(x_ref, mod_ref, 0).astype(_BF16)
    c = jnp.dot(h, win_ref[...], preferred_element_type=_F32)
    cq = _rms(c[:, :B_Q_RANK], qn_ref[...]).astype(_BF16)
    ckv = _rms(c[:, B_Q_RANK:B_Q_RANK + B_KV_RANK], kvn_ref[...]).astype(_BF16)
    cos, sin = cos_ref[...], sin_ref[...]
    k_rope = _rope(c[:, B_Q_RANK + B_KV_RANK:], cos, sin, 8)
    q = jnp.dot(cq, wuq_ref[...], preferred_element_type=_F32)
    k_nope = jnp.dot(ckv, wuk_ref[...], preferred_element_type=_F32)
    for hd in range(B_HEADS):
        sl = slice(hd * LANES, (hd + 1) * LANES)
        q_ref[0, :, sl] = (_rope(q[:, sl], cos, sin, 8) * scale).astype(_BF16)
        k_ref[0, :, sl] = (k_nope[:, sl] + k_rope).astype(_BF16)
    vt_ref[0] = jnp.dot(ckv, wv_ref[...], preferred_element_type=_F32).T.astype(_BF16)


def _proj_mla(x, mods, w_in, q_norm, kv_norm, w_uq, w_uk, w_v, cos, sin, n_lat_tiles):
    b, p, d = x.shape
    tm = ROW_TILE
    x_spec, mod_spec = _row_specs(b, p, d, n_lat_tiles, tm)
    full = lambda a: pl.BlockSpec(a.shape, lambda i, j: (0,) * a.ndim)
    nq, nv = w_uq.shape[1], w_v.shape[1]
    tile = lambda n: pl.BlockSpec((1, tm, n), lambda i, j: (i, j, 0))
    return pl.pallas_call(
        functools.partial(_proj_mla_kernel, scale=(B_NOPE + B_ROPE) ** -0.5 * LOG2E),
        grid=(b, p // tm),
        in_specs=[x_spec, mod_spec, full(w_in), full(q_norm), full(kv_norm), full(w_uq), full(w_uk),
                  full(w_v),
                  pl.BlockSpec((tm, LANES), lambda i, j: (j, 0)),
                  pl.BlockSpec((tm, LANES), lambda i, j: (j, 0))],
        out_specs=[tile(nq), tile(nq), pl.BlockSpec((1, nv, tm), lambda i, j: (i, 0, j))],
        out_shape=[jax.ShapeDtypeStruct((b, p, nq), _BF16), jax.ShapeDtypeStruct((b, p, nq), _BF16),
                   jax.ShapeDtypeStruct((b, nv, p), _BF16)],
        compiler_params=_cparams("parallel", "parallel"),
        name="proj_mla",
    )(x, mods, w_in, q_norm, kv_norm, w_uq, w_uk, w_v, cos, sin)


def _attn_win_kernel(sink_ref, q_ref, k0_ref, k1_ref, k2_ref, kc_ref, vt0_ref, vt1_ref, vt2_ref, vtc_ref,
                     o_ref, *, n_lat_tiles):
    j = pl.program_id(1)
    w = WIN_BLOCK
    n_ctx = kc_ref.shape[1]
    n_keys = 3 * w + n_ctx
    nq = A_GROUP * w
    row = lax.broadcasted_iota(jnp.int32, (n_keys, nq), 0)
    q_pos = j * w + lax.broadcasted_iota(jnp.int32, (n_keys, nq), 1) % w
    k_pos = (j - 1) * w + row
    n_latent_keys = jnp.where(j < n_lat_tiles, n_lat_tiles * w, 0)
    valid = (row >= 3 * w) | ((jnp.abs(q_pos - k_pos) <= A_WINDOW) & (k_pos >= 0) & (k_pos < n_latent_keys))
    low = _lane_iota((w, LANES)) < A_HEAD_DIM

    def scores(kvh):
        cs = slice(kvh * LANES, (kvh + 1) * LANES)
        kk = jnp.concatenate([k0_ref[0, :, cs], k1_ref[0, :, cs], k2_ref[0, :, cs], kc_ref[0, :, cs]], axis=0)
        qa = q_ref[0, :, 2 * kvh * LANES:(2 * kvh + 1) * LANES]
        qb = q_ref[0, :, (2 * kvh + 1) * LANES:(2 * kvh + 2) * LANES]
        zero = jnp.zeros_like(qa)
        q4 = jnp.concatenate([jnp.where(low, qa, zero), jnp.where(low, zero, qa),
                              jnp.where(low, qb, zero), jnp.where(low, zero, qb)], axis=0)
        return lax.dot_general(kk, q4, _NT, preferred_element_type=_F32)

    outs = []
    s_next = scores(0)
    for kvh in range(A_KV_HEADS):
        s = s_next
        if kvh + 1 < A_KV_HEADS:
            s_next = scores(kvh + 1)
        rs = slice(kvh * A_HEAD_DIM, (kvh + 1) * A_HEAD_DIM)
        vt = jnp.concatenate([vt0_ref[0, rs, :], vt1_ref[0, rs, :], vt2_ref[0, rs, :], vtc_ref[0, rs, :]], axis=1)
        sink = jnp.concatenate([jnp.full((1, w), sink_ref[kvh * A_GROUP + g] * LOG2E, _F32)
                                for g in range(A_GROUP)], axis=1)
        s = jnp.where(valid, s, -jnp.inf)
        m = jnp.maximum(jnp.max(s, axis=0, keepdims=True), sink)
        e = jnp.exp2(s - m)
        denom = jnp.sum(e, axis=0, keepdims=True) + jnp.exp2(sink - m)
        o = jnp.dot(vt, e.astype(_BF16), preferred_element_type=_F32) / denom
        outs += [o[:, g * w:(g + 1) * w] for g in range(A_GROUP)]
    o_ref[0] = jnp.concatenate(outs, axis=0).T.astype(_BF16)


def _attn_win(qk, vt, sink, s_len, l_ctx):
    b, p, _ = qk.shape
    w = WIN_BLOCK
    n_lat = s_len // w
    nq = A_HEADS * A_HEAD_DIM
    kw = A_KV_HEADS * LANES
    nv = A_KV_HEADS * A_HEAD_DIM
    ctx_blk = s_len // l_ctx

    def nb(off):
        return lambda i, j: (i, jnp.clip(j + off, 0, n_lat - 1), nq // kw)

    def nbt(off):
        return lambda i, j: (i, 0, jnp.clip(j + off, 0, n_lat - 1))

    return pl.pallas_call(
        functools.partial(_attn_win_kernel, n_lat_tiles=n_lat),
        grid=(b, p // w),
        in_specs=[pl.BlockSpec(memory_space=pltpu.SMEM),
                  pl.BlockSpec((1, w, nq), lambda i, j: (i, j, 0)),
                  pl.BlockSpec((1, w, kw), nb(-1)), pl.BlockSpec((1, w, kw), nb(0)),
                  pl.BlockSpec((1, w, kw), nb(1)),
                  pl.BlockSpec((1, l_ctx, kw), lambda i, j: (i, ctx_blk, nq // kw)),
                  pl.BlockSpec((1, nv, w), nbt(-1)), pl.BlockSpec((1, nv, w), nbt(0)),
                  pl.BlockSpec((1, nv, w), nbt(1)),
                  pl.BlockSpec((1, nv, l_ctx), lambda i, j: (i, 0, ctx_blk))],
        out_specs=pl.BlockSpec((1, w, nq), lambda i, j: (i, j, 0)),
        out_shape=jax.ShapeDtypeStruct((b, p, nq), _BF16),
        compiler_params=_cparams("parallel", "parallel"),
        name="attn_win",
    )(sink, qk, qk, qk, qk, qk, vt, vt, vt, vt)


def _key_chunks(s_len, l_ctx, latent_query):
    chunks = [(lo, KEY_CHUNK) for lo in range(0, s_len, KEY_CHUNK)] if latent_query else []
    return chunks + [(s_len, l_ctx)]


def _flash_keys_major(q, k_chunk, vt_chunk, chunks):
    def scores(c):
        return lax.dot_general(k_chunk(*chunks[c]), q, _NT, preferred_element_type=_F32)

    m = acc = l = None
    pending = None
    s_next = scores(0)

    def flush(acc):
        (lo, n), e, corr = pending
        pv = jnp.dot(vt_chunk(lo, n), e, preferred_element_type=_F32)
        return pv if acc is None else acc * corr + pv

    for c in range(len(chunks)):
        s = s_next
        if c + 1 < len(chunks):
            s_next = scores(c + 1)
        if pending is not None:
            acc = flush(acc)
        m_chunk = jnp.max(s, axis=0, keepdims=True)
        m_new = m_chunk if m is None else jnp.maximum(m, m_chunk)
        e = jnp.exp2(s - m_new)
        l_chunk = jnp.sum(e, axis=0, keepdims=True)
        corr = None if m is None else jnp.exp2(m - m_new)
        l = l_chunk if m is None else l * corr + l_chunk
        pending = (chunks[c], e.astype(_BF16), corr)
        m = m_new
    return flush(acc), l


def _attn_mla_kernel(q_ref, k_ref, vt_ref, o_ref, *, n_lat_tiles, s_len, l_ctx):
    j = pl.program_id(2)

    tq = q_ref.shape[1]
    q = q_ref[0]
    first = _lane_iota(q.shape) < LANES
    zero = jnp.zeros_like(q)
    q2 = jnp.concatenate([jnp.where(first, q, zero), jnp.where(first, zero, q)], axis=0)

    def run(latent_query):
        acc, l = _flash_keys_major(q2, lambda lo, n: k_ref[0, lo:lo + n, :],
                                   lambda lo, n: vt_ref[0, :, lo:lo + n],
                                   _key_chunks(s_len, l_ctx, latent_query))
        o2 = acc / l
        o = jnp.concatenate([o2[:B_V, :tq], o2[B_V:, tq:]], axis=0)
        o_ref[0] = o.T.astype(_BF16)

    @pl.when(j < n_lat_tiles)
    def _():
        run(True)

    @pl.when(j >= n_lat_tiles)
    def _():
        run(False)


def _attn_mla(q, k, vt, s_len):
    b, p, _ = q.shape
    tq = ATTN_Q_TILE
    pairs = B_HEADS // 2
    return pl.pallas_call(
        functools.partial(_attn_mla_kernel, n_lat_tiles=s_len // tq, s_len=s_len, l_ctx=p - s_len),
        grid=(b, pairs, p // tq),
        in_specs=[pl.BlockSpec((1, tq, 2 * LANES), lambda i, c, j: (i, j, c)),
                  pl.BlockSpec((1, p, 2 * LANES), lambda i, c, j: (i, 0, c)),
                  pl.BlockSpec((1, 2 * B_V, p), lambda i, c, j: (i, c, 0))],
        out_specs=pl.BlockSpec((1, tq, 2 * B_V), lambda i, c, j: (i, j, c)),
        out_shape=jax.ShapeDtypeStruct((b, p, B_HEADS * B_V), _BF16),
        compiler_params=_cparams("parallel", "parallel", "parallel"),
        name="attn_mla",
    )(q, k, vt)


def _attn_diff_kernel(lam_ref, subln_ref, q_ref, k_ref, vt_ref, o_ref, *, n_lat_tiles, s_len, l_ctx,
                      lambda_init):
    j = pl.program_id(2)
    tq = q_ref.shape[1]
    lp = lam_ref[...]
    lam = (jnp.exp(jnp.sum(lp[0:1] * lp[1:2], axis=-1, keepdims=True))
           - jnp.exp(jnp.sum(lp[2:3] * lp[3:4], axis=-1, keepdims=True)) + lambda_init)
    q = q_ref[0]
    low = _lane_iota(q.shape) < C_HEAD_DIM
    zero = jnp.zeros_like(q)
    q2 = jnp.concatenate([jnp.where(low, q, zero), jnp.where(low, zero, q)], axis=0)

    def run(latent_query):
        acc, l = _flash_keys_major(q2, lambda lo, n: k_ref[0, lo:lo + n, :],
                                   lambda lo, n: vt_ref[0, :, lo:lo + n],
                                   _key_chunks(s_len, l_ctx, latent_query))
        o2 = acc / l
        o = o2[:, :tq] - lam * o2[:, tq:]
        o = o * lax.rsqrt(jnp.mean(o * o, axis=0, keepdims=True) + RMS_EPS) * subln_ref[...]
        o_ref[0] = (o * (1.0 - lambda_init)).T.astype(_BF16)

    @pl.when(j < n_lat_tiles)
    def _():
        run(True)

    @pl.when(j >= n_lat_tiles)
    def _():
        run(False)


def _attn_diff(qk, vt, lam_params, subln, s_len, lambda_init):
    b, p, _ = qk.shape
    tq = ATTN_Q_TILE
    return pl.pallas_call(
        functools.partial(_attn_diff_kernel, n_lat_tiles=s_len // tq, s_len=s_len, l_ctx=p - s_len,
                          lambda_init=lambda_init),
        grid=(b, C_HEADS, p // tq),
        in_specs=[pl.BlockSpec(lam_params.shape, lambda i, h, j: (0, 0)),
                  pl.BlockSpec((C_V_DIM, 1), lambda i, h, j: (0, 0)),
                  pl.BlockSpec((1, tq, LANES), lambda i, h, j: (i, j, h)),
                  pl.BlockSpec((1, p, LANES), lambda i, h, j: (i, 0, C_HEADS + h)),
                  pl.BlockSpec((1, C_V_DIM, p), lambda i, h, j: (i, h, 0))],
        out_specs=pl.BlockSpec((1, tq, LANES), lambda i, h, j: (i, j, h)),
        out_shape=jax.ShapeDtypeStruct((b, p, C_HEADS * C_V_DIM), _BF16),
        compiler_params=_cparams("parallel", "parallel", "parallel"),
        name="attn_diff",
    )(lam_params, subln.reshape(C_V_DIM, 1), qk, qk, vt)


def _layer_norm(z, g, b):
    mu = jnp.mean(z, axis=-1, keepdims=True)
    zc = z - mu
    var = jnp.mean(zc * zc, axis=-1, keepdims=True)
    return zc * lax.rsqrt(var + LN_EPS) * g + b


def _out_proj_kernel(o_ref, w_ref, x_ref, mod_ref, g_ref, b_ref, wr_ref, br_ref,
                     xo_ref, h_ref, lg_ref, *, alpha):
    y = jnp.dot(o_ref[0], w_ref[...], preferred_element_type=_F32)
    m = mod_ref[0]
    x = _layer_norm(alpha * x_ref[0] + m[2:3, :] * y, g_ref[...], b_ref[...])
    xo_ref[0] = x
    h = x * (1.0 + m[4:5, :]) + m[3:4, :]
    h_ref[0] = h
    tm = h.shape[0]
    h_hi = h.astype(_BF16)
    h_lo = (h - h_hi.astype(_F32)).astype(_BF16)
    t = jnp.dot(jnp.concatenate([h_hi, h_lo], axis=0), wr_ref[...], preferred_element_type=_F32)
    lg_ref[0] = t[:tm, :LANES] + t[tm:, :LANES] + t[:tm, LANES:] + br_ref[...]


def _out_proj(o, w_out, x, mods, ln_g, ln_b, w_router, b_router, alpha, n_lat_tiles):
    b, p, d = x.shape
    tm = ROW_TILE
    x_spec, mod_spec = _row_specs(b, p, d, n_lat_tiles, tm)
    full = lambda a: pl.BlockSpec(a.shape, lambda i, j: (0,) * a.ndim)
    n_o = o.shape[2]
    return pl.pallas_call(
        functools.partial(_out_proj_kernel, alpha=alpha),
        grid=(b, p // tm),
        in_specs=[pl.BlockSpec((1, tm, n_o), lambda i, j: (i, j, 0)), full(w_out), x_spec, mod_spec,
                  full(ln_g), full(ln_b), full(w_router), full(b_router)],
        out_specs=[x_spec, x_spec, pl.BlockSpec((1, tm, LANES), lambda i, j: (i, j, 0))],
        out_shape=[jax.ShapeDtypeStruct((b, p, d), _F32), jax.ShapeDtypeStruct((b, p, d), _F32),
                   jax.ShapeDtypeStruct((b, p, LANES), _F32)],
        compiler_params=_cparams("parallel", "parallel"),
        name="out_proj_ln",
    )(o, w_out, x, mods, ln_g, ln_b, w_router, b_router)


def _route_kernel(lg_ref, info_ref, btab_ref, cnt_ref, start_ref, run_ref, *, block_rows):
    sweep, i = pl.program_id(0), pl.program_id(1)
    x = lg_ref[...]
    tm = x.shape[0]
    lane = _lane_iota(x.shape)
    xg = jnp.where(lane < N_GROUPS, x, -jnp.inf)
    g_max = jnp.max(xg, axis=-1, keepdims=True)
    g_sel = jnp.min(jnp.where(xg == g_max, lane, LANES), axis=-1, keepdims=True)
    p_g = 1.0 / jnp.sum(jnp.exp(xg - g_max), axis=-1, keepdims=True)
    lo = N_GROUPS + g_sel * EXPERTS_PER_GROUP
    xe = jnp.where((lane >= lo) & (lane < lo + EXPERTS_PER_GROUP), x, -jnp.inf)
    v1 = jnp.max(xe, axis=-1, keepdims=True)
    i1 = jnp.min(jnp.where(xe == v1, lane, LANES), axis=-1, keepdims=True)
    xe = jnp.where(lane == i1, -jnp.inf, xe)
    v2 = jnp.max(xe, axis=-1, keepdims=True)
    i2 = jnp.min(jnp.where(xe == v2, lane, LANES), axis=-1, keepdims=True)
    hit1, hit2 = lane == i1, lane == i2
    onehot = (hit1 | hit2).astype(_F32)

    @pl.when((sweep == 0) & (i == 0))
    def _():
        cnt_ref[...] = jnp.zeros_like(cnt_ref)

    @pl.when(sweep == 0)
    def _():
        cnt_ref[...] += jnp.sum(onehot, axis=0, keepdims=True)

    @pl.when((sweep == 1) & (i == 0))
    def _():
        cnt = cnt_ref[...]
        padded = jnp.floor((cnt + (block_rows - 1)) * (1.0 / block_rows)) * block_rows
        r = lax.broadcasted_iota(jnp.int32, (LANES, LANES), 0)
        c = lax.broadcasted_iota(jnp.int32, (LANES, LANES), 1)
        before = (r < c).astype(_F32)
        start = jnp.dot(jnp.broadcast_to(padded, (8, LANES)), before, preferred_element_type=_F32,
                        precision=lax.Precision.HIGHEST)[0:1]
        start_ref[...] = start
        run_ref[...] = jnp.zeros_like(run_ref)
        nb = btab_ref.shape[0]
        row0 = (lax.broadcasted_iota(jnp.int32, (nb, LANES), 0) * block_rows).astype(_F32)
        is_e = (_lane_iota((nb, LANES)) >= N_GROUPS) & (_lane_iota((nb, LANES)) < N_GROUPS + N_EXPERTS)
        end = start + padded
        owner = jnp.sum(jnp.where(is_e & (end <= row0), 1.0, 0.0), axis=-1, keepdims=True)
        inside = is_e & (start <= row0) & (row0 < end)
        real = jnp.sum(jnp.where(inside, jnp.clip(cnt - (row0 - start), 0.0, block_rows), 0.0),
                       axis=-1, keepdims=True)
        bl = _lane_iota((nb, LANES))
        btab_ref[...] = jnp.where(bl == 0, jnp.minimum(owner, N_EXPERTS - 1.0),
                                  jnp.where(bl == 1, real, 0.0)).astype(jnp.int32)

    @pl.when(sweep == 1)
    def _():
        row = lax.broadcasted_iota(jnp.int32, (tm, tm), 0)
        col = lax.broadcasted_iota(jnp.int32, (tm, tm), 1)
        earlier = (col < row).astype(_BF16)
        rank = jnp.dot(earlier, onehot.astype(_BF16), preferred_element_type=_F32) + run_ref[...]
        pos = rank + start_ref[...]
        d1 = jnp.sum(jnp.where(hit1, pos, 0.0), axis=-1, keepdims=True)
        d2 = jnp.sum(jnp.where(hit2, pos, 0.0), axis=-1, keepdims=True)
        run_ref[...] += jnp.sum(onehot, axis=0, keepdims=True)
        t = jnp.exp(v2 - v1)
        w1 = p_g / (1.0 + t)
        info_ref[...] = jnp.where(lane == 0, d1, jnp.where(lane == 1, d2, jnp.where(lane == 2, w1,
                                  jnp.where(lane == 3, w1 * t, 0.0))))


def _route(logits, n_blocks):
    t = logits.shape[0]
    tm = ROUTE_TILE
    nb_pad = -(-n_blocks // 8) * 8
    return pl.pallas_call(
        functools.partial(_route_kernel, block_rows=MOE_ROWS),
        grid=(2, t // tm),
        in_specs=[pl.BlockSpec((tm, LANES), lambda s, i: (i, 0))],
        out_specs=[pl.BlockSpec((tm, LANES), lambda s, i: (i * s, 0)),
                   pl.BlockSpec((nb_pad, LANES), lambda s, i: (0, 0))],
        out_shape=[jax.ShapeDtypeStruct((t, LANES), _F32), jax.ShapeDtypeStruct((nb_pad, LANES), jnp.int32)],
        scratch_shapes=[pltpu.VMEM((1, LANES), _F32)] * 3,
        compiler_params=_cparams("arbitrary", "arbitrary"),
        name="moe_route",
    )(logits)


def _row_copy_wait(src_rows, dst_rows, sem):
    pltpu.make_async_copy(src_rows, dst_rows, sem).wait()


def _dispatch_kernel(dest_ref, h_ref, xb_in_ref, xb_ref, sem):
    del xb_in_ref
    tm = h_ref.shape[0]
    base = pl.program_id(0) * tm

    for r in range(tm):
        for k in range(TOP_K):
            d = dest_ref[(base + r) * TOP_K + k]
            pltpu.make_async_copy(h_ref.at[pl.ds(r, 1)], xb_ref.at[pl.ds(d, 1)], sem).start()
    for k in range(TOP_K):
        _row_copy_wait(h_ref, xb_ref.at[pl.ds(0, tm)], sem)


def _dispatch(dest, h, rows):
    t, d = h.shape
    tm = ROW_TILE
    grid_spec = pltpu.PrefetchScalarGridSpec(
        num_scalar_prefetch=1,
        grid=(t // tm,),
        in_specs=[pl.BlockSpec((tm, d), lambda i, dest: (i, 0)),
                  pl.BlockSpec(memory_space=pl.ANY)],
        out_specs=pl.BlockSpec(memory_space=pl.ANY),
        scratch_shapes=[pltpu.SemaphoreType.DMA(())],
    )
    return pl.pallas_call(
        _dispatch_kernel,
        grid_spec=grid_spec,
        out_shape=jax.ShapeDtypeStruct((rows, d), h.dtype),
        input_output_aliases={2: 0},
        compiler_params=_cparams("arbitrary", disable_bounds_checks=True),
        name="moe_dispatch",
    )(dest, h, jnp.zeros((rows, d), h.dtype))


def _ffn_kernel(btab_ref, x_ref, wg_ref, wu_ref, wd_ref, y_ref, wg_bf, wu_bf, wd_bf):
    i = pl.program_id(0)
    expert, real = btab_ref[i, 0], btab_ref[i, 1]
    prev = btab_ref[jnp.maximum(i - 1, 0), 0]

    @pl.when((i == 0) | (expert != prev))
    def _():
        wg_bf[...] = wg_ref[0].astype(_BF16)
        wu_bf[...] = wu_ref[0].astype(_BF16)
        wd_bf[...] = wd_ref[0].astype(_BF16)

    @pl.when(real > 0)
    def _():
        x = x_ref[...].astype(_BF16)
        g = jnp.dot(x, wg_bf[...], preferred_element_type=_F32)
        u = jnp.dot(x, wu_bf[...], preferred_element_type=_F32)
        a = (g * jax.nn.sigmoid(g) * u).astype(_BF16)
        y_ref[...] = jnp.dot(a, wd_bf[...], preferred_element_type=_F32)

    @pl.when(real <= 0)
    def _():
        y_ref[...] = jnp.zeros_like(y_ref)


def _expert_ffn(btab, xb, w_gate, w_up, w_down):
    rows, d = xb.shape
    de = w_gate.shape[2]
    bm = MOE_ROWS
    grid_spec = pltpu.PrefetchScalarGridSpec(
        num_scalar_prefetch=1,
        grid=(rows // bm,),
        in_specs=[pl.BlockSpec((bm, d), lambda i, bt: (i, 0)),
                  pl.BlockSpec((1, d, de), lambda i, bt: (bt[i, 0], 0, 0)),
                  pl.BlockSpec((1, d, de), lambda i, bt: (bt[i, 0], 0, 0)),
                  pl.BlockSpec((1, de, d), lambda i, bt: (bt[i, 0], 0, 0))],
        out_specs=pl.BlockSpec((bm, d), lambda i, bt: (i, 0)),
        scratch_shapes=[pltpu.VMEM((d, de), _BF16), pltpu.VMEM((d, de), _BF16), pltpu.VMEM((de, d), _BF16)],
    )
    return pl.pallas_call(
        _ffn_kernel,
        grid_spec=grid_spec,
        out_shape=jax.ShapeDtypeStruct((rows, d), _F32),
        compiler_params=_cparams("arbitrary"),
        name="expert_ffn",
    )(btab, xb, w_gate, w_up, w_down)


def _combine_kernel(dest_ref, info_ref, x_ref, mod_ref, g_ref, b_ref, yb_ref, xo_ref, buf, sem, *, alpha):
    tm = x_ref.shape[1]
    nj = pl.num_programs(1)
    step = pl.program_id(0) * nj + pl.program_id(1)
    n_steps = pl.num_programs(0) * nj

    def issue(at_step, slot):
        base = at_step * tm

        for r in range(tm):
            for k in range(TOP_K):
                d = dest_ref[(base + r) * TOP_K + k]
                pltpu.make_async_copy(yb_ref.at[pl.ds(d, 1)], buf.at[slot, k, pl.ds(r, 1)], sem.at[slot]).start()

    @pl.when(step == 0)
    def _():
        issue(0, 0)

    @pl.when(step + 1 < n_steps)
    def _():
        issue(step + 1, (step + 1) % 2)

    slot = step % 2
    for k in range(TOP_K):
        _row_copy_wait(yb_ref.at[pl.ds(0, tm)], buf.at[slot, k], sem.at[slot])
    info = info_ref[0]
    y = info[:, 2:3] * buf[slot, 0] + info[:, 3:4] * buf[slot, 1]
    m = mod_ref[0]
    xo_ref[0] = _layer_norm(alpha * x_ref[0] + m[5:6, :] * y, g_ref[...], b_ref[...])


def _combine(dest, info, yb, x, mods, ln_g, ln_b, alpha, n_lat_tiles):
    b, p, d = x.shape
    tm = ROW_TILE
    grid_spec = pltpu.PrefetchScalarGridSpec(
        num_scalar_prefetch=1,
        grid=(b, p // tm),
        in_specs=[pl.BlockSpec((1, tm, LANES), lambda i, j, dest: (i, j, 0)),
                  pl.BlockSpec((1, tm, d), lambda i, j, dest: (i, j, 0)),
                  pl.BlockSpec((1, 6, d), lambda i, j, dest: (jnp.where(j < n_lat_tiles, i, b), 0, 0)),
                  pl.BlockSpec(ln_g.shape, lambda i, j, dest: (0, 0)),
                  pl.BlockSpec(ln_b.shape, lambda i, j, dest: (0, 0)),
                  pl.BlockSpec(memory_space=pl.ANY)],
        out_specs=pl.BlockSpec((1, tm, d), lambda i, j, dest: (i, j, 0)),
        scratch_shapes=[pltpu.VMEM((2, TOP_K, tm, d), _F32), pltpu.SemaphoreType.DMA((2,))],
    )
    return pl.pallas_call(
        functools.partial(_combine_kernel, alpha=alpha),
        grid_spec=grid_spec,
        out_shape=jax.ShapeDtypeStruct((b, p, d), _F32),
        compiler_params=_cparams("arbitrary", "arbitrary", disable_bounds_checks=True),
        name="moe_combine_ln",
    )(dest, info.reshape(b, p, LANES), x, mods, ln_g, ln_b, yb)


def _moe_layer(h, logits, x, mods, ln_g, ln_b, w_gate, w_up, w_down, alpha, n_lat_tiles):
    t, d = h.shape
    n_blocks = (t * TOP_K) // MOE_ROWS + N_EXPERTS
    info, btab = _route(logits, n_blocks)
    dest = info[:, :TOP_K].astype(jnp.int32).reshape(t * TOP_K)
    xb = _dispatch(dest, h, n_blocks * MOE_ROWS)
    yb = _expert_ffn(btab[:n_blocks], xb, w_gate, w_up, w_down)
    return _combine(dest, info, yb, x, mods, ln_g, ln_b, alpha, n_lat_tiles)


def _rope_tables(s_len, l_ctx, rot_dim, lane_lo, period):
    rows = s_len // GRID_W
    row = jnp.repeat(jnp.arange(rows, dtype=_F32), GRID_W)
    col = jnp.tile(jnp.arange(GRID_W, dtype=_F32), rows)
    axis_dim = rot_dim // 2
    half = axis_dim // 2
    inv_freq = ROPE_THETA ** (-jnp.arange(0, axis_dim, 2, dtype=_F32) / axis_dim)
    lane = jnp.arange(LANES)
    rel = (lane % period) - lane_lo
    active = (rel >= 0) & (rel < rot_dim)
    rel = jnp.clip(rel, 0, rot_dim - 1)
    use_col = rel >= axis_dim
    f = inv_freq[(rel % axis_dim) % half]
    ang = jnp.where(use_col[None, :], col[:, None], row[:, None]) * f[None, :]
    sign = jnp.where((rel % axis_dim) < half, -1.0, 1.0)
    cos = jnp.where(active[None, :], jnp.cos(ang), 1.0)
    sin = jnp.where(active[None, :], jnp.sin(ang) * sign[None, :], 0.0)
    ident = jnp.ones((l_ctx, LANES), _F32)
    return (jnp.concatenate([cos, ident], axis=0), jnp.concatenate([sin, 0.0 * ident], axis=0))


def _win_weights(w_in):
    d = w_in.shape[0]
    nq, nkv = A_HEADS * A_HEAD_DIM, A_KV_HEADS * A_HEAD_DIM
    k = w_in[:, nq:nq + nkv].reshape(d, A_KV_HEADS, 1, A_HEAD_DIM)
    k_dup = jnp.concatenate([k, k], axis=2).reshape(d, 2 * nkv)
    return jnp.concatenate([w_in[:, :nq], k_dup, w_in[:, nq + nkv:]], axis=1).astype(_BF16)


def _mla_weights(w_in, w_uq, w_ukv):
    d = w_in.shape[0]
    pad = LANES - B_NOPE - B_ROPE
    kr = jnp.concatenate([jnp.zeros((d, B_NOPE), _F32), w_in[:, B_Q_RANK + B_KV_RANK:],
                          jnp.zeros((d, pad), _F32)], axis=1)
    w_in_p = jnp.concatenate([w_in[:, :B_Q_RANK + B_KV_RANK], kr], axis=1).astype(_BF16)
    uq = w_uq.reshape(B_Q_RANK, B_HEADS, B_NOPE + B_ROPE)
    uq = jnp.pad(uq, ((0, 0), (0, 0), (0, pad))).reshape(B_Q_RANK, B_HEADS * LANES).astype(_BF16)
    ukv = w_ukv.reshape(B_KV_RANK, B_HEADS, B_NOPE + B_V)
    uk = jnp.pad(ukv[:, :, :B_NOPE], ((0, 0), (0, 0), (0, LANES - B_NOPE)))
    uk = uk.reshape(B_KV_RANK, B_HEADS * LANES).astype(_BF16)
    uv = ukv[:, :, B_NOPE:].reshape(B_KV_RANK, B_HEADS * B_V).astype(_BF16)
    return w_in_p, uq, uk, uv


def kernel(x, c, ctx, c_ctx, ada_w, ada_b, ln_mix_g, ln_mix_b, ln_ffn_g, ln_ffn_b, win_w_in, win_w_out, win_sink, mla_w_in, mla_q_norm, mla_kv_norm, mla_w_uq, mla_w_ukv, mla_w_out, dif_w_in, dif_lambda, dif_subln, dif_w_out, moe_w_grp, moe_b_grp, moe_w_rt, moe_b_rt, moe_w_gate, moe_w_up, moe_w_down):
    b, s_len, d = x.shape
    l_ctx = ctx.shape[1]
    depth = ada_w.shape[0]
    p = s_len + l_ctx
    assert s_len % ROW_TILE == 0 and l_ctx % ROW_TILE == 0 and s_len % GRID_W == 0
    assert s_len % ATTN_Q_TILE == 0 and l_ctx == ATTN_Q_TILE and s_len % l_ctx == 0
    alpha = (2.0 * depth) ** 0.25
    n_lat_tiles = s_len // ROW_TILE

    xs = jnp.concatenate([x, ctx], axis=1)
    cs = jnp.concatenate([c, c_ctx[None, :]], axis=0)
    mods = _ada_mods(cs, ada_w, ada_b).reshape(depth, b + 1, 6, d)
    cos64, sin64 = _rope_tables(s_len, l_ctx, A_HEAD_DIM, 0, A_HEAD_DIM)
    cos32, sin32 = _rope_tables(s_len, l_ctx, B_ROPE, B_NOPE, LANES)
    row = lambda v: v.reshape(1, -1)

    for i in range(depth):
        kind, slot = i % N_MIXERS, i // N_MIXERS
        if kind == 0:
            qk, vt = _proj_rope(xs, mods[i], _win_weights(win_w_in[slot]), cos64, sin64,
                                (A_HEADS + 2 * A_KV_HEADS) * A_HEAD_DIM, A_KV_HEADS * A_HEAD_DIM, n_lat_tiles,
                                n_q=A_HEADS * A_HEAD_DIM, q_scale=A_HEAD_DIM ** -0.5 * LOG2E)
            o = _attn_win(qk, vt, win_sink[slot], s_len, l_ctx)
            w_out = win_w_out[slot]
        elif kind == 1:
            w_in_p, uq, uk, uv = _mla_weights(mla_w_in[slot], mla_w_uq[slot], mla_w_ukv[slot])
            q, k, vt = _proj_mla(xs, mods[i], w_in_p, row(mla_q_norm[slot]), row(mla_kv_norm[slot]),
                                uq, uk, uv, cos32, sin32, n_lat_tiles)
            o = _attn_mla(q, k, vt, s_len)
            w_out = mla_w_out[slot]
        else:
            lambda_init = 0.8 - 0.6 * math.exp(-0.3 * i)
            qk, vt = _proj_rope(xs, mods[i], dif_w_in[slot].astype(_BF16), cos64, sin64,
                                2 * C_HEADS * 2 * C_HEAD_DIM, C_HEADS * C_V_DIM, n_lat_tiles,
                                n_q=C_HEADS * 2 * C_HEAD_DIM, q_scale=C_HEAD_DIM ** -0.5 * LOG2E)
            o = _attn_diff(qk, vt, dif_lambda[slot], dif_subln[slot], s_len, lambda_init)
            w_out = dif_w_out[slot]

        w_router = jnp.concatenate([moe_w_grp[i], moe_w_rt[i],
                                    jnp.zeros((d, LANES - N_GROUPS - N_EXPERTS), _F32)], axis=1)
        b_router = jnp.concatenate([moe_b_grp[i], moe_b_rt[i],
                                    jnp.zeros((LANES - N_GROUPS - N_EXPERTS,), _F32)])[None, :]
        w_hi = w_router.astype(_BF16)
        w_lo = (w_router - w_hi.astype(_F32)).astype(_BF16)
        xs, h, logits = _out_proj(o, w_out.astype(_BF16), xs, mods[i], row(ln_mix_g[i]), row(ln_mix_b[i]),
                                  jnp.concatenate([w_hi, w_lo], axis=1), b_router, alpha, n_lat_tiles)
        xs = _moe_layer(h.reshape(b * p, d), logits.reshape(b * p, LANES), xs, mods[i], row(ln_ffn_g[i]),
                        row(ln_ffn_b[i]), moe_w_gate[i], moe_w_up[i], moe_w_down[i], alpha, n_lat_tiles)
    return xs[:, :s_len]
```

```python
import functools
import math

import jax
import jax.numpy as jnp
from jax import lax
from jax.experimental import pallas as pl
from jax.experimental.pallas import tpu as pltpu

LANES = 128
GRID_W = 64
ROPE_THETA = 10000.0
LN_EPS = 1e-5
RMS_EPS = 1e-6
LOG2E = math.log2(math.e)

A_HEADS, A_KV_HEADS, A_HEAD_DIM, A_WINDOW = 16, 4, 64, 128
A_GROUP = A_HEADS // A_KV_HEADS
B_HEADS, B_NOPE, B_ROPE, B_V, B_Q_RANK, B_KV_RANK = 16, 64, 32, 64, 512, 256
C_HEADS, C_HEAD_DIM = 8, 64
C_V_DIM = 2 * C_HEAD_DIM
N_GROUPS, EXPERTS_PER_GROUP, TOP_K = 4, 8, 2
N_EXPERTS = N_GROUPS * EXPERTS_PER_GROUP
N_MIXERS = 3

ROW_TILE = 256
ATTN_Q_TILE = 256
Q_TILES_PER_STEP = 2
WIN_BLOCK = 128
KEY_CHUNK = 1024
MOE_ROWS = 256
ROUTE_TILE = 512
VMEM_LIMIT = 56 * 1024 * 1024

_F32 = jnp.float32
_BF16 = jnp.bfloat16
_NT = (((1,), (1,)), ((), ()))


def _cparams(*sem, **kw):
    return pltpu.CompilerParams(dimension_semantics=sem, vmem_limit_bytes=VMEM_LIMIT, **kw)


def _lane_iota(shape):
    return lax.broadcasted_iota(jnp.int32, shape, len(shape) - 1)


def _rope(a, cos, sin_signed, half):
    lane = _lane_iota(a.shape)
    swapped = jnp.where(lane % (2 * half) < half,
                        pltpu.roll(a, a.shape[-1] - half, axis=1),
                        pltpu.roll(a, half, axis=1))
    return a * cos + swapped * sin_signed


def _modulated(x_ref, mod_ref, shift_row):
    m = mod_ref[0]
    return x_ref[0] * (1.0 + m[shift_row + 1:shift_row + 2, :]) + m[shift_row:shift_row + 1, :]


def _ada_kernel(c_ref, w_ref, b_ref, o_ref):
    c = c_ref[...]
    s = c * jax.nn.sigmoid(c)
    o_ref[0] = jnp.dot(s, w_ref[0], preferred_element_type=_F32,
                       precision=lax.Precision.HIGHEST) + b_ref[0]


def _ada_mods(cs, ada_w, ada_b):
    depth, d, n = ada_w.shape
    r = cs.shape[0]
    tn = 1024
    return pl.pallas_call(
        _ada_kernel,
        grid=(depth, n // tn),
        in_specs=[pl.BlockSpec((r, d), lambda i, j: (0, 0)),
                  pl.BlockSpec((1, d, tn), lambda i, j: (i, 0, j)),
                  pl.BlockSpec((1, 1, tn), lambda i, j: (i, 0, j))],
        out_specs=pl.BlockSpec((1, r, tn), lambda i, j: (i, 0, j)),
        out_shape=jax.ShapeDtypeStruct((depth, r, n), _F32),
        compiler_params=_cparams("parallel", "parallel"),
        name="ada_mods",
    )(cs, ada_w, ada_b.reshape(depth, 1, n))


def _row_specs(b, p, d, n_lat_tiles, tm):
    x_spec = pl.BlockSpec((1, tm, d), lambda i, j: (i, j, 0))
    mod_spec = pl.BlockSpec((1, 6, d), lambda i, j: (jnp.where(j < n_lat_tiles, i, b), 0, 0))
    return x_spec, mod_spec


def _proj_rope_kernel(x_ref, mod_ref, w_ref, cos_ref, sin_ref, o_ref, *vt_ref, n_rope, n_q, q_scale):
    h = _modulated(x_ref, mod_ref, 0).astype(_BF16)
    acc = jnp.dot(h, w_ref[...], preferred_element_type=_F32)
    cos, sin = cos_ref[...], sin_ref[...]
    n = o_ref.shape[2]
    for j in range(n // LANES):
        a = acc[:, j * LANES:(j + 1) * LANES]
        if j * LANES < n_rope:
            a = _rope(a, cos, sin, 16)
        if j * LANES < n_q and q_scale != 1.0:
            a = a * q_scale
        o_ref[0, :, j * LANES:(j + 1) * LANES] = a.astype(_BF16)
    if vt_ref:
        vt_ref[0][0] = acc[:, n:].T.astype(_BF16)


def _proj_rope(x, mods, w, cos, sin, n_rope, n_t, n_lat_tiles, n_q=0, q_scale=1.0):
    b, p, d = x.shape
    n_all = w.shape[1]
    n = n_all - n_t
    tm = ROW_TILE
    x_spec, mod_spec = _row_specs(b, p, d, n_lat_tiles, tm)
    out_specs = [pl.BlockSpec((1, tm, n), lambda i, j: (i, j, 0))]
    out_shape = [jax.ShapeDtypeStruct((b, p, n), _BF16)]
    if n_t:
        out_specs.append(pl.BlockSpec((1, n_t, tm), lambda i, j: (i, 0, j)))
        out_shape.append(jax.ShapeDtypeStruct((b, n_t, p), _BF16))
    return pl.pallas_call(
        functools.partial(_proj_rope_kernel, n_rope=n_rope, n_q=n_q, q_scale=q_scale),
        grid=(b, p // tm),
        in_specs=[x_spec, mod_spec,
                  pl.BlockSpec((d, n_all), lambda i, j: (0, 0)),
                  pl.BlockSpec((tm, LANES), lambda i, j: (j, 0)),
                  pl.BlockSpec((tm, LANES), lambda i, j: (j, 0))],
        out_specs=out_specs,
        out_shape=out_shape,
        compiler_params=_cparams("parallel", "parallel"),
        name="proj_rope",
    )(x, mods, w, cos, sin)


def _rms(x, g):
    return x * lax.rsqrt(jnp.mean(x * x, axis=-1, keepdims=True) + RMS_EPS) * g


def _proj_mla_kernel(x_ref, mod_ref, win_ref, qn_ref, kvn_ref, wuq_ref, wuk_ref, wv_ref,
                     cos_ref, sin_ref, q_ref, k_ref, vt_ref, *, scale):
    h = _modulated(x_ref, mod_ref, 0).astype(_BF16)
    c = jnp.dot(h, win_ref[...], preferred_element_type=_F32)
    cq = _rms(c[:, :B_Q_RANK], qn_ref[...]).astype(_BF16)
    ckv = _rms(c[:, B_Q_RANK:B_Q_RANK + B_KV_RANK], kvn_ref[...]).astype(_BF16)
    cos, sin = cos_ref[...], sin_ref[...]
    k_rope = _rope(c[:, B_Q_RANK + B_KV_RANK:], cos, sin, 8)
    q = jnp.dot(cq, wuq_ref[...], preferred_element_type=_F32)
    k_nope = jnp.dot(ckv, wuk_ref[...], preferred_element_type=_F32)
    for hd in range(B_HEADS):
        sl = slice(hd * LANES, (hd + 1) * LANES)
        q_ref[0, :, sl] = (_rope(q[:, sl], cos, sin, 8) * scale).astype(_BF16)
        k_ref[0, :, sl] = (k_nope[:, sl] + k_rope).astype(_BF16)
    vt_ref[0] = jnp.dot(ckv, wv_ref[...], preferred_element_type=_F32).T.astype(_BF16)


def _proj_mla(x, mods, w_in, q_norm, kv_norm, w_uq, w_uk, w_v, cos, sin, n_lat_tiles):
    b, p, d = x.shape
    tm = ROW_TILE
    x_spec, mod_spec = _row_specs(b, p, d, n_lat_tiles, tm)
    full = lambda a: pl.BlockSpec(a.shape, lambda i, j: (0,) * a.ndim)
    nq, nv = w_uq.shape[1], w_v.shape[1]
    tile = lambda n: pl.BlockSpec((1, tm, n), lambda i, j: (i, j, 0))
    return pl.pallas_call(
        functools.partial(_proj_mla_kernel, scale=(B_NOPE + B_ROPE) ** -0.5 * LOG2E),
        grid=(b, p // tm),
        in_specs=[x_spec, mod_spec, full(w_in), full(q_norm), full(kv_norm), full(w_uq), full(w_uk),
                  full(w_v),
                  pl.BlockSpec((tm, LANES), lambda i, j: (j, 0)),
                  pl.BlockSpec((tm, LANES), lambda i, j: (j, 0))],
        out_specs=[tile(nq), tile(nq), pl.BlockSpec((1, nv, tm), lambda i, j: (i, 0, j))],
        out_shape=[jax.ShapeDtypeStruct((b, p, nq), _BF16), jax.ShapeDtypeStruct((b, p, nq), _BF16),
                   jax.ShapeDtypeStruct((b, nv, p), _BF16)],
        compiler_params=_cparams("parallel", "parallel"),
        name="proj_mla",
    )(x, mods, w_in, q_norm, kv_norm, w_uq, w_uk, w_v, cos, sin)


def _attn_win_kernel(sink_ref, q_ref, k0_ref, k1_ref, k2_ref, kc_ref, vt0_ref, vt1_ref, vt2_ref, vtc_ref,
                     o_ref, *, n_lat_tiles):
    j = pl.program_id(1)
    w = WIN_BLOCK
    n_ctx = kc_ref.shape[1]
    n_keys = 3 * w + n_ctx
    nq = A_GROUP * w
    row = lax.broadcasted_iota(jnp.int32, (n_keys, nq), 0)
    q_pos = j * w + lax.broadcasted_iota(jnp.int32, (n_keys, nq), 1) % w
    k_pos = (j - 1) * w + row
    n_latent_keys = jnp.where(j < n_lat_tiles, n_lat_tiles * w, 0)
    valid = (row >= 3 * w) | ((jnp.abs(q_pos - k_pos) <= A_WINDOW) & (k_pos >= 0) & (k_pos < n_latent_keys))
    low = _lane_iota((w, LANES)) < A_HEAD_DIM

    def scores(kvh):
        cs = slice(kvh * LANES, (kvh + 1) * LANES)
        kk = jnp.concatenate([k0_ref[0, :, cs], k1_ref[0, :, cs], k2_ref[0, :, cs], kc_ref[0, :, cs]], axis=0)
        qa = q_ref[0, :, 2 * kvh * LANES:(2 * kvh + 1) * LANES]
        qb = q_ref[0, :, (2 * kvh + 1) * LANES:(2 * kvh + 2) * LANES]
        zero = jnp.zeros_like(qa)
        q4 = jnp.concatenate([jnp.where(low, qa, zero), jnp.where(low, zero, qa),
                              jnp.where(low, qb, zero), jnp.where(low, zero, qb)], axis=0)
        return lax.dot_general(kk, q4, _NT, preferred_element_type=_F32)

    outs = []
    s_next = scores(0)
    for kvh in range(A_KV_HEADS):
        s = s_next
        if kvh + 1 < A_KV_HEADS:
            s_next = scores(kvh + 1)
        rs = slice(kvh * A_HEAD_DIM, (kvh + 1) * A_HEAD_DIM)
        vt = jnp.concatenate([vt0_ref[0, rs, :], vt1_ref[0, rs, :], vt2_ref[0, rs, :], vtc_ref[0, rs, :]], axis=1)
        sink = jnp.concatenate([jnp.full((1, w), sink_ref[kvh * A_GROUP + g] * LOG2E, _F32)
                                for g in range(A_GROUP)], axis=1)
        s = jnp.where(valid, s, -jnp.inf)
        m = jnp.maximum(jnp.max(s, axis=0, keepdims=True), sink)
        e = jnp.exp2(s - m)
        denom = jnp.sum(e, axis=0, keepdims=True) + jnp.exp2(sink - m)
        o = jnp.dot(vt, e.astype(_BF16), preferred_element_type=_F32) / denom
        outs += [o[:, g * w:(g + 1) * w] for g in range(A_GROUP)]
    o_ref[0] = jnp.concatenate(outs, axis=0).T.astype(_BF16)


def _attn_win(qk, vt, sink, s_len, l_ctx):
    b, p, _ = qk.shape
    w = WIN_BLOCK
    n_lat = s_len // w
    nq = A_HEADS * A_HEAD_DIM
    kw = A_KV_HEADS * LANES
    nv = A_KV_HEADS * A_HEAD_DIM
    ctx_blk = s_len // l_ctx

    def nb(off):
        return lambda i, j: (i, jnp.clip(j + off, 0, n_lat - 1), nq // kw)

    def nbt(off):
        return lambda i, j: (i, 0, jnp.clip(j + off, 0, n_lat - 1))

    return pl.pallas_call(
        functools.partial(_attn_win_kernel, n_lat_tiles=n_lat),
        grid=(b, p // w),
        in_specs=[pl.BlockSpec(memory_space=pltpu.SMEM),
                  pl.BlockSpec((1, w, nq), lambda i, j: (i, j, 0)),
                  pl.BlockSpec((1, w, kw), nb(-1)), pl.BlockSpec((1, w, kw), nb(0)),
                  pl.BlockSpec((1, w, kw), nb(1)),
                  pl.BlockSpec((1, l_ctx, kw), lambda i, j: (i, ctx_blk, nq // kw)),
                  pl.BlockSpec((1, nv, w), nbt(-1)), pl.BlockSpec((1, nv, w), nbt(0)),
                  pl.BlockSpec((1, nv, w), nbt(1)),
                  pl.BlockSpec((1, nv, l_ctx), lambda i, j: (i, 0, ctx_blk))],
        out_specs=pl.BlockSpec((1, w, nq), lambda i, j: (i, j, 0)),
        out_shape=jax.ShapeDtypeStruct((b, p, nq), _BF16),
        compiler_params=_cparams("parallel", "parallel"),
        name="attn_win",
    )(sink, qk, qk, qk, qk, qk, vt, vt, vt, vt)


def _key_chunks(s_len, l_ctx, latent_query):
    chunks = [(lo, KEY_CHUNK) for lo in range(0, s_len, KEY_CHUNK)] if latent_query else []
    return chunks + [(s_len, l_ctx)]


def _flash_keys_major(q, k_chunk, vt_chunk, chunks):
    def scores(c):
        return lax.dot_general(k_chunk(*chunks[c]), q, _NT, preferred_element_type=_F32)

    m = acc = l = None
    pending = None
    s_next = scores(0)

    def flush(acc):
        (lo, n), e, corr = pending
        pv = jnp.dot(vt_chunk(lo, n), e, preferred_element_type=_F32)
        return pv if acc is None else acc * corr + pv

    for c in range(len(chunks)):
        s = s_next
        if c + 1 < len(chunks):
            s_next = scores(c + 1)
        if pending is not None:
            acc = flush(acc)
        m_chunk = jnp.max(s, axis=0, keepdims=True)
        m_new = m_chunk if m is None else jnp.maximum(m, m_chunk)
        e = jnp.exp2(s - m_new)
        l_chunk = jnp.sum(e, axis=0, keepdims=True)
        corr = None if m is None else jnp.exp2(m - m_new)
        l = l_chunk if m is None else l * corr + l_chunk
        pending = (chunks[c], e.astype(_BF16), corr)
        m = m_new
    return flush(acc), l


def _per_step_tiles(tile, n_lat_steps):
    j = pl.program_id(2)

    @pl.when(j < n_lat_steps)
    def _():
        for t in range(Q_TILES_PER_STEP):
            tile(t, True)

    @pl.when(j >= n_lat_steps)
    def _():
        tile(0, False)


def _attn_mla_kernel(q_ref, k_ref, vt_ref, o_ref, *, n_lat_steps, s_len, l_ctx):
    tq = ATTN_Q_TILE

    def tile(t, latent_query):
        rows = slice(t * tq, (t + 1) * tq)
        q = q_ref[0, rows, :]
        first = _lane_iota(q.shape) < LANES
        zero = jnp.zeros_like(q)
        q2 = jnp.concatenate([jnp.where(first, q, zero), jnp.where(first, zero, q)], axis=0)
        acc, l = _flash_keys_major(q2, lambda lo, n: k_ref[0, lo:lo + n, :],
                                   lambda lo, n: vt_ref[0, :, lo:lo + n],
                                   _key_chunks(s_len, l_ctx, latent_query))
        o2 = acc / l
        o = jnp.concatenate([o2[:B_V, :tq], o2[B_V:, tq:]], axis=0)
        o_ref[0, rows, :] = o.T.astype(_BF16)

    _per_step_tiles(tile, n_lat_steps)


def _attn_mla(q, k, vt, s_len):
    b, p, _ = q.shape
    tq = ATTN_Q_TILE * Q_TILES_PER_STEP
    pairs = B_HEADS // 2
    return pl.pallas_call(
        functools.partial(_attn_mla_kernel, n_lat_steps=s_len // tq, s_len=s_len, l_ctx=p - s_len),
        grid=(b, pairs, pl.cdiv(p, tq)),
        in_specs=[pl.BlockSpec((1, tq, 2 * LANES), lambda i, c, j: (i, j, c)),
                  pl.BlockSpec((1, p, 2 * LANES), lambda i, c, j: (i, 0, c)),
                  pl.BlockSpec((1, 2 * B_V, p), lambda i, c, j: (i, c, 0))],
        out_specs=pl.BlockSpec((1, tq, 2 * B_V), lambda i, c, j: (i, j, c)),
        out_shape=jax.ShapeDtypeStruct((b, p, B_HEADS * B_V), _BF16),
        compiler_params=_cparams("parallel", "parallel", "parallel"),
        name="attn_mla",
    )(q, k, vt)


def _attn_diff_kernel(lam_ref, subln_ref, q_ref, k_ref, vt_ref, o_ref, *, n_lat_steps, s_len, l_ctx,
                      lambda_init):
    tq = ATTN_Q_TILE
    lp = lam_ref[...]
    lam = (jnp.exp(jnp.sum(lp[0:1] * lp[1:2], axis=-1, keepdims=True))
           - jnp.exp(jnp.sum(lp[2:3] * lp[3:4], axis=-1, keepdims=True)) + lambda_init)

    def tile(t, latent_query):
        rows = slice(t * tq, (t + 1) * tq)
        q = q_ref[0, rows, :]
        low = _lane_iota(q.shape) < C_HEAD_DIM
        zero = jnp.zeros_like(q)
        q2 = jnp.concatenate([jnp.where(low, q, zero), jnp.where(low, zero, q)], axis=0)
        acc, l = _flash_keys_major(q2, lambda lo, n: k_ref[0, lo:lo + n, :],
                                   lambda lo, n: vt_ref[0, :, lo:lo + n],
                                   _key_chunks(s_len, l_ctx, latent_query))
        o2 = acc / l
        o = o2[:, :tq] - lam * o2[:, tq:]
        o = o * lax.rsqrt(jnp.mean(o * o, axis=0, keepdims=True) + RMS_EPS) * subln_ref[...]
        o_ref[0, rows, :] = (o * (1.0 - lambda_init)).T.astype(_BF16)

    _per_step_tiles(tile, n_lat_steps)


def _attn_diff(qk, vt, lam_params, subln, s_len, lambda_init):
    b, p, _ = qk.shape
    tq = ATTN_Q_TILE * Q_TILES_PER_STEP
    return pl.pallas_call(
        functools.partial(_attn_diff_kernel, n_lat_steps=s_len // tq, s_len=s_len, l_ctx=p - s_len,
                          lambda_init=lambda_init),
        grid=(b, C_HEADS, pl.cdiv(p, tq)),
        in_specs=[pl.BlockSpec(lam_params.shape, lambda i, h, j: (0, 0)),
                  pl.BlockSpec((C_V_DIM, 1), lambda i, h, j: (0, 0)),
                  pl.BlockSpec((1, tq, LANES), lambda i, h, j: (i, j, h)),
                  pl.BlockSpec((1, p, LANES), lambda i, h, j: (i, 0, C_HEADS + h)),
                  pl.BlockSpec((1, C_V_DIM, p), lambda i, h, j: (i, h, 0))],
        out_specs=pl.BlockSpec((1, tq, LANES), lambda i, h, j: (i, j, h)),
        out_shape=jax.ShapeDtypeStruct((b, p, C_HEADS * C_V_DIM), _BF16),
        compiler_params=_cparams("parallel", "parallel", "parallel"),
        name="attn_diff",
    )(lam_params, subln.reshape(C_V_DIM, 1), qk, qk, vt)


def _layer_norm(z, g, b):
    mu = jnp.mean(z, axis=-1, keepdims=True)
    zc = z - mu
    var = jnp.mean(zc * zc, axis=-1, keepdims=True)
    return zc * lax.rsqrt(var + LN_EPS) * g + b


def _out_proj_kernel(o_ref, w_ref, x_ref, mod_ref, g_ref, b_ref, wr_ref, br_ref,
                     xo_ref, h_ref, lg_ref, *, alpha):
    y = jnp.dot(o_ref[0], w_ref[...], preferred_element_type=_F32)
    m = mod_ref[0]
    x = _layer_norm(alpha * x_ref[0] + m[2:3, :] * y, g_ref[...], b_ref[...])
    xo_ref[0] = x
    h = x * (1.0 + m[4:5, :]) + m[3:4, :]
    h_ref[0] = h
    tm = h.shape[0]
    h_hi = h.astype(_BF16)
    h_lo = (h - h_hi.astype(_F32)).astype(_BF16)
    t = jnp.dot(jnp.concatenate([h_hi, h_lo], axis=0), wr_ref[...], preferred_element_type=_F32)
    lg_ref[0] = t[:tm, :LANES] + t[tm:, :LANES] + t[:tm, LANES:] + br_ref[...]


def _out_proj(o, w_out, x, mods, ln_g, ln_b, w_router, b_router, alpha, n_lat_tiles):
    b, p, d = x.shape
    tm = ROW_TILE
    x_spec, mod_spec = _row_specs(b, p, d, n_lat_tiles, tm)
    full = lambda a: pl.BlockSpec(a.shape, lambda i, j: (0,) * a.ndim)
    n_o = o.shape[2]
    return pl.pallas_call(
        functools.partial(_out_proj_kernel, alpha=alpha),
        grid=(b, p // tm),
        in_specs=[pl.BlockSpec((1, tm, n_o), lambda i, j: (i, j, 0)), full(w_out), x_spec, mod_spec,
                  full(ln_g), full(ln_b), full(w_router), full(b_router)],
        out_specs=[x_spec, x_spec, pl.BlockSpec((1, tm, LANES), lambda i, j: (i, j, 0))],
        out_shape=[jax.ShapeDtypeStruct((b, p, d), _F32), jax.ShapeDtypeStruct((b, p, d), _F32),
                   jax.ShapeDtypeStruct((b, p, LANES), _F32)],
        compiler_params=_cparams("parallel", "parallel"),
        name="out_proj_ln",
    )(o, w_out, x, mods, ln_g, ln_b, w_router, b_router)


def _route_kernel(lg_ref, info_ref, btab_ref, cnt_ref, start_ref, run_ref, *, block_rows):
    sweep, i = pl.program_id(0), pl.program_id(1)
    x = lg_ref[...]
    tm = x.shape[0]
    lane = _lane_iota(x.shape)
    xg = jnp.where(lane < N_GROUPS, x, -jnp.inf)
    g_max = jnp.max(xg, axis=-1, keepdims=True)
    g_sel = jnp.min(jnp.where(xg == g_max, lane, LANES), axis=-1, keepdims=True)
    p_g = 1.0 / jnp.sum(jnp.exp(xg - g_max), axis=-1, keepdims=True)
    lo = N_GROUPS + g_sel * EXPERTS_PER_GROUP
    xe = jnp.where((lane >= lo) & (lane < lo + EXPERTS_PER_GROUP), x, -jnp.inf)
    v1 = jnp.max(xe, axis=-1, keepdims=True)
    i1 = jnp.min(jnp.where(xe == v1, lane, LANES), axis=-1, keepdims=True)
    xe = jnp.where(lane == i1, -jnp.inf, xe)
    v2 = jnp.max(xe, axis=-1, keepdims=True)
    i2 = jnp.min(jnp.where(xe == v2, lane, LANES), axis=-1, keepdims=True)
    hit1, hit2 = lane == i1, lane == i2
    onehot = (hit1 | hit2).astype(_F32)

    @pl.when((sweep == 0) & (i == 0))
    def _():
        cnt_ref[...] = jnp.zeros_like(cnt_ref)

    @pl.when(sweep == 0)
    def _():
        cnt_ref[...] += jnp.sum(onehot, axis=0, keepdims=True)

    @pl.when((sweep == 1) & (i == 0))
    def _():
        cnt = cnt_ref[...]
        padded = jnp.floor((cnt + (block_rows - 1)) * (1.0 / block_rows)) * block_rows
        r = lax.broadcasted_iota(jnp.int32, (LANES, LANES), 0)
        c = lax.broadcasted_iota(jnp.int32, (LANES, LANES), 1)
        before = (r < c).astype(_F32)
        start = jnp.dot(jnp.broadcast_to(padded, (8, LANES)), before, preferred_element_type=_F32,
                        precision=lax.Precision.HIGHEST)[0:1]
        start_ref[...] = start
        run_ref[...] = jnp.zeros_like(run_ref)
        nb = btab_ref.shape[0]
        row0 = (lax.broadcasted_iota(jnp.int32, (nb, LANES), 0) * block_rows).astype(_F32)
        is_e = (_lane_iota((nb, LANES)) >= N_GROUPS) & (_lane_iota((nb, LANES)) < N_GROUPS + N_EXPERTS)
        end = start + padded
        owner = jnp.sum(jnp.where(is_e & (end <= row0), 1.0, 0.0), axis=-1, keepdims=True)
        inside = is_e & (start <= row0) & (row0 < end)
        real = jnp.sum(jnp.where(inside, jnp.clip(cnt - (row0 - start), 0.0, block_rows), 0.0),
                       axis=-1, keepdims=True)
        bl = _lane_iota((nb, LANES))
        btab_ref[...] = jnp.where(bl == 0, jnp.minimum(owner, N_EXPERTS - 1.0),
                                  jnp.where(bl == 1, real, 0.0)).astype(jnp.int32)

    @pl.when(sweep == 1)
    def _():
        row = lax.broadcasted_iota(jnp.int32, (tm, tm), 0)
        col = lax.broadcasted_iota(jnp.int32, (tm, tm), 1)
        earlier = (col < row).astype(_BF16)
        rank = jnp.dot(earlier, onehot.astype(_BF16), preferred_element_type=_F32) + run_ref[...]
        pos = rank + start_ref[...]
        d1 = jnp.sum(jnp.where(hit1, pos, 0.0), axis=-1, keepdims=True)
        d2 = jnp.sum(jnp.where(hit2, pos, 0.0), axis=-1, keepdims=True)
        run_ref[...] += jnp.sum(onehot, axis=0, keepdims=True)
        t = jnp.exp(v2 - v1)
        w1 = p_g / (1.0 + t)
        info_ref[...] = jnp.where(lane == 0, d1, jnp.where(lane == 1, d2, jnp.where(lane == 2, w1,
                                  jnp.where(lane == 3, w1 * t, 0.0))))


def _route(logits, n_blocks):
    t = logits.shape[0]
    tm = ROUTE_TILE
    nb_pad = -(-n_blocks // 8) * 8
    return pl.pallas_call(
        functools.partial(_route_kernel, block_rows=MOE_ROWS),
        grid=(2, t // tm),
        in_specs=[pl.BlockSpec((tm, LANES), lambda s, i: (i, 0))],
        out_specs=[pl.BlockSpec((tm, LANES), lambda s, i: (i * s, 0)),
                   pl.BlockSpec((nb_pad, LANES), lambda s, i: (0, 0))],
        out_shape=[jax.ShapeDtypeStruct((t, LANES), _F32), jax.ShapeDtypeStruct((nb_pad, LANES), jnp.int32)],
        scratch_shapes=[pltpu.VMEM((1, LANES), _F32)] * 3,
        compiler_params=_cparams("arbitrary", "arbitrary"),
        name="moe_route",
    )(logits)


def _row_copy_wait(src_rows, dst_rows, sem):
    pltpu.make_async_copy(src_rows, dst_rows, sem).wait()


def _dispatch_kernel(dest_ref, btab_ref, h_ref, xb_ref, zeros, sem, zero_sem, *, n_blocks):
    tm = h_ref.shape[0]
    bm = MOE_ROWS
    base = pl.program_id(0) * tm

    @pl.when(pl.program_id(0) == 0)
    def _():
        zeros[...] = jnp.zeros_like(zeros)

        def tail_rows(blk):
            real = btab_ref[blk, 1]
            return jnp.where(real > 0, bm - real, 0)

        def tail_copy(row):
            return pltpu.make_async_copy(zeros.at[pl.ds(0, 1)], xb_ref.at[pl.ds(row, 1)], zero_sem)

        def block_copy(blk):
            return pltpu.make_async_copy(zeros, xb_ref.at[pl.ds(blk * bm, bm)], zero_sem)

        for blk in range(n_blocks):
            first = blk * bm + btab_ref[blk, 1]
            lax.fori_loop(0, tail_rows(blk), lambda r, c: (tail_copy(first + r).start(), c)[1], 0)
            pl.when(btab_ref[blk, 1] == 0)(lambda: block_copy(blk).start())
        for blk in range(n_blocks):
            lax.fori_loop(0, tail_rows(blk), lambda r, c: (tail_copy(0).wait(), c)[1], 0)
            pl.when(btab_ref[blk, 1] == 0)(lambda: block_copy(blk).wait())

    for r in range(tm):
        for k in range(TOP_K):
            d = dest_ref[(base + r) * TOP_K + k]
            pltpu.make_async_copy(h_ref.at[pl.ds(r, 1)], xb_ref.at[pl.ds(d, 1)], sem).start()
    for k in range(TOP_K):
        _row_copy_wait(h_ref, xb_ref.at[pl.ds(0, tm)], sem)


def _dispatch(dest, btab, h, n_blocks):
    t, d = h.shape
    tm = ROW_TILE
    grid_spec = pltpu.PrefetchScalarGridSpec(
        num_scalar_prefetch=2,
        grid=(t // tm,),
        in_specs=[pl.BlockSpec((tm, d), lambda i, dest, bt: (i, 0))],
        out_specs=pl.BlockSpec(memory_space=pl.ANY),
        scratch_shapes=[pltpu.VMEM((MOE_ROWS, d), h.dtype), pltpu.SemaphoreType.DMA(()),
                        pltpu.SemaphoreType.DMA(())],
    )
    return pl.pallas_call(
        functools.partial(_dispatch_kernel, n_blocks=n_blocks),
        grid_spec=grid_spec,
        out_shape=jax.ShapeDtypeStruct((n_blocks * MOE_ROWS, d), h.dtype),
        compiler_params=_cparams("arbitrary", disable_bounds_checks=True),
        name="moe_dispatch",
    )(dest, btab, h)


def _ffn_kernel(btab_ref, x_ref, wg_ref, wu_ref, wd_ref, y_ref, wg_bf, wu_bf, wd_bf):
    i = pl.program_id(0)
    expert, real = btab_ref[i, 0], btab_ref[i, 1]
    prev = btab_ref[jnp.maximum(i - 1, 0), 0]

    @pl.when((i == 0) | (expert != prev))
    def _():
        wg_bf[...] = wg_ref[0, 0].astype(_BF16)
        wu_bf[...] = wu_ref[0, 0].astype(_BF16)
        wd_bf[...] = wd_ref[0, 0].astype(_BF16)

    @pl.when(real > 0)
    def _():
        x = x_ref[...].astype(_BF16)
        g = jnp.dot(x, wg_bf[...], preferred_element_type=_F32)
        u = jnp.dot(x, wu_bf[...], preferred_element_type=_F32)
        a = (g * jax.nn.sigmoid(g) * u).astype(_BF16)
        y_ref[...] = jnp.dot(a, wd_bf[...], preferred_element_type=_F32)

    @pl.when(real <= 0)
    def _():
        y_ref[...] = jnp.zeros_like(y_ref)


def _expert_ffn(btab, xb, w_gate, w_up, w_down, layer):
    rows, d = xb.shape
    de = w_gate.shape[3]
    bm = MOE_ROWS
    w_map = lambda i, bt: (layer, bt[i, 0], 0, 0)
    grid_spec = pltpu.PrefetchScalarGridSpec(
        num_scalar_prefetch=1,
        grid=(rows // bm,),
        in_specs=[pl.BlockSpec((bm, d), lambda i, bt: (i, 0)),
                  pl.BlockSpec((1, 1, d, de), w_map),
                  pl.BlockSpec((1, 1, d, de), w_map),
                  pl.BlockSpec((1, 1, de, d), w_map)],
        out_specs=pl.BlockSpec((bm, d), lambda i, bt: (i, 0)),
        scratch_shapes=[pltpu.VMEM((d, de), _BF16), pltpu.VMEM((d, de), _BF16), pltpu.VMEM((de, d), _BF16)],
    )
    return pl.pallas_call(
        _ffn_kernel,
        grid_spec=grid_spec,
        out_shape=jax.ShapeDtypeStruct((rows, d), _F32),
        compiler_params=_cparams("arbitrary"),
        name="expert_ffn",
    )(btab, xb, w_gate, w_up, w_down)


def _combine_kernel(dest_ref, info_ref, x_ref, mod_ref, g_ref, b_ref, yb_ref, xo_ref, buf, sem, *, alpha):
    tm = x_ref.shape[1]
    nj = pl.num_programs(1)
    step = pl.program_id(0) * nj + pl.program_id(1)
    n_steps = pl.num_programs(0) * nj

    def issue(at_step, slot):
        base = at_step * tm

        for r in range(tm):
            for k in range(TOP_K):
                d = dest_ref[(base + r) * TOP_K + k]
                pltpu.make_async_copy(yb_ref.at[pl.ds(d, 1)], buf.at[slot, k, pl.ds(r, 1)], sem.at[slot]).start()

    @pl.when(step == 0)
    def _():
        issue(0, 0)

    @pl.when(step + 1 < n_steps)
    def _():
        issue(step + 1, (step + 1) % 2)

    slot = step % 2
    for k in range(TOP_K):
        _row_copy_wait(yb_ref.at[pl.ds(0, tm)], buf.at[slot, k], sem.at[slot])
    info = info_ref[0]
    y = info[:, 2:3] * buf[slot, 0] + info[:, 3:4] * buf[slot, 1]
    m = mod_ref[0]
    xo_ref[0] = _layer_norm(alpha * x_ref[0] + m[5:6, :] * y, g_ref[...], b_ref[...])


def _combine(dest, info, yb, x, mods, ln_g, ln_b, alpha, n_lat_tiles):
    b, p, d = x.shape
    tm = ROW_TILE
    grid_spec = pltpu.PrefetchScalarGridSpec(
        num_scalar_prefetch=1,
        grid=(b, p // tm),
        in_specs=[pl.BlockSpec((1, tm, LANES), lambda i, j, dest: (i, j, 0)),
                  pl.BlockSpec((1, tm, d), lambda i, j, dest: (i, j, 0)),
                  pl.BlockSpec((1, 6, d), lambda i, j, dest: (jnp.where(j < n_lat_tiles, i, b), 0, 0)),
                  pl.BlockSpec(ln_g.shape, lambda i, j, dest: (0, 0)),
                  pl.BlockSpec(ln_b.shape, lambda i, j, dest: (0, 0)),
                  pl.BlockSpec(memory_space=pl.ANY)],
        out_specs=pl.BlockSpec((1, tm, d), lambda i, j, dest: (i, j, 0)),
        scratch_shapes=[pltpu.VMEM((2, TOP_K, tm, d), _F32), pltpu.SemaphoreType.DMA((2,))],
    )
    return pl.pallas_call(
        functools.partial(_combine_kernel, alpha=alpha),
        grid_spec=grid_spec,
        out_shape=jax.ShapeDtypeStruct((b, p, d), _F32),
        compiler_params=_cparams("arbitrary", "arbitrary", disable_bounds_checks=True),
        name="moe_combine_ln",
    )(dest, info.reshape(b, p, LANES), x, mods, ln_g, ln_b, yb)


def _moe_layer(h, logits, x, mods, ln_g, ln_b, w_gate, w_up, w_down, layer, alpha, n_lat_tiles):
    t, d = h.shape
    n_blocks = (t * TOP_K) // MOE_ROWS + N_EXPERTS
    info, btab = _route(logits, n_blocks)
    dest = info[:, :TOP_K].astype(jnp.int32).reshape(t * TOP_K)
    xb = _dispatch(dest, btab, h, n_blocks)
    yb = _expert_ffn(btab, xb, w_gate, w_up, w_down, layer)
    return _combine(dest, info, yb, x, mods, ln_g, ln_b, alpha, n_lat_tiles)


def _rope_tables(s_len, l_ctx, rot_dim, lane_lo, period):
    rows = s_len // GRID_W
    row = jnp.repeat(jnp.arange(rows, dtype=_F32), GRID_W)
    col = jnp.tile(jnp.arange(GRID_W, dtype=_F32), rows)
    axis_dim = rot_dim // 2
    half = axis_dim // 2
    inv_freq = ROPE_THETA ** (-jnp.arange(0, axis_dim, 2, dtype=_F32) / axis_dim)
    lane = jnp.arange(LANES)
    rel = (lane % period) - lane_lo
    active = (rel >= 0) & (rel < rot_dim)
    rel = jnp.clip(rel, 0, rot_dim - 1)
    use_col = rel >= axis_dim
    f = inv_freq[(rel % axis_dim) % half]
    ang = jnp.where(use_col[None, :], col[:, None], row[:, None]) * f[None, :]
    sign = jnp.where((rel % axis_dim) < half, -1.0, 1.0)
    cos = jnp.where(active[None, :], jnp.cos(ang), 1.0)
    sin = jnp.where(active[None, :], jnp.sin(ang) * sign[None, :], 0.0)
    ident = jnp.ones((l_ctx, LANES), _F32)
    return (jnp.concatenate([cos, ident], axis=0), jnp.concatenate([sin, 0.0 * ident], axis=0))


def _win_weights(w_in):
    d = w_in.shape[0]
    nq, nkv = A_HEADS * A_HEAD_DIM, A_KV_HEADS * A_HEAD_DIM
    k = w_in[:, nq:nq + nkv].reshape(d, A_KV_HEADS, 1, A_HEAD_DIM)
    k_dup = jnp.concatenate([k, k], axis=2).reshape(d, 2 * nkv)
    return jnp.concatenate([w_in[:, :nq], k_dup, w_in[:, nq + nkv:]], axis=1).astype(_BF16)


def _mla_weights(w_in, w_uq, w_ukv):
    d = w_in.shape[0]
    pad = LANES - B_NOPE - B_ROPE
    kr = jnp.concatenate([jnp.zeros((d, B_NOPE), _F32), w_in[:, B_Q_RANK + B_KV_RANK:],
                          jnp.zeros((d, pad), _F32)], axis=1)
    w_in_p = jnp.concatenate([w_in[:, :B_Q_RANK + B_KV_RANK], kr], axis=1).astype(_BF16)
    uq = w_uq.reshape(B_Q_RANK, B_HEADS, B_NOPE + B_ROPE)
    uq = jnp.pad(uq, ((0, 0), (0, 0), (0, pad))).reshape(B_Q_RANK, B_HEADS * LANES).astype(_BF16)
    ukv = w_ukv.reshape(B_KV_RANK, B_HEADS, B_NOPE + B_V)
    uk = jnp.pad(ukv[:, :, :B_NOPE], ((0, 0), (0, 0), (0, LANES - B_NOPE)))
    uk = uk.reshape(B_KV_RANK, B_HEADS * LANES).astype(_BF16)
    uv = ukv[:, :, B_NOPE:].reshape(B_KV_RANK, B_HEADS * B_V).astype(_BF16)
    return w_in_p, uq, uk, uv


def kernel(x, c, ctx, c_ctx, ada_w, ada_b, ln_mix_g, ln_mix_b, ln_ffn_g, ln_ffn_b, win_w_in, win_w_out, win_sink, mla_w_in, mla_q_norm, mla_kv_norm, mla_w_uq, mla_w_ukv, mla_w_out, dif_w_in, dif_lambda, dif_subln, dif_w_out, moe_w_grp, moe_b_grp, moe_w_rt, moe_b_rt, moe_w_gate, moe_w_up, moe_w_down):
    b, s_len, d = x.shape
    l_ctx = ctx.shape[1]
    depth = ada_w.shape[0]
    p = s_len + l_ctx
    assert s_len % ROW_TILE == 0 and l_ctx % ROW_TILE == 0 and s_len % GRID_W == 0
    assert s_len % (ATTN_Q_TILE * Q_TILES_PER_STEP) == 0 and l_ctx == ATTN_Q_TILE and s_len % l_ctx == 0
    assert s_len % KEY_CHUNK == 0 and MOE_ROWS & (MOE_ROWS - 1) == 0 and (b * p) % ROUTE_TILE == 0
    alpha = (2.0 * depth) ** 0.25
    n_lat_tiles = s_len // ROW_TILE

    xs = jnp.concatenate([x, ctx], axis=1)
    cs = jnp.concatenate([c, c_ctx[None, :]], axis=0)
    mods = _ada_mods(cs, ada_w, ada_b).reshape(depth, b + 1, 6, d)
    cos64, sin64 = _rope_tables(s_len, l_ctx, A_HEAD_DIM, 0, A_HEAD_DIM)
    cos32, sin32 = _rope_tables(s_len, l_ctx, B_ROPE, B_NOPE, LANES)
    row = lambda v: v.reshape(1, -1)

    for i in range(depth):
        kind, slot = i % N_MIXERS, i // N_MIXERS
        if kind == 0:
            qk, vt = _proj_rope(xs, mods[i], _win_weights(win_w_in[slot]), cos64, sin64,
                                (A_HEADS + 2 * A_KV_HEADS) * A_HEAD_DIM, A_KV_HEADS * A_HEAD_DIM, n_lat_tiles,
                                n_q=A_HEADS * A_HEAD_DIM, q_scale=A_HEAD_DIM ** -0.5 * LOG2E)
            o = _attn_win(qk, vt, win_sink[slot], s_len, l_ctx)
            w_out = win_w_out[slot]
        elif kind == 1:
            w_in_p, uq, uk, uv = _mla_weights(mla_w_in[slot], mla_w_uq[slot], mla_w_ukv[slot])
            q, k, vt = _proj_mla(xs, mods[i], w_in_p, row(mla_q_norm[slot]), row(mla_kv_norm[slot]),
                                uq, uk, uv, cos32, sin32, n_lat_tiles)
            o = _attn_mla(q, k, vt, s_len)
            w_out = mla_w_out[slot]
        else:
            lambda_init = 0.8 - 0.6 * math.exp(-0.3 * i)
            qk, vt = _proj_rope(xs, mods[i], dif_w_in[slot].astype(_BF16), cos64, sin64,
                                2 * C_HEADS * 2 * C_HEAD_DIM, C_HEADS * C_V_DIM, n_lat_tiles,
                                n_q=C_HEADS * 2 * C_HEAD_DIM, q_scale=C_HEAD_DIM ** -0.5 * LOG2E)
            o = _attn_diff(qk, vt, dif_lambda[slot], dif_subln[slot], s_len, lambda_init)
            w_out = dif_w_out[slot]

        w_router = jnp.concatenate([moe_w_grp[i], moe_w_rt[i],
                                    jnp.zeros((d, LANES - N_GROUPS - N_EXPERTS), _F32)], axis=1)
        b_router = jnp.concatenate([moe_b_grp[i], moe_b_rt[i],
                                    jnp.zeros((LANES - N_GROUPS - N_EXPERTS,), _F32)])[None, :]
        w_hi = w_router.astype(_BF16)
        w_lo = (w_router - w_hi.astype(_F32)).astype(_BF16)
        xs, h, logits = _out_proj(o, w_out.astype(_BF16), xs, mods[i], row(ln_mix_g[i]), row(ln_mix_b[i]),
                                  jnp.concatenate([w_hi, w_lo], axis=1), b_router, alpha, n_lat_tiles)
        xs = _moe_layer(h.reshape(b * p, d), logits.reshape(b * p, LANES), xs, mods[i], row(ln_ffn_g[i]),
                        row(ln_ffn_b[i]), moe_w_gate, moe_w_up, moe_w_down, i, alpha, n_lat_tiles)
    return xs[:, :s_len]
```

```python
import functools
import math

import jax
import jax.numpy as jnp
from jax import lax
from jax.experimental import pallas as pl
from jax.experimental.pallas import tpu as pltpu

LANES = 128
GRID_W = 64
ROPE_THETA = 10000.0
LN_EPS = 1e-5
RMS_EPS = 1e-6
LOG2E = math.log2(math.e)

A_HEADS, A_KV_HEADS, A_HEAD_DIM, A_WINDOW = 16, 4, 64, 128
A_GROUP = A_HEADS // A_KV_HEADS
B_HEADS, B_NOPE, B_ROPE, B_V, B_Q_RANK, B_KV_RANK = 16, 64, 32, 64, 512, 256
C_HEADS, C_HEAD_DIM = 8, 64
C_V_DIM = 2 * C_HEAD_DIM
N_GROUPS, EXPERTS_PER_GROUP, TOP_K = 4, 8, 2
N_EXPERTS = N_GROUPS * EXPERTS_PER_GROUP
N_MIXERS = 3

ROW_TILE = 256
ATTN_Q_TILE = 256
Q_TILES_PER_STEP = 2
WIN_BLOCK = 128
KEY_CHUNK = 1024
MOE_ROWS = 256
ROUTE_TILE = 512
VMEM_LIMIT = 56 * 1024 * 1024

_F32 = jnp.float32
_BF16 = jnp.bfloat16
_NT = (((1,), (1,)), ((), ()))


def _cparams(*sem, **kw):
    return pltpu.CompilerParams(dimension_semantics=sem, vmem_limit_bytes=VMEM_LIMIT, **kw)


def _lane_iota(shape):
    return lax.broadcasted_iota(jnp.int32, shape, len(shape) - 1)


def _rope(a, cos, sin_signed, half):
    lane = _lane_iota(a.shape)
    swapped = jnp.where(lane % (2 * half) < half,
                        pltpu.roll(a, a.shape[-1] - half, axis=1),
                        pltpu.roll(a, half, axis=1))
    return a * cos + swapped * sin_signed


def _modulated(x_ref, mod_ref, shift_row):
    m = mod_ref[0]
    return x_ref[0] * (1.0 + m[shift_row + 1:shift_row + 2, :]) + m[shift_row:shift_row + 1, :]


def _ada_kernel(c_ref, w_ref, b_ref, o_ref):
    c = c_ref[...]
    s = c * jax.nn.sigmoid(c)
    o_ref[0] = jnp.dot(s, w_ref[0], preferred_element_type=_F32,
                       precision=lax.Precision.HIGHEST) + b_ref[0]


def _ada_mods(cs, ada_w, ada_b):
    depth, d, n = ada_w.shape
    r = cs.shape[0]
    tn = 1024
    return pl.pallas_call(
        _ada_kernel,
        grid=(depth, n // tn),
        in_specs=[pl.BlockSpec((r, d), lambda i, j: (0, 0)),
                  pl.BlockSpec((1, d, tn), lambda i, j: (i, 0, j)),
                  pl.BlockSpec((1, 1, tn), lambda i, j: (i, 0, j))],
        out_specs=pl.BlockSpec((1, r, tn), lambda i, j: (i, 0, j)),
        out_shape=jax.ShapeDtypeStruct((depth, r, n), _F32),
        compiler_params=_cparams("parallel", "parallel"),
        name="ada_mods",
    )(cs, ada_w, ada_b.reshape(depth, 1, n))


def _row_specs(b, p, d, n_lat_tiles, tm):
    x_spec = pl.BlockSpec((1, tm, d), lambda i, j: (i, j, 0))
    mod_spec = pl.BlockSpec((1, 6, d), lambda i, j: (jnp.where(j < n_lat_tiles, i, b), 0, 0))
    return x_spec, mod_spec


def _proj_rope_kernel(x_ref, mod_ref, w_ref, cos_ref, sin_ref, o_ref, *vt_ref, n_rope, n_q, q_scale):
    h = _modulated(x_ref, mod_ref, 0).astype(_BF16)
    acc = jnp.dot(h, w_ref[...], preferred_element_type=_F32)
    cos, sin = cos_ref[...], sin_ref[...]
    n = o_ref.shape[2]
    for j in range(n // LANES):
        a = acc[:, j * LANES:(j + 1) * LANES]
        if j * LANES < n_rope:
            a = _rope(a, cos, sin, 16)
        if j * LANES < n_q and q_scale != 1.0:
            a = a * q_scale
        o_ref[0, :, j * LANES:(j + 1) * LANES] = a.astype(_BF16)
    if vt_ref:
        vt_ref[0][0] = acc[:, n:].T.astype(_BF16)


def _proj_rope(x, mods, w, cos, sin, n_rope, n_t, n_lat_tiles, n_q=0, q_scale=1.0):
    b, p, d = x.shape
    n_all = w.shape[1]
    n = n_all - n_t
    tm = ROW_TILE
    x_spec, mod_spec = _row_specs(b, p, d, n_lat_tiles, tm)
    out_specs = [pl.BlockSpec((1, tm, n), lambda i, j: (i, j, 0))]
    out_shape = [jax.ShapeDtypeStruct((b, p, n), _BF16)]
    if n_t:
        out_specs.append(pl.BlockSpec((1, n_t, tm), lambda i, j: (i, 0, j)))
        out_shape.append(jax.ShapeDtypeStruct((b, n_t, p), _BF16))
    return pl.pallas_call(
        functools.partial(_proj_rope_kernel, n_rope=n_rope, n_q=n_q, q_scale=q_scale),
        grid=(b, p // tm),
        in_specs=[x_spec, mod_spec,
                  pl.BlockSpec((d, n_all), lambda i, j: (0, 0)),
                  pl.BlockSpec((tm, LANES), lambda i, j: (j, 0)),
                  pl.BlockSpec((tm, LANES), lambda i, j: (j, 0))],
        out_specs=out_specs,
        out_shape=out_shape,
        compiler_params=_cparams("parallel", "parallel"),
        name="proj_rope",
    )(x, mods, w, cos, sin)


def _rms(x, g):
    return x * lax.rsqrt(jnp.mean(x * x, axis=-1, keepdims=True) + RMS_EPS) * g


def _proj_mla_kernel(x_ref, mod_ref, win_ref, qn_ref, kvn_ref, wuq_ref, wuk_ref, wv_ref,
                     cos_ref, sin_ref, q_ref, k_ref, vt_ref, *, scale):
    h = _modulated(x_ref, mod_ref, 0).astype(_BF16)
    c = jnp.dot(h, win_ref[...], preferred_element_type=_F32)
    cq = _rms(c[:, :B_Q_RANK], qn_ref[...]).astype(_BF16)
    ckv = _rms(c[:, B_Q_RANK:B_Q_RANK + B_KV_RANK], kvn_ref[...]).astype(_BF16)
    cos, sin = cos_ref[...], sin_ref[...]
    k_rope = _rope(c[:, B_Q_RANK + B_KV_RANK:], cos, sin, 8)
    q = jnp.dot(cq, wuq_ref[...], preferred_element_type=_F32)
    k_nope = jnp.dot(ckv, wuk_ref[...], preferred_element_type=_F32)
    for hd in range(B_HEADS):
        sl = slice(hd * LANES, (hd + 1) * LANES)
        q_ref[0, :, sl] = (_rope(q[:, sl], cos, sin, 8) * scale).astype(_BF16)
        k_ref[0, :, sl] = (k_nope[:, sl] + k_rope).astype(_BF16)
    vt_ref[0] = jnp.dot(ckv, wv_ref[...], preferred_element_type=_F32).T.astype(_BF16)


def _proj_mla(x, mods, w_in, q_norm, kv_norm, w_uq, w_uk, w_v, cos, sin, n_lat_tiles):
    b, p, d = x.shape
    tm = ROW_TILE
    x_spec, mod_spec = _row_specs(b, p, d, n_lat_tiles, tm)
    full = lambda a: pl.BlockSpec(a.shape, lambda i, j: (0,) * a.ndim)
    nq, nv = w_uq.shape[1], w_v.shape[1]
    tile = lambda n: pl.BlockSpec((1, tm, n), lambda i, j: (i, j, 0))
    return pl.pallas_call(
        functools.partial(_proj_mla_kernel, scale=(B_NOPE + B_ROPE) ** -0.5 * LOG2E),
        grid=(b, p // tm),
        in_specs=[x_spec, mod_spec, full(w_in), full(q_norm), full(kv_norm), full(w_uq), full(w_uk),
                  full(w_v),
                  pl.BlockSpec((tm, LANES), lambda i, j: (j, 0)),
                  pl.BlockSpec((tm, LANES), lambda i, j: (j, 0))],
        out_specs=[tile(nq), tile(nq), pl.BlockSpec((1, nv, tm), lambda i, j: (i, 0, j))],
        out_shape=[jax.ShapeDtypeStruct((b, p, nq), _BF16), jax.ShapeDtypeStruct((b, p, nq), _BF16),
                   jax.ShapeDtypeStruct((b, nv, p), _BF16)],
        compiler_params=_cparams("parallel", "parallel"),
        name="proj_mla",
    )(x, mods, w_in, q_norm, kv_norm, w_uq, w_uk, w_v, cos, sin)


def _attn_win_kernel(sink_ref, q_ref, k0_ref, k1_ref, k2_ref, kc_ref, vt0_ref, vt1_ref, vt2_ref, vtc_ref,
                     o_ref, *, n_lat_tiles):
    j = pl.program_id(1)
    w = WIN_BLOCK
    n_ctx = kc_ref.shape[1]
    n_keys = 3 * w + n_ctx
    nq = A_GROUP * w
    row = lax.broadcasted_iota(jnp.int32, (n_keys, nq), 0)
    q_pos = j * w + lax.broadcasted_iota(jnp.int32, (n_keys, nq), 1) % w
    k_pos = (j - 1) * w + row
    n_latent_keys = jnp.where(j < n_lat_tiles, n_lat_tiles * w, 0)
    valid = (row >= 3 * w) | ((jnp.abs(q_pos - k_pos) <= A_WINDOW) & (k_pos >= 0) & (k_pos < n_latent_keys))
    low = _lane_iota((w, LANES)) < A_HEAD_DIM

    def scores(kvh):
        cs = slice(kvh * LANES, (kvh + 1) * LANES)
        kk = jnp.concatenate([k0_ref[0, :, cs], k1_ref[0, :, cs], k2_ref[0, :, cs], kc_ref[0, :, cs]], axis=0)
        qa = q_ref[0, :, 2 * kvh * LANES:(2 * kvh + 1) * LANES]
        qb = q_ref[0, :, (2 * kvh + 1) * LANES:(2 * kvh + 2) * LANES]
        zero = jnp.zeros_like(qa)
        q4 = jnp.concatenate([jnp.where(low, qa, zero), jnp.where(low, zero, qa),
                              jnp.where(low, qb, zero), jnp.where(low, zero, qb)], axis=0)
        return lax.dot_general(kk, q4, _NT, preferred_element_type=_F32)

    outs = []
    s_next = scores(0)
    for kvh in range(A_KV_HEADS):
        s = s_next
        if kvh + 1 < A_KV_HEADS:
            s_next = scores(kvh + 1)
        rs = slice(kvh * A_HEAD_DIM, (kvh + 1) * A_HEAD_DIM)
        vt = jnp.concatenate([vt0_ref[0, rs, :], vt1_ref[0, rs, :], vt2_ref[0, rs, :], vtc_ref[0, rs, :]], axis=1)
        sink = jnp.concatenate([jnp.full((1, w), sink_ref[kvh * A_GROUP + g] * LOG2E, _F32)
                                for g in range(A_GROUP)], axis=1)
        s = jnp.where(valid, s, -jnp.inf)
        m = jnp.maximum(jnp.max(s, axis=0, keepdims=True), sink)
        e = jnp.exp2(s - m)
        denom = jnp.sum(e, axis=0, keepdims=True) + jnp.exp2(sink - m)
        o = jnp.dot(vt, e.astype(_BF16), preferred_element_type=_F32) / denom
        outs += [o[:, g * w:(g + 1) * w] for g in range(A_GROUP)]
    o_ref[0] = jnp.concatenate(outs, axis=0).T.astype(_BF16)


def _attn_win(qk, vt, sink, s_len, l_ctx):
    b, p, _ = qk.shape
    w = WIN_BLOCK
    n_lat = s_len // w
    nq = A_HEADS * A_HEAD_DIM
    kw = A_KV_HEADS * LANES
    nv = A_KV_HEADS * A_HEAD_DIM
    ctx_blk = s_len // l_ctx

    def nb(off):
        return lambda i, j: (i, jnp.clip(j + off, 0, n_lat - 1), nq // kw)

    def nbt(off):
        return lambda i, j: (i, 0, jnp.clip(j + off, 0, n_lat - 1))

    return pl.pallas_call(
        functools.partial(_attn_win_kernel, n_lat_tiles=n_lat),
        grid=(b, p // w),
        in_specs=[pl.BlockSpec(memory_space=pltpu.SMEM),
                  pl.BlockSpec((1, w, nq), lambda i, j: (i, j, 0)),
                  pl.BlockSpec((1, w, kw), nb(-1)), pl.BlockSpec((1, w, kw), nb(0)),
                  pl.BlockSpec((1, w, kw), nb(1)),
                  pl.BlockSpec((1, l_ctx, kw), lambda i, j: (i, ctx_blk, nq // kw)),
                  pl.BlockSpec((1, nv, w), nbt(-1)), pl.BlockSpec((1, nv, w), nbt(0)),
                  pl.BlockSpec((1, nv, w), nbt(1)),
                  pl.BlockSpec((1, nv, l_ctx), lambda i, j: (i, 0, ctx_blk))],
        out_specs=pl.BlockSpec((1, w, nq), lambda i, j: (i, j, 0)),
        out_shape=jax.ShapeDtypeStruct((b, p, nq), _BF16),
        compiler_params=_cparams("parallel", "parallel"),
        name="attn_win",
    )(sink, qk, qk, qk, qk, qk, vt, vt, vt, vt)


def _key_chunks(s_len, l_ctx, latent_query):
    chunks = [(lo, KEY_CHUNK) for lo in range(0, s_len, KEY_CHUNK)] if latent_query else []
    return chunks + [(s_len, l_ctx)]


def _flash_keys_major(q, k_chunk, vt_chunk, chunks):
    def scores(c):
        return lax.dot_general(k_chunk(*chunks[c]), q, _NT, preferred_element_type=_F32)

    m = acc = l = None
    pending = None
    s_next = scores(0)

    def flush(acc):
        (lo, n), e, corr = pending
        pv = jnp.dot(vt_chunk(lo, n), e, preferred_element_type=_F32)
        return pv if acc is None else acc * corr + pv

    for c in range(len(chunks)):
        s = s_next
        if c + 1 < len(chunks):
            s_next = scores(c + 1)
        if pending is not None:
            acc = flush(acc)
        m_chunk = jnp.max(s, axis=0, keepdims=True)
        m_new = m_chunk if m is None else jnp.maximum(m, m_chunk)
        e = jnp.exp2(s - m_new)
        l_chunk = jnp.sum(e, axis=0, keepdims=True)
        corr = None if m is None else jnp.exp2(m - m_new)
        l = l_chunk if m is None else l * corr + l_chunk
        pending = (chunks[c], e.astype(_BF16), corr)
        m = m_new
    return flush(acc), l


def _per_step_tiles(tile, n_lat_steps):
    j = pl.program_id(2)

    @pl.when(j < n_lat_steps)
    def _():
        for t in range(Q_TILES_PER_STEP):
            tile(t, True)

    @pl.when(j >= n_lat_steps)
    def _():
        tile(0, False)


def _attn_mla_kernel(q_ref, k_ref, vt_ref, o_ref, *, n_lat_steps, s_len, l_ctx):
    tq = ATTN_Q_TILE

    def tile(t, latent_query):
        rows = slice(t * tq, (t + 1) * tq)
        q = q_ref[0, rows, :]
        first = _lane_iota(q.shape) < LANES
        zero = jnp.zeros_like(q)
        q2 = jnp.concatenate([jnp.where(first, q, zero), jnp.where(first, zero, q)], axis=0)
        acc, l = _flash_keys_major(q2, lambda lo, n: k_ref[0, lo:lo + n, :],
                                   lambda lo, n: vt_ref[0, :, lo:lo + n],
                                   _key_chunks(s_len, l_ctx, latent_query))
        o2 = acc / l
        o = jnp.concatenate([o2[:B_V, :tq], o2[B_V:, tq:]], axis=0)
        o_ref[0, rows, :] = o.T.astype(_BF16)

    _per_step_tiles(tile, n_lat_steps)


def _attn_mla(q, k, vt, s_len):
    b, p, _ = q.shape
    tq = ATTN_Q_TILE * Q_TILES_PER_STEP
    pairs = B_HEADS // 2
    return pl.pallas_call(
        functools.partial(_attn_mla_kernel, n_lat_steps=s_len // tq, s_len=s_len, l_ctx=p - s_len),
        grid=(b, pairs, pl.cdiv(p, tq)),
        in_specs=[pl.BlockSpec((1, tq, 2 * LANES), lambda i, c, j: (i, j, c)),
                  pl.BlockSpec((1, p, 2 * LANES), lambda i, c, j: (i, 0, c)),
                  pl.BlockSpec((1, 2 * B_V, p), lambda i, c, j: (i, c, 0))],
        out_specs=pl.BlockSpec((1, tq, 2 * B_V), lambda i, c, j: (i, j, c)),
        out_shape=jax.ShapeDtypeStruct((b, p, B_HEADS * B_V), _BF16),
        compiler_params=_cparams("parallel", "parallel", "parallel"),
        name="attn_mla",
    )(q, k, vt)


def _attn_diff_kernel(lam_ref, subln_ref, q_ref, k_ref, vt_ref, o_ref, *, n_lat_steps, s_len, l_ctx,
                      lambda_init):
    tq = ATTN_Q_TILE
    lp = lam_ref[...]
    lam = (jnp.exp(jnp.sum(lp[0:1] * lp[1:2], axis=-1, keepdims=True))
           - jnp.exp(jnp.sum(lp[2:3] * lp[3:4], axis=-1, keepdims=True)) + lambda_init)

    def tile(t, latent_query):
        rows = slice(t * tq, (t + 1) * tq)
        q = q_ref[0, rows, :]
        low = _lane_iota(q.shape) < C_HEAD_DIM
        zero = jnp.zeros_like(q)
        q2 = jnp.concatenate([jnp.where(low, q, zero), jnp.where(low, zero, q)], axis=0)
        acc, l = _flash_keys_major(q2, lambda lo, n: k_ref[0, lo:lo + n, :],
                                   lambda lo, n: vt_ref[0, :, lo:lo + n],
                                   _key_chunks(s_len, l_ctx, latent_query))
        o2 = acc / l
        o = o2[:, :tq] - lam * o2[:, tq:]
        o = o * lax.rsqrt(jnp.mean(o * o, axis=0, keepdims=True) + RMS_EPS) * subln_ref[...]
        o_ref[0, rows, :] = (o * (1.0 - lambda_init)).T.astype(_BF16)

    _per_step_tiles(tile, n_lat_steps)


def _attn_diff(qk, vt, lam_params, subln, s_len, lambda_init):
    b, p, _ = qk.shape
    tq = ATTN_Q_TILE * Q_TILES_PER_STEP
    return pl.pallas_call(
        functools.partial(_attn_diff_kernel, n_lat_steps=s_len // tq, s_len=s_len, l_ctx=p - s_len,
                          lambda_init=lambda_init),
        grid=(b, C_HEADS, pl.cdiv(p, tq)),
        in_specs=[pl.BlockSpec(lam_params.shape, lambda i, h, j: (0, 0)),
                  pl.BlockSpec((C_V_DIM, 1), lambda i, h, j: (0, 0)),
                  pl.BlockSpec((1, tq, LANES), lambda i, h, j: (i, j, h)),
                  pl.BlockSpec((1, p, LANES), lambda i, h, j: (i, 0, C_HEADS + h)),
                  pl.BlockSpec((1, C_V_DIM, p), lambda i, h, j: (i, h, 0))],
        out_specs=pl.BlockSpec((1, tq, LANES), lambda i, h, j: (i, j, h)),
        out_shape=jax.ShapeDtypeStruct((b, p, C_HEADS * C_V_DIM), _BF16),
        compiler_params=_cparams("parallel", "parallel", "parallel"),
        name="attn_diff",
    )(lam_params, subln.reshape(C_V_DIM, 1), qk, qk, vt)


def _layer_norm(z, g, b):
    mu = jnp.mean(z, axis=-1, keepdims=True)
    zc = z - mu
    var = jnp.mean(zc * zc, axis=-1, keepdims=True)
    return zc * lax.rsqrt(var + LN_EPS) * g + b


def _out_proj_kernel(o_ref, w_ref, x_ref, mod_ref, g_ref, b_ref, wr_ref, br_ref,
                     xo_ref, h_ref, lg_ref, *, alpha):
    y = jnp.dot(o_ref[0], w_ref[...], preferred_element_type=_F32)
    m = mod_ref[0]
    x = _layer_norm(alpha * x_ref[0] + m[2:3, :] * y, g_ref[...], b_ref[...])
    xo_ref[0] = x
    h = x * (1.0 + m[4:5, :]) + m[3:4, :]
    h_ref[0] = h
    tm = h.shape[0]
    h_hi = h.astype(_BF16)
    h_lo = (h - h_hi.astype(_F32)).astype(_BF16)
    t = jnp.dot(jnp.concatenate([h_hi, h_lo], axis=0), wr_ref[...], preferred_element_type=_F32)
    lg_ref[0] = t[:tm, :LANES] + t[tm:, :LANES] + t[:tm, LANES:] + br_ref[...]


def _out_proj(o, w_out, x, mods, ln_g, ln_b, w_router, b_router, alpha, n_lat_tiles):
    b, p, d = x.shape
    tm = ROW_TILE
    x_spec, mod_spec = _row_specs(b, p, d, n_lat_tiles, tm)
    full = lambda a: pl.BlockSpec(a.shape, lambda i, j: (0,) * a.ndim)
    n_o = o.shape[2]
    return pl.pallas_call(
        functools.partial(_out_proj_kernel, alpha=alpha),
        grid=(b, p // tm),
        in_specs=[pl.BlockSpec((1, tm, n_o), lambda i, j: (i, j, 0)), full(w_out), x_spec, mod_spec,
                  full(ln_g), full(ln_b), full(w_router), full(b_router)],
        out_specs=[x_spec, x_spec, pl.BlockSpec((1, tm, LANES), lambda i, j: (i, j, 0))],
        out_shape=[jax.ShapeDtypeStruct((b, p, d), _F32), jax.ShapeDtypeStruct((b, p, d), _F32),
                   jax.ShapeDtypeStruct((b, p, LANES), _F32)],
        compiler_params=_cparams("parallel", "parallel"),
        name="out_proj_ln",
    )(o, w_out, x, mods, ln_g, ln_b, w_router, b_router)


def _route_kernel(lg_ref, info_ref, btab_ref, sel_ref, cnt_ref, start_ref, run_ref, *, block_rows):
    sweep, i = pl.program_id(0), pl.program_id(1)
    tm = lg_ref.shape[0]
    rows = pl.ds(pl.multiple_of(i * tm, tm), tm)
    lane = _lane_iota((tm, LANES))

    @pl.when((sweep == 0) & (i == 0))
    def _():
        cnt_ref[...] = jnp.zeros_like(cnt_ref)

    @pl.when(sweep == 0)
    def _():
        x = lg_ref[...]
        xg = jnp.where(lane < N_GROUPS, x, -jnp.inf)
        g_max = jnp.max(xg, axis=-1, keepdims=True)
        g_sel = jnp.min(jnp.where(xg == g_max, lane, LANES), axis=-1, keepdims=True)
        p_g = 1.0 / jnp.sum(jnp.exp(xg - g_max), axis=-1, keepdims=True)
        lo = N_GROUPS + g_sel * EXPERTS_PER_GROUP
        xe = jnp.where((lane >= lo) & (lane < lo + EXPERTS_PER_GROUP), x, -jnp.inf)
        v1 = jnp.max(xe, axis=-1, keepdims=True)
        i1 = jnp.min(jnp.where(xe == v1, lane, LANES), axis=-1, keepdims=True)
        xe = jnp.where(lane == i1, -jnp.inf, xe)
        v2 = jnp.max(xe, axis=-1, keepdims=True)
        i2 = jnp.min(jnp.where(xe == v2, lane, LANES), axis=-1, keepdims=True)
        t = jnp.exp(v2 - v1)
        w1 = p_g / (1.0 + t)
        sel_ref[rows, :] = jnp.where(lane == 0, i1.astype(_F32), jnp.where(lane == 1, i2.astype(_F32),
                                     jnp.where(lane == 2, w1, jnp.where(lane == 3, w1 * t, 0.0))))
        onehot = ((lane == i1) | (lane == i2)).astype(_F32)
        cnt_ref[...] += jnp.sum(onehot, axis=0, keepdims=True)

    @pl.when((sweep == 1) & (i == 0))
    def _():
        cnt = cnt_ref[...]
        padded = jnp.floor((cnt + (block_rows - 1)) * (1.0 / block_rows)) * block_rows
        r = lax.broadcasted_iota(jnp.int32, (LANES, LANES), 0)
        c = lax.broadcasted_iota(jnp.int32, (LANES, LANES), 1)
        before = (r < c).astype(_F32)
        start = jnp.dot(jnp.broadcast_to(padded, (8, LANES)), before, preferred_element_type=_F32,
                        precision=lax.Precision.HIGHEST)[0:1]
        start_ref[...] = start
        run_ref[...] = jnp.zeros_like(run_ref)
        nb = btab_ref.shape[0]
        row0 = (lax.broadcasted_iota(jnp.int32, (nb, LANES), 0) * block_rows).astype(_F32)
        is_e = (_lane_iota((nb, LANES)) >= N_GROUPS) & (_lane_iota((nb, LANES)) < N_GROUPS + N_EXPERTS)
        end = start + padded
        owner = jnp.sum(jnp.where(is_e & (end <= row0), 1.0, 0.0), axis=-1, keepdims=True)
        inside = is_e & (start <= row0) & (row0 < end)
        real = jnp.sum(jnp.where(inside, jnp.clip(cnt - (row0 - start), 0.0, block_rows), 0.0),
                       axis=-1, keepdims=True)
        bl = _lane_iota((nb, LANES))
        btab_ref[...] = jnp.where(bl == 0, jnp.minimum(owner, N_EXPERTS - 1.0),
                                  jnp.where(bl == 1, real, 0.0)).astype(jnp.int32)

    @pl.when(sweep == 1)
    def _():
        sel = sel_ref[rows, :]
        hit1 = lane == sel[:, 0:1].astype(jnp.int32)
        hit2 = lane == sel[:, 1:2].astype(jnp.int32)
        onehot = (hit1 | hit2).astype(_F32)
        row = lax.broadcasted_iota(jnp.int32, (tm, tm), 0)
        col = lax.broadcasted_iota(jnp.int32, (tm, tm), 1)
        earlier = (col < row).astype(_BF16)
        rank = jnp.dot(earlier, onehot.astype(_BF16), preferred_element_type=_F32) + run_ref[...]
        pos = rank + start_ref[...]
        d1 = jnp.sum(jnp.where(hit1, pos, 0.0), axis=-1, keepdims=True)
        d2 = jnp.sum(jnp.where(hit2, pos, 0.0), axis=-1, keepdims=True)
        run_ref[...] += jnp.sum(onehot, axis=0, keepdims=True)
        info_ref[...] = jnp.where(lane == 0, d1, jnp.where(lane == 1, d2, sel))


def _route(logits, n_blocks):
    t = logits.shape[0]
    tm = ROUTE_TILE
    nb_pad = -(-n_blocks // 8) * 8
    return pl.pallas_call(
        functools.partial(_route_kernel, block_rows=MOE_ROWS),
        grid=(2, t // tm),
        in_specs=[pl.BlockSpec((tm, LANES), lambda s, i: (i * (1 - s), 0))],
        out_specs=[pl.BlockSpec((tm, LANES), lambda s, i: (i * s, 0)),
                   pl.BlockSpec((nb_pad, LANES), lambda s, i: (0, 0))],
        out_shape=[jax.ShapeDtypeStruct((t, LANES), _F32), jax.ShapeDtypeStruct((nb_pad, LANES), jnp.int32)],
        scratch_shapes=[pltpu.VMEM((t, LANES), _F32)] + [pltpu.VMEM((1, LANES), _F32)] * 3,
        compiler_params=_cparams("arbitrary", "arbitrary"),
        name="moe_route",
    )(logits)


def _row_copy_wait(src_rows, dst_rows, sem):
    pltpu.make_async_copy(src_rows, dst_rows, sem).wait()


def _dispatch_kernel(dest_ref, btab_ref, h_ref, xb_ref, zeros, sem, zero_sem, *, n_blocks):
    tm = h_ref.shape[0]
    bm = MOE_ROWS
    base = pl.program_id(0) * tm

    @pl.when(pl.program_id(0) == 0)
    def _():
        zeros[...] = jnp.zeros_like(zeros)

        def tail_rows(blk):
            real = btab_ref[blk, 1]
            return jnp.where(real > 0, bm - real, 0)

        def tail_copy(row):
            return pltpu.make_async_copy(zeros.at[pl.ds(0, 1)], xb_ref.at[pl.ds(row, 1)], zero_sem)

        def block_copy(blk):
            return pltpu.make_async_copy(zeros, xb_ref.at[pl.ds(blk * bm, bm)], zero_sem)

        for blk in range(n_blocks):
            first = blk * bm + btab_ref[blk, 1]
            lax.fori_loop(0, tail_rows(blk), lambda r, c: (tail_copy(first + r).start(), c)[1], 0)
            pl.when(btab_ref[blk, 1] == 0)(lambda: block_copy(blk).start())
        for blk in range(n_blocks):
            lax.fori_loop(0, tail_rows(blk), lambda r, c: (tail_copy(0).wait(), c)[1], 0)
            pl.when(btab_ref[blk, 1] == 0)(lambda: block_copy(blk).wait())

    for r in range(tm):
        for k in range(TOP_K):
            d = dest_ref[(base + r) * TOP_K + k]
            pltpu.make_async_copy(h_ref.at[pl.ds(r, 1)], xb_ref.at[pl.ds(d, 1)], sem).start()
    for k in range(TOP_K):
        _row_copy_wait(h_ref, xb_ref.at[pl.ds(0, tm)], sem)


def _dispatch(dest, btab, h, n_blocks):
    t, d = h.shape
    tm = ROW_TILE
    grid_spec = pltpu.PrefetchScalarGridSpec(
        num_scalar_prefetch=2,
        grid=(t // tm,),
        in_specs=[pl.BlockSpec((tm, d), lambda i, dest, bt: (i, 0))],
        out_specs=pl.BlockSpec(memory_space=pl.ANY),
        scratch_shapes=[pltpu.VMEM((MOE_ROWS, d), h.dtype), pltpu.SemaphoreType.DMA(()),
                        pltpu.SemaphoreType.DMA(())],
    )
    return pl.pallas_call(
        functools.partial(_dispatch_kernel, n_blocks=n_blocks),
        grid_spec=grid_spec,
        out_shape=jax.ShapeDtypeStruct((n_blocks * MOE_ROWS, d), h.dtype),
        compiler_params=_cparams("arbitrary", disable_bounds_checks=True),
        name="moe_dispatch",
    )(dest, btab, h)


def _ffn_kernel(btab_ref, x_ref, wg_ref, wu_ref, wd_ref, y_ref, wg_bf, wu_bf, wd_bf):
    i = pl.program_id(0)
    expert, real = btab_ref[i, 0], btab_ref[i, 1]
    prev = btab_ref[jnp.maximum(i - 1, 0), 0]

    @pl.when((i == 0) | (expert != prev))
    def _():
        wg_bf[...] = wg_ref[0, 0].astype(_BF16)
        wu_bf[...] = wu_ref[0, 0].astype(_BF16)
        wd_bf[...] = wd_ref[0, 0].astype(_BF16)

    @pl.when(real > 0)
    def _():
        x = x_ref[...].astype(_BF16)
        g = jnp.dot(x, wg_bf[...], preferred_element_type=_F32)
        u = jnp.dot(x, wu_bf[...], preferred_element_type=_F32)
        a = (g * jax.nn.sigmoid(g) * u).astype(_BF16)
        y_ref[...] = jnp.dot(a, wd_bf[...], preferred_element_type=_F32)

    @pl.when(real <= 0)
    def _():
        y_ref[...] = jnp.zeros_like(y_ref)


def _expert_ffn(btab, xb, w_gate, w_up, w_down, layer):
    rows, d = xb.shape
    de = w_gate.shape[3]
    bm = MOE_ROWS
    w_map = lambda i, bt: (layer, bt[i, 0], 0, 0)
    grid_spec = pltpu.PrefetchScalarGridSpec(
        num_scalar_prefetch=1,
        grid=(rows // bm,),
        in_specs=[pl.BlockSpec((bm, d), lambda i, bt: (i, 0)),
                  pl.BlockSpec((1, 1, d, de), w_map),
                  pl.BlockSpec((1, 1, d, de), w_map),
                  pl.BlockSpec((1, 1, de, d), w_map)],
        out_specs=pl.BlockSpec((bm, d), lambda i, bt: (i, 0)),
        scratch_shapes=[pltpu.VMEM((d, de), _BF16), pltpu.VMEM((d, de), _BF16), pltpu.VMEM((de, d), _BF16)],
    )
    return pl.pallas_call(
        _ffn_kernel,
        grid_spec=grid_spec,
        out_shape=jax.ShapeDtypeStruct((rows, d), _F32),
        compiler_params=_cparams("arbitrary"),
        name="expert_ffn",
    )(btab, xb, w_gate, w_up, w_down)


def _combine_kernel(dest_ref, info_ref, x_ref, mod_ref, g_ref, b_ref, yb_ref, xo_ref, buf, sem, *, alpha,
                    tiles_per_batch):
    tm = x_ref.shape[1]
    i, j = pl.program_id(0), pl.program_id(1)
    nj = pl.num_programs(1)
    step = i * nj + j
    tile = i * tiles_per_batch + j

    def issue(at_tile, slot):
        base = at_tile * tm

        for r in range(tm):
            for k in range(TOP_K):
                d = dest_ref[(base + r) * TOP_K + k]
                pltpu.make_async_copy(yb_ref.at[pl.ds(d, 1)], buf.at[slot, k, pl.ds(r, 1)], sem.at[slot]).start()

    @pl.when(step == 0)
    def _():
        issue(tile, 0)

    @pl.when(step + 1 < pl.num_programs(0) * nj)
    def _():
        issue(jnp.where(j + 1 < nj, tile + 1, (i + 1) * tiles_per_batch), (step + 1) % 2)

    slot = step % 2
    for k in range(TOP_K):
        _row_copy_wait(yb_ref.at[pl.ds(0, tm)], buf.at[slot, k], sem.at[slot])
    info = info_ref[0]
    y = info[:, 2:3] * buf[slot, 0] + info[:, 3:4] * buf[slot, 1]
    m = mod_ref[0]
    xo_ref[0] = _layer_norm(alpha * x_ref[0] + m[5:6, :] * y, g_ref[...], b_ref[...])


def _combine(dest, info, yb, x, mods, ln_g, ln_b, alpha, n_lat_tiles, latent_only):
    b, p, d = x.shape
    tm = ROW_TILE
    n_tiles = n_lat_tiles if latent_only else p // tm
    grid_spec = pltpu.PrefetchScalarGridSpec(
        num_scalar_prefetch=1,
        grid=(b, n_tiles),
        in_specs=[pl.BlockSpec((1, tm, LANES), lambda i, j, dest: (i, j, 0)),
                  pl.BlockSpec((1, tm, d), lambda i, j, dest: (i, j, 0)),
                  pl.BlockSpec((1, 6, d), lambda i, j, dest: (jnp.where(j < n_lat_tiles, i, b), 0, 0)),
                  pl.BlockSpec(ln_g.shape, lambda i, j, dest: (0, 0)),
                  pl.BlockSpec(ln_b.shape, lambda i, j, dest: (0, 0)),
                  pl.BlockSpec(memory_space=pl.ANY)],
        out_specs=pl.BlockSpec((1, tm, d), lambda i, j, dest: (i, j, 0)),
        scratch_shapes=[pltpu.VMEM((2, TOP_K, tm, d), _F32), pltpu.SemaphoreType.DMA((2,))],
    )
    return pl.pallas_call(
        functools.partial(_combine_kernel, alpha=alpha, tiles_per_batch=p // tm),
        grid_spec=grid_spec,
        out_shape=jax.ShapeDtypeStruct((b, n_tiles * tm, d), _F32),
        compiler_params=_cparams("arbitrary", "arbitrary", disable_bounds_checks=True),
        name="moe_combine_ln",
    )(dest, info.reshape(b, p, LANES), x, mods, ln_g, ln_b, yb)


def _moe_layer(h, logits, x, mods, ln_g, ln_b, w_gate, w_up, w_down, layer, alpha, n_lat_tiles, latent_only):
    t, d = h.shape
    n_blocks = (t * TOP_K) // MOE_ROWS + N_EXPERTS
    info, btab = _route(logits, n_blocks)
    dest = info[:, :TOP_K].astype(jnp.int32).reshape(t * TOP_K)
    xb = _dispatch(dest, btab, h, n_blocks)
    yb = _expert_ffn(btab, xb, w_gate, w_up, w_down, layer)
    return _combine(dest, info, yb, x, mods, ln_g, ln_b, alpha, n_lat_tiles, latent_only)


def _rope_tables(s_len, l_ctx, rot_dim, lane_lo, period):
    rows = s_len // GRID_W
    row = jnp.repeat(jnp.arange(rows, dtype=_F32), GRID_W)
    col = jnp.tile(jnp.arange(GRID_W, dtype=_F32), rows)
    axis_dim = rot_dim // 2
    half = axis_dim // 2
    inv_freq = ROPE_THETA ** (-jnp.arange(0, axis_dim, 2, dtype=_F32) / axis_dim)
    lane = jnp.arange(LANES)
    rel = (lane % period) - lane_lo
    active = (rel >= 0) & (rel < rot_dim)
    rel = jnp.clip(rel, 0, rot_dim - 1)
    use_col = rel >= axis_dim
    f = inv_freq[(rel % axis_dim) % half]
    ang = jnp.where(use_col[None, :], col[:, None], row[:, None]) * f[None, :]
    sign = jnp.where((rel % axis_dim) < half, -1.0, 1.0)
    cos = jnp.where(active[None, :], jnp.cos(ang), 1.0)
    sin = jnp.where(active[None, :], jnp.sin(ang) * sign[None, :], 0.0)
    ident = jnp.ones((l_ctx, LANES), _F32)
    return (jnp.concatenate([cos, ident], axis=0), jnp.concatenate([sin, 0.0 * ident], axis=0))


def _win_weights(w_in):
    d = w_in.shape[0]
    nq, nkv = A_HEADS * A_HEAD_DIM, A_KV_HEADS * A_HEAD_DIM
    k = w_in[:, nq:nq + nkv].reshape(d, A_KV_HEADS, 1, A_HEAD_DIM)
    k_dup = jnp.concatenate([k, k], axis=2).reshape(d, 2 * nkv)
    return jnp.concatenate([w_in[:, :nq], k_dup, w_in[:, nq + nkv:]], axis=1).astype(_BF16)


def _mla_weights(w_in, w_uq, w_ukv):
    d = w_in.shape[0]
    pad = LANES - B_NOPE - B_ROPE
    kr = jnp.concatenate([jnp.zeros((d, B_NOPE), _F32), w_in[:, B_Q_RANK + B_KV_RANK:],
                          jnp.zeros((d, pad), _F32)], axis=1)
    w_in_p = jnp.concatenate([w_in[:, :B_Q_RANK + B_KV_RANK], kr], axis=1).astype(_BF16)
    uq = w_uq.reshape(B_Q_RANK, B_HEADS, B_NOPE + B_ROPE)
    uq = jnp.pad(uq, ((0, 0), (0, 0), (0, pad))).reshape(B_Q_RANK, B_HEADS * LANES).astype(_BF16)
    ukv = w_ukv.reshape(B_KV_RANK, B_HEADS, B_NOPE + B_V)
    uk = jnp.pad(ukv[:, :, :B_NOPE], ((0, 0), (0, 0), (0, LANES - B_NOPE)))
    uk = uk.reshape(B_KV_RANK, B_HEADS * LANES).astype(_BF16)
    uv = ukv[:, :, B_NOPE:].reshape(B_KV_RANK, B_HEADS * B_V).astype(_BF16)
    return w_in_p, uq, uk, uv


def kernel(x, c, ctx, c_ctx, ada_w, ada_b, ln_mix_g, ln_mix_b, ln_ffn_g, ln_ffn_b, win_w_in, win_w_out, win_sink, mla_w_in, mla_q_norm, mla_kv_norm, mla_w_uq, mla_w_ukv, mla_w_out, dif_w_in, dif_lambda, dif_subln, dif_w_out, moe_w_grp, moe_b_grp, moe_w_rt, moe_b_rt, moe_w_gate, moe_w_up, moe_w_down):
    b, s_len, d = x.shape
    l_ctx = ctx.shape[1]
    depth = ada_w.shape[0]
    p = s_len + l_ctx
    assert s_len % ROW_TILE == 0 and l_ctx % ROW_TILE == 0 and s_len % GRID_W == 0
    assert s_len % (ATTN_Q_TILE * Q_TILES_PER_STEP) == 0 and l_ctx == ATTN_Q_TILE and s_len % l_ctx == 0
    assert s_len % KEY_CHUNK == 0 and MOE_ROWS & (MOE_ROWS - 1) == 0 and (b * p) % ROUTE_TILE == 0
    alpha = (2.0 * depth) ** 0.25
    n_lat_tiles = s_len // ROW_TILE

    xs = jnp.concatenate([x, ctx], axis=1)
    cs = jnp.concatenate([c, c_ctx[None, :]], axis=0)
    mods = _ada_mods(cs, ada_w, ada_b).reshape(depth, b + 1, 6, d)
    cos64, sin64 = _rope_tables(s_len, l_ctx, A_HEAD_DIM, 0, A_HEAD_DIM)
    cos32, sin32 = _rope_tables(s_len, l_ctx, B_ROPE, B_NOPE, LANES)
    row = lambda v: v.reshape(1, -1)

    for i in range(depth):
        kind, slot = i % N_MIXERS, i // N_MIXERS
        if kind == 0:
            qk, vt = _proj_rope(xs, mods[i], _win_weights(win_w_in[slot]), cos64, sin64,
                                (A_HEADS + 2 * A_KV_HEADS) * A_HEAD_DIM, A_KV_HEADS * A_HEAD_DIM, n_lat_tiles,
                                n_q=A_HEADS * A_HEAD_DIM, q_scale=A_HEAD_DIM ** -0.5 * LOG2E)
            o = _attn_win(qk, vt, win_sink[slot], s_len, l_ctx)
            w_out = win_w_out[slot]
        elif kind == 1:
            w_in_p, uq, uk, uv = _mla_weights(mla_w_in[slot], mla_w_uq[slot], mla_w_ukv[slot])
            q, k, vt = _proj_mla(xs, mods[i], w_in_p, row(mla_q_norm[slot]), row(mla_kv_norm[slot]),
                                uq, uk, uv, cos32, sin32, n_lat_tiles)
            o = _attn_mla(q, k, vt, s_len)
            w_out = mla_w_out[slot]
        else:
            lambda_init = 0.8 - 0.6 * math.exp(-0.3 * i)
            qk, vt = _proj_rope(xs, mods[i], dif_w_in[slot].astype(_BF16), cos64, sin64,
                                2 * C_HEADS * 2 * C_HEAD_DIM, C_HEADS * C_V_DIM, n_lat_tiles,
                                n_q=C_HEADS * 2 * C_HEAD_DIM, q_scale=C_HEAD_DIM ** -0.5 * LOG2E)
            o = _attn_diff(qk, vt, dif_lambda[slot], dif_subln[slot], s_len, lambda_init)
            w_out = dif_w_out[slot]

        w_router = jnp.concatenate([moe_w_grp[i], moe_w_rt[i],
                                    jnp.zeros((d, LANES - N_GROUPS - N_EXPERTS), _F32)], axis=1)
        b_router = jnp.concatenate([moe_b_grp[i], moe_b_rt[i],
                                    jnp.zeros((LANES - N_GROUPS - N_EXPERTS,), _F32)])[None, :]
        w_hi = w_router.astype(_BF16)
        w_lo = (w_router - w_hi.astype(_F32)).astype(_BF16)
        xs, h, logits = _out_proj(o, w_out.astype(_BF16), xs, mods[i], row(ln_mix_g[i]), row(ln_mix_b[i]),
                                  jnp.concatenate([w_hi, w_lo], axis=1), b_router, alpha, n_lat_tiles)
        xs = _moe_layer(h.reshape(b * p, d), logits.reshape(b * p, LANES), xs, mods[i], row(ln_ffn_g[i]),
                        row(ln_ffn_b[i]), moe_w_gate, moe_w_up, moe_w_down, i, alpha, n_lat_tiles,
                        latent_only=i == depth - 1)
    return xs
```

```python
import functools
import math

import jax
import jax.numpy as jnp
from jax import lax
from jax.experimental import pallas as pl
from jax.experimental.pallas import tpu as pltpu

LANES = 128
GRID_W = 64
ROPE_THETA = 10000.0
LN_EPS = 1e-5
RMS_EPS = 1e-6
LOG2E = math.log2(math.e)

A_HEADS, A_KV_HEADS, A_HEAD_DIM, A_WINDOW = 16, 4, 64, 128
A_GROUP = A_HEADS // A_KV_HEADS
B_HEADS, B_NOPE, B_ROPE, B_V, B_Q_RANK, B_KV_RANK = 16, 64, 32, 64, 512, 256
C_HEADS, C_HEAD_DIM = 8, 64
C_V_DIM = 2 * C_HEAD_DIM
N_GROUPS, EXPERTS_PER_GROUP, TOP_K = 4, 8, 2
N_EXPERTS = N_GROUPS * EXPERTS_PER_GROUP
N_MIXERS = 3

ROW_TILE = 256
ATTN_Q_TILE = 256
Q_TILES_PER_STEP = 2
WIN_BLOCK = 128
KEY_CHUNK = 1024
MOE_ROWS = 256
ROUTE_TILE = 512
ROW_CHUNKS = 8
VMEM_LIMIT = 56 * 1024 * 1024

_F32 = jnp.float32
_BF16 = jnp.bfloat16
_NT = (((1,), (1,)), ((), ()))


def _cparams(*sem, **kw):
    return pltpu.CompilerParams(dimension_semantics=sem, vmem_limit_bytes=VMEM_LIMIT, **kw)


def _lane_iota(shape):
    return lax.broadcasted_iota(jnp.int32, shape, len(shape) - 1)


def _rope(a, cos, sin_signed, half):
    lane = _lane_iota(a.shape)
    swapped = jnp.where(lane % (2 * half) < half,
                        pltpu.roll(a, a.shape[-1] - half, axis=1),
                        pltpu.roll(a, half, axis=1))
    return a * cos + swapped * sin_signed


def _store_rows_chunked(ref, lead, val):
    n, d = val.shape
    c = d // LANES
    for k in range(c):
        ref[lead + (pl.ds(k, n, stride=c), slice(None))] = val[:, k * LANES:(k + 1) * LANES]


def _load_rows_chunked(ref, lead, n, c):
    return jnp.concatenate([ref[lead + (pl.ds(k, n, stride=c), slice(None))] for k in range(c)], axis=1)


def _modulated(x_ref, mod_ref, shift_row):
    m = mod_ref[0]
    return x_ref[0] * (1.0 + m[shift_row + 1:shift_row + 2, :]) + m[shift_row:shift_row + 1, :]


def _ada_kernel(c_ref, w_ref, b_ref, o_ref):
    c = c_ref[...]
    s = c * jax.nn.sigmoid(c)
    o_ref[0] = jnp.dot(s, w_ref[0], preferred_element_type=_F32,
                       precision=lax.Precision.HIGHEST) + b_ref[0]


def _ada_mods(cs, ada_w, ada_b):
    depth, d, n = ada_w.shape
    r = cs.shape[0]
    tn = 1024
    return pl.pallas_call(
        _ada_kernel,
        grid=(depth, n // tn),
        in_specs=[pl.BlockSpec((r, d), lambda i, j: (0, 0)),
                  pl.BlockSpec((1, d, tn), lambda i, j: (i, 0, j)),
                  pl.BlockSpec((1, 1, tn), lambda i, j: (i, 0, j))],
        out_specs=pl.BlockSpec((1, r, tn), lambda i, j: (i, 0, j)),
        out_shape=jax.ShapeDtypeStruct((depth, r, n), _F32),
        compiler_params=_cparams("parallel", "parallel"),
        name="ada_mods",
    )(cs, ada_w, ada_b.reshape(depth, 1, n))


def _row_specs(b, p, d, n_lat_tiles, tm):
    x_spec = pl.BlockSpec((1, tm, d), lambda i, j: (i, j, 0))
    mod_spec = pl.BlockSpec((1, 6, d), lambda i, j: (jnp.where(j < n_lat_tiles, i, b), 0, 0))
    return x_spec, mod_spec


def _proj_rope_kernel(x_ref, mod_ref, w_ref, cos_ref, sin_ref, o_ref, *vt_ref, n_rope, n_q, q_scale):
    h = _modulated(x_ref, mod_ref, 0).astype(_BF16)
    acc = jnp.dot(h, w_ref[...], preferred_element_type=_F32)
    cos, sin = cos_ref[...], sin_ref[...]
    n = o_ref.shape[2]
    for j in range(n // LANES):
        a = acc[:, j * LANES:(j + 1) * LANES]
        if j * LANES < n_rope:
            a = _rope(a, cos, sin, 16)
        if j * LANES < n_q and q_scale != 1.0:
            a = a * q_scale
        o_ref[0, :, j * LANES:(j + 1) * LANES] = a.astype(_BF16)
    if vt_ref:
        vt_ref[0][0] = acc[:, n:].T.astype(_BF16)


def _proj_rope(x, mods, w, cos, sin, n_rope, n_t, n_lat_tiles, n_q=0, q_scale=1.0):
    b, p, d = x.shape
    n_all = w.shape[1]
    n = n_all - n_t
    tm = ROW_TILE
    x_spec, mod_spec = _row_specs(b, p, d, n_lat_tiles, tm)
    out_specs = [pl.BlockSpec((1, tm, n), lambda i, j: (i, j, 0))]
    out_shape = [jax.ShapeDtypeStruct((b, p, n), _BF16)]
    if n_t:
        out_specs.append(pl.BlockSpec((1, n_t, tm), lambda i, j: (i, 0, j)))
        out_shape.append(jax.ShapeDtypeStruct((b, n_t, p), _BF16))
    return pl.pallas_call(
        functools.partial(_proj_rope_kernel, n_rope=n_rope, n_q=n_q, q_scale=q_scale),
        grid=(b, p // tm),
        in_specs=[x_spec, mod_spec,
                  pl.BlockSpec((d, n_all), lambda i, j: (0, 0)),
                  pl.BlockSpec((tm, LANES), lambda i, j: (j, 0)),
                  pl.BlockSpec((tm, LANES), lambda i, j: (j, 0))],
        out_specs=out_specs,
        out_shape=out_shape,
        compiler_params=_cparams("parallel", "parallel"),
        name="proj_rope",
    )(x, mods, w, cos, sin)


def _rms(x, g):
    return x * lax.rsqrt(jnp.mean(x * x, axis=-1, keepdims=True) + RMS_EPS) * g


def _proj_mla_kernel(x_ref, mod_ref, win_ref, qn_ref, kvn_ref, wuq_ref, wuk_ref, wv_ref,
                     cos_ref, sin_ref, q_ref, k_ref, vt_ref, *, scale):
    h = _modulated(x_ref, mod_ref, 0).astype(_BF16)
    c = jnp.dot(h, win_ref[...], preferred_element_type=_F32)
    cq = _rms(c[:, :B_Q_RANK], qn_ref[...]).astype(_BF16)
    ckv = _rms(c[:, B_Q_RANK:B_Q_RANK + B_KV_RANK], kvn_ref[...]).astype(_BF16)
    cos, sin = cos_ref[...], sin_ref[...]
    k_rope = _rope(c[:, B_Q_RANK + B_KV_RANK:], cos, sin, 8)
    q = jnp.dot(cq, wuq_ref[...], preferred_element_type=_F32)
    k_nope = jnp.dot(ckv, wuk_ref[...], preferred_element_type=_F32)
    for hd in range(B_HEADS):
        sl = slice(hd * LANES, (hd + 1) * LANES)
        q_ref[0, :, sl] = (_rope(q[:, sl], cos, sin, 8) * scale).astype(_BF16)
        k_ref[0, :, sl] = (k_nope[:, sl] + k_rope).astype(_BF16)
    vt_ref[0] = jnp.dot(ckv, wv_ref[...], preferred_element_type=_F32).T.astype(_BF16)


def _proj_mla(x, mods, w_in, q_norm, kv_norm, w_uq, w_uk, w_v, cos, sin, n_lat_tiles):
    b, p, d = x.shape
    tm = ROW_TILE
    x_spec, mod_spec = _row_specs(b, p, d, n_lat_tiles, tm)
    full = lambda a: pl.BlockSpec(a.shape, lambda i, j: (0,) * a.ndim)
    nq, nv = w_uq.shape[1], w_v.shape[1]
    tile = lambda n: pl.BlockSpec((1, tm, n), lambda i, j: (i, j, 0))
    return pl.pallas_call(
        functools.partial(_proj_mla_kernel, scale=(B_NOPE + B_ROPE) ** -0.5 * LOG2E),
        grid=(b, p // tm),
        in_specs=[x_spec, mod_spec, full(w_in), full(q_norm), full(kv_norm), full(w_uq), full(w_uk),
                  full(w_v),
                  pl.BlockSpec((tm, LANES), lambda i, j: (j, 0)),
                  pl.BlockSpec((tm, LANES), lambda i, j: (j, 0))],
        out_specs=[tile(nq), tile(nq), pl.BlockSpec((1, nv, tm), lambda i, j: (i, 0, j))],
        out_shape=[jax.ShapeDtypeStruct((b, p, nq), _BF16), jax.ShapeDtypeStruct((b, p, nq), _BF16),
                   jax.ShapeDtypeStruct((b, nv, p), _BF16)],
        compiler_params=_cparams("parallel", "parallel"),
        name="proj_mla",
    )(x, mods, w_in, q_norm, kv_norm, w_uq, w_uk, w_v, cos, sin)


def _attn_win_kernel(sink_ref, q_ref, k0_ref, k1_ref, k2_ref, kc_ref, vt0_ref, vt1_ref, vt2_ref, vtc_ref,
                     o_ref, *, n_lat_tiles):
    j = pl.program_id(1)
    w = WIN_BLOCK
    n_ctx = kc_ref.shape[1]
    n_keys = 3 * w + n_ctx
    nq = A_GROUP * w
    row = lax.broadcasted_iota(jnp.int32, (n_keys, nq), 0)
    q_pos = j * w + lax.broadcasted_iota(jnp.int32, (n_keys, nq), 1) % w
    k_pos = (j - 1) * w + row
    n_latent_keys = jnp.where(j < n_lat_tiles, n_lat_tiles * w, 0)
    valid = (row >= 3 * w) | ((jnp.abs(q_pos - k_pos) <= A_WINDOW) & (k_pos >= 0) & (k_pos < n_latent_keys))
    low = _lane_iota((w, LANES)) < A_HEAD_DIM

    def scores(kvh):
        cs = slice(kvh * LANES, (kvh + 1) * LANES)
        kk = jnp.concatenate([k0_ref[0, :, cs], k1_ref[0, :, cs], k2_ref[0, :, cs], kc_ref[0, :, cs]], axis=0)
        qa = q_ref[0, :, 2 * kvh * LANES:(2 * kvh + 1) * LANES]
        qb = q_ref[0, :, (2 * kvh + 1) * LANES:(2 * kvh + 2) * LANES]
        zero = jnp.zeros_like(qa)
        q4 = jnp.concatenate([jnp.where(low, qa, zero), jnp.where(low, zero, qa),
                              jnp.where(low, qb, zero), jnp.where(low, zero, qb)], axis=0)
        return lax.dot_general(kk, q4, _NT, preferred_element_type=_F32)

    outs = []
    s_next = scores(0)
    for kvh in range(A_KV_HEADS):
        s = s_next
        if kvh + 1 < A_KV_HEADS:
            s_next = scores(kvh + 1)
        rs = slice(kvh * A_HEAD_DIM, (kvh + 1) * A_HEAD_DIM)
        vt = jnp.concatenate([vt0_ref[0, rs, :], vt1_ref[0, rs, :], vt2_ref[0, rs, :], vtc_ref[0, rs, :]], axis=1)
        sink = jnp.concatenate([jnp.full((1, w), sink_ref[kvh * A_GROUP + g] * LOG2E, _F32)
                                for g in range(A_GROUP)], axis=1)
        s = jnp.where(valid, s, -jnp.inf)
        m = jnp.maximum(jnp.max(s, axis=0, keepdims=True), sink)
        e = jnp.exp2(s - m)
        denom = jnp.sum(e, axis=0, keepdims=True) + jnp.exp2(sink - m)
        o = jnp.dot(vt, e.astype(_BF16), preferred_element_type=_F32) / denom
        outs += [o[:, g * w:(g + 1) * w] for g in range(A_GROUP)]
    o_ref[0] = jnp.concatenate(outs, axis=0).T.astype(_BF16)


def _attn_win(qk, vt, sink, s_len, l_ctx):
    b, p, _ = qk.shape
    w = WIN_BLOCK
    n_lat = s_len // w
    nq = A_HEADS * A_HEAD_DIM
    kw = A_KV_HEADS * LANES
    nv = A_KV_HEADS * A_HEAD_DIM
    ctx_blk = s_len // l_ctx

    def nb(off):
        return lambda i, j: (i, jnp.clip(j + off, 0, n_lat - 1), nq // kw)

    def nbt(off):
        return lambda i, j: (i, 0, jnp.clip(j + off, 0, n_lat - 1))

    return pl.pallas_call(
        functools.partial(_attn_win_kernel, n_lat_tiles=n_lat),
        grid=(b, p // w),
        in_specs=[pl.BlockSpec(memory_space=pltpu.SMEM),
                  pl.BlockSpec((1, w, nq), lambda i, j: (i, j, 0)),
                  pl.BlockSpec((1, w, kw), nb(-1)), pl.BlockSpec((1, w, kw), nb(0)),
                  pl.BlockSpec((1, w, kw), nb(1)),
                  pl.BlockSpec((1, l_ctx, kw), lambda i, j: (i, ctx_blk, nq // kw)),
                  pl.BlockSpec((1, nv, w), nbt(-1)), pl.BlockSpec((1, nv, w), nbt(0)),
                  pl.BlockSpec((1, nv, w), nbt(1)),
                  pl.BlockSpec((1, nv, l_ctx), lambda i, j: (i, 0, ctx_blk))],
        out_specs=pl.BlockSpec((1, w, nq), lambda i, j: (i, j, 0)),
        out_shape=jax.ShapeDtypeStruct((b, p, nq), _BF16),
        compiler_params=_cparams("parallel", "parallel"),
        name="attn_win",
    )(sink, qk, qk, qk, qk, qk, vt, vt, vt, vt)


def _key_chunks(s_len, l_ctx, latent_query):
    chunks = [(lo, KEY_CHUNK) for lo in range(0, s_len, KEY_CHUNK)] if latent_query else []
    return chunks + [(s_len, l_ctx)]


def _flash_keys_major(q, k_chunk, vt_chunk, chunks):
    def scores(c):
        return lax.dot_general(k_chunk(*chunks[c]), q, _NT, preferred_element_type=_F32)

    m = acc = l = None
    pending = None
    s_next = scores(0)

    def flush(acc):
        (lo, n), e, corr = pending
        pv = jnp.dot(vt_chunk(lo, n), e, preferred_element_type=_F32)
        return pv if acc is None else acc * corr + pv

    for c in range(len(chunks)):
        s = s_next
        if c + 1 < len(chunks):
            s_next = scores(c + 1)
        if pending is not None:
            acc = flush(acc)
        m_chunk = jnp.max(s, axis=0, keepdims=True)
        m_new = m_chunk if m is None else jnp.maximum(m, m_chunk)
        e = jnp.exp2(s - m_new)
        l_chunk = jnp.sum(e, axis=0, keepdims=True)
        corr = None if m is None else jnp.exp2(m - m_new)
        l = l_chunk if m is None else l * corr + l_chunk
        pending = (chunks[c], e.astype(_BF16), corr)
        m = m_new
    return flush(acc), l


def _per_step_tiles(tile, n_lat_steps):
    j = pl.program_id(2)

    @pl.when(j < n_lat_steps)
    def _():
        for t in range(Q_TILES_PER_STEP):
            tile(t, True)

    @pl.when(j >= n_lat_steps)
    def _():
        tile(0, False)


def _attn_mla_kernel(q_ref, k_ref, vt_ref, o_ref, *, n_lat_steps, s_len, l_ctx):
    tq = ATTN_Q_TILE

    def tile(t, latent_query):
        rows = slice(t * tq, (t + 1) * tq)
        q = q_ref[0, rows, :]
        first = _lane_iota(q.shape) < LANES
        zero = jnp.zeros_like(q)
        q2 = jnp.concatenate([jnp.where(first, q, zero), jnp.where(first, zero, q)], axis=0)
        acc, l = _flash_keys_major(q2, lambda lo, n: k_ref[0, lo:lo + n, :],
                                   lambda lo, n: vt_ref[0, :, lo:lo + n],
                                   _key_chunks(s_len, l_ctx, latent_query))
        o2 = acc / l
        o = jnp.concatenate([o2[:B_V, :tq], o2[B_V:, tq:]], axis=0)
        o_ref[0, rows, :] = o.T.astype(_BF16)

    _per_step_tiles(tile, n_lat_steps)


def _attn_mla(q, k, vt, s_len):
    b, p, _ = q.shape
    tq = ATTN_Q_TILE * Q_TILES_PER_STEP
    pairs = B_HEADS // 2
    return pl.pallas_call(
        functools.partial(_attn_mla_kernel, n_lat_steps=s_len // tq, s_len=s_len, l_ctx=p - s_len),
        grid=(b, pairs, pl.cdiv(p, tq)),
        in_specs=[pl.BlockSpec((1, tq, 2 * LANES), lambda i, c, j: (i, j, c)),
                  pl.BlockSpec((1, p, 2 * LANES), lambda i, c, j: (i, 0, c)),
                  pl.BlockSpec((1, 2 * B_V, p), lambda i, c, j: (i, c, 0))],
        out_specs=pl.BlockSpec((1, tq, 2 * B_V), lambda i, c, j: (i, j, c)),
        out_shape=jax.ShapeDtypeStruct((b, p, B_HEADS * B_V), _BF16),
        compiler_params=_cparams("parallel", "parallel", "parallel"),
        name="attn_mla",
    )(q, k, vt)


def _attn_diff_kernel(lam_ref, subln_ref, q_ref, k_ref, vt_ref, o_ref, *, n_lat_steps, s_len, l_ctx,
                      lambda_init):
    tq = ATTN_Q_TILE
    lp = lam_ref[...]
    lam = (jnp.exp(jnp.sum(lp[0:1] * lp[1:2], axis=-1, keepdims=True))
           - jnp.exp(jnp.sum(lp[2:3] * lp[3:4], axis=-1, keepdims=True)) + lambda_init)

    def tile(t, latent_query):
        rows = slice(t * tq, (t + 1) * tq)
        q = q_ref[0, rows, :]
        low = _lane_iota(q.shape) < C_HEAD_DIM
        zero = jnp.zeros_like(q)
        q2 = jnp.concatenate([jnp.where(low, q, zero), jnp.where(low, zero, q)], axis=0)
        acc, l = _flash_keys_major(q2, lambda lo, n: k_ref[0, lo:lo + n, :],
                                   lambda lo, n: vt_ref[0, :, lo:lo + n],
                                   _key_chunks(s_len, l_ctx, latent_query))
        o2 = acc / l
        o = o2[:, :tq] - lam * o2[:, tq:]
        o = o * lax.rsqrt(jnp.mean(o * o, axis=0, keepdims=True) + RMS_EPS) * subln_ref[...]
        o_ref[0, rows, :] = (o * (1.0 - lambda_init)).T.astype(_BF16)

    _per_step_tiles(tile, n_lat_steps)


def _attn_diff(qk, vt, lam_params, subln, s_len, lambda_init):
    b, p, _ = qk.shape
    tq = ATTN_Q_TILE * Q_TILES_PER_STEP
    return pl.pallas_call(
        functools.partial(_attn_diff_kernel, n_lat_steps=s_len // tq, s_len=s_len, l_ctx=p - s_len,
                          lambda_init=lambda_init),
        grid=(b, C_HEADS, pl.cdiv(p, tq)),
        in_specs=[pl.BlockSpec(lam_params.shape, lambda i, h, j: (0, 0)),
                  pl.BlockSpec((C_V_DIM, 1), lambda i, h, j: (0, 0)),
                  pl.BlockSpec((1, tq, LANES), lambda i, h, j: (i, j, h)),
                  pl.BlockSpec((1, p, LANES), lambda i, h, j: (i, 0, C_HEADS + h)),
                  pl.BlockSpec((1, C_V_DIM, p), lambda i, h, j: (i, h, 0))],
        out_specs=pl.BlockSpec((1, tq, LANES), lambda i, h, j: (i, j, h)),
        out_shape=jax.ShapeDtypeStruct((b, p, C_HEADS * C_V_DIM), _BF16),
        compiler_params=_cparams("parallel", "parallel", "parallel"),
        name="attn_diff",
    )(lam_params, subln.reshape(C_V_DIM, 1), qk, qk, vt)


def _layer_norm(z, g, b):
    mu = jnp.mean(z, axis=-1, keepdims=True)
    zc = z - mu
    var = jnp.mean(zc * zc, axis=-1, keepdims=True)
    return zc * lax.rsqrt(var + LN_EPS) * g + b


def _out_proj_kernel(o_ref, w_ref, x_ref, mod_ref, g_ref, b_ref, wr_ref, br_ref,
                     xo_ref, h_ref, lg_ref, *, alpha):
    y = jnp.dot(o_ref[0], w_ref[...], preferred_element_type=_F32)
    m = mod_ref[0]
    x = _layer_norm(alpha * x_ref[0] + m[2:3, :] * y, g_ref[...], b_ref[...])
    xo_ref[0] = x
    h = x * (1.0 + m[4:5, :]) + m[3:4, :]
    _store_rows_chunked(h_ref, (0,), h)
    tm = h.shape[0]
    h_hi = h.astype(_BF16)
    h_lo = (h - h_hi.astype(_F32)).astype(_BF16)
    t = jnp.dot(jnp.concatenate([h_hi, h_lo], axis=0), wr_ref[...], preferred_element_type=_F32)
    lg_ref[0] = t[:tm, :LANES] + t[tm:, :LANES] + t[:tm, LANES:] + br_ref[...]


def _out_proj(o, w_out, x, mods, ln_g, ln_b, w_router, b_router, alpha, n_lat_tiles):
    b, p, d = x.shape
    tm = ROW_TILE
    x_spec, mod_spec = _row_specs(b, p, d, n_lat_tiles, tm)
    full = lambda a: pl.BlockSpec(a.shape, lambda i, j: (0,) * a.ndim)
    n_o = o.shape[2]
    return pl.pallas_call(
        functools.partial(_out_proj_kernel, alpha=alpha),
        grid=(b, p // tm),
        in_specs=[pl.BlockSpec((1, tm, n_o), lambda i, j: (i, j, 0)), full(w_out), x_spec, mod_spec,
                  full(ln_g), full(ln_b), full(w_router), full(b_router)],
        out_specs=[x_spec, pl.BlockSpec((1, tm * (d // LANES), LANES), lambda i, j: (i, j, 0)),
                   pl.BlockSpec((1, tm, LANES), lambda i, j: (i, j, 0))],
        out_shape=[jax.ShapeDtypeStruct((b, p, d), _F32), jax.ShapeDtypeStruct((b, p * (d // LANES), LANES), _F32),
                   jax.ShapeDtypeStruct((b, p, LANES), _F32)],
        compiler_params=_cparams("parallel", "parallel"),
        name="out_proj_ln",
    )(o, w_out, x, mods, ln_g, ln_b, w_router, b_router)


def _route_kernel(lg_ref, info_ref, btab_ref, sel_ref, cnt_ref, start_ref, run_ref, *, block_rows):
    sweep, i = pl.program_id(0), pl.program_id(1)
    tm = lg_ref.shape[0]
    rows = pl.ds(pl.multiple_of(i * tm, tm), tm)
    lane = _lane_iota((tm, LANES))

    @pl.when((sweep == 0) & (i == 0))
    def _():
        cnt_ref[...] = jnp.zeros_like(cnt_ref)

    @pl.when(sweep == 0)
    def _():
        x = lg_ref[...]
        xg = jnp.where(lane < N_GROUPS, x, -jnp.inf)
        g_max = jnp.max(xg, axis=-1, keepdims=True)
        g_sel = jnp.min(jnp.where(xg == g_max, lane, LANES), axis=-1, keepdims=True)
        p_g = 1.0 / jnp.sum(jnp.exp(xg - g_max), axis=-1, keepdims=True)
        lo = N_GROUPS + g_sel * EXPERTS_PER_GROUP
        xe = jnp.where((lane >= lo) & (lane < lo + EXPERTS_PER_GROUP), x, -jnp.inf)
        v1 = jnp.max(xe, axis=-1, keepdims=True)
        i1 = jnp.min(jnp.where(xe == v1, lane, LANES), axis=-1, keepdims=True)
        xe = jnp.where(lane == i1, -jnp.inf, xe)
        v2 = jnp.max(xe, axis=-1, keepdims=True)
        i2 = jnp.min(jnp.where(xe == v2, lane, LANES), axis=-1, keepdims=True)
        t = jnp.exp(v2 - v1)
        w1 = p_g / (1.0 + t)
        sel_ref[rows, :] = jnp.where(lane == 0, i1.astype(_F32), jnp.where(lane == 1, i2.astype(_F32),
                                     jnp.where(lane == 2, w1, jnp.where(lane == 3, w1 * t, 0.0))))
        onehot = ((lane == i1) | (lane == i2)).astype(_F32)
        cnt_ref[...] += jnp.sum(onehot, axis=0, keepdims=True)

    @pl.when((sweep == 1) & (i == 0))
    def _():
        cnt = cnt_ref[...]
        padded = jnp.floor((cnt + (block_rows - 1)) * (1.0 / block_rows)) * block_rows
        r = lax.broadcasted_iota(jnp.int32, (LANES, LANES), 0)
        c = lax.broadcasted_iota(jnp.int32, (LANES, LANES), 1)
        before = (r < c).astype(_F32)
        start = jnp.dot(jnp.broadcast_to(padded, (8, LANES)), before, preferred_element_type=_F32,
                        precision=lax.Precision.HIGHEST)[0:1]
        start_ref[...] = start
        run_ref[...] = jnp.zeros_like(run_ref)
        nb = btab_ref.shape[0]
        row0 = (lax.broadcasted_iota(jnp.int32, (nb, LANES), 0) * block_rows).astype(_F32)
        is_e = (_lane_iota((nb, LANES)) >= N_GROUPS) & (_lane_iota((nb, LANES)) < N_GROUPS + N_EXPERTS)
        end = start + padded
        owner = jnp.sum(jnp.where(is_e & (end <= row0), 1.0, 0.0), axis=-1, keepdims=True)
        inside = is_e & (start <= row0) & (row0 < end)
        real = jnp.sum(jnp.where(inside, jnp.clip(cnt - (row0 - start), 0.0, block_rows), 0.0),
                       axis=-1, keepdims=True)
        bl = _lane_iota((nb, LANES))
        btab_ref[...] = jnp.where(bl == 0, jnp.minimum(owner, N_EXPERTS - 1.0),
                                  jnp.where(bl == 1, real, 0.0)).astype(jnp.int32)

    @pl.when(sweep == 1)
    def _():
        sel = sel_ref[rows, :]
        hit1 = lane == sel[:, 0:1].astype(jnp.int32)
        hit2 = lane == sel[:, 1:2].astype(jnp.int32)
        onehot = (hit1 | hit2).astype(_F32)
        row = lax.broadcasted_iota(jnp.int32, (tm, tm), 0)
        col = lax.broadcasted_iota(jnp.int32, (tm, tm), 1)
        earlier = (col < row).astype(_BF16)
        rank = jnp.dot(earlier, onehot.astype(_BF16), preferred_element_type=_F32) + run_ref[...]
        pos = rank + start_ref[...]
        d1 = jnp.sum(jnp.where(hit1, pos, 0.0), axis=-1, keepdims=True)
        d2 = jnp.sum(jnp.where(hit2, pos, 0.0), axis=-1, keepdims=True)
        run_ref[...] += jnp.sum(onehot, axis=0, keepdims=True)
        info_ref[...] = jnp.where(lane == 0, d1, jnp.where(lane == 1, d2, sel))


def _route(logits, n_blocks):
    t = logits.shape[0]
    tm = ROUTE_TILE
    nb_pad = -(-n_blocks // 8) * 8
    return pl.pallas_call(
        functools.partial(_route_kernel, block_rows=MOE_ROWS),
        grid=(2, t // tm),
        in_specs=[pl.BlockSpec((tm, LANES), lambda s, i: (i * (1 - s), 0))],
        out_specs=[pl.BlockSpec((tm, LANES), lambda s, i: (i * s, 0)),
                   pl.BlockSpec((nb_pad, LANES), lambda s, i: (0, 0))],
        out_shape=[jax.ShapeDtypeStruct((t, LANES), _F32), jax.ShapeDtypeStruct((nb_pad, LANES), jnp.int32)],
        scratch_shapes=[pltpu.VMEM((t, LANES), _F32)] + [pltpu.VMEM((1, LANES), _F32)] * 3,
        compiler_params=_cparams("arbitrary", "arbitrary"),
        name="moe_route",
    )(logits)


def _row_copy_wait(src_rows, dst_rows, sem):
    pltpu.make_async_copy(src_rows, dst_rows, sem).wait()


def _dispatch_kernel(dest_ref, btab_ref, h_ref, xb_ref, zeros, sem, zero_sem, *, n_blocks):
    ch = ROW_CHUNKS
    tm = h_ref.shape[0] // ch
    bm = MOE_ROWS
    base = pl.program_id(0) * tm

    @pl.when(pl.program_id(0) == 0)
    def _():
        zeros[...] = jnp.zeros_like(zeros)

        def tail_rows(blk):
            real = btab_ref[blk, 1]
            return jnp.where(real > 0, bm - real, 0)

        def tail_copy(row):
            return pltpu.make_async_copy(zeros.at[pl.ds(0, ch)], xb_ref.at[pl.ds(pl.multiple_of(row * ch, ch), ch)],
                                         zero_sem)

        def block_copy(blk):
            return pltpu.make_async_copy(zeros, xb_ref.at[pl.ds(blk * bm * ch, bm * ch)], zero_sem)

        for blk in range(n_blocks):
            first = blk * bm + btab_ref[blk, 1]
            lax.fori_loop(0, tail_rows(blk), lambda r, c: (tail_copy(first + r).start(), c)[1], 0)
            pl.when(btab_ref[blk, 1] == 0)(lambda: block_copy(blk).start())
        for blk in range(n_blocks):
            lax.fori_loop(0, tail_rows(blk), lambda r, c: (tail_copy(0).wait(), c)[1], 0)
            pl.when(btab_ref[blk, 1] == 0)(lambda: block_copy(blk).wait())

    for r in range(tm):
        for k in range(TOP_K):
            d = dest_ref[(base + r) * TOP_K + k]
            pltpu.make_async_copy(h_ref.at[pl.ds(r * ch, ch)], xb_ref.at[pl.ds(pl.multiple_of(d * ch, ch), ch)],
                                  sem).start()
    for k in range(TOP_K):
        _row_copy_wait(h_ref, xb_ref.at[pl.ds(0, tm * ch)], sem)


def _dispatch(dest, btab, h, n_blocks):
    ch = ROW_CHUNKS
    t, d = h.shape[0] // ch, h.shape[1]
    tm = ROW_TILE
    grid_spec = pltpu.PrefetchScalarGridSpec(
        num_scalar_prefetch=2,
        grid=(t // tm,),
        in_specs=[pl.BlockSpec((tm * ch, d), lambda i, dest, bt: (i, 0))],
        out_specs=pl.BlockSpec(memory_space=pl.ANY),
        scratch_shapes=[pltpu.VMEM((MOE_ROWS * ch, d), h.dtype), pltpu.SemaphoreType.DMA(()),
                        pltpu.SemaphoreType.DMA(())],
    )
    return pl.pallas_call(
        functools.partial(_dispatch_kernel, n_blocks=n_blocks),
        grid_spec=grid_spec,
        out_shape=jax.ShapeDtypeStruct((n_blocks * MOE_ROWS * ch, d), h.dtype),
        compiler_params=_cparams("arbitrary", disable_bounds_checks=True),
        name="moe_dispatch",
    )(dest, btab, h)


def _ffn_kernel(btab_ref, x_ref, wg_ref, wu_ref, wd_ref, y_ref, wg_bf, wu_bf, wd_bf):
    i = pl.program_id(0)
    expert, real = btab_ref[i, 0], btab_ref[i, 1]
    prev = btab_ref[jnp.maximum(i - 1, 0), 0]

    @pl.when((i == 0) | (expert != prev))
    def _():
        wg_bf[...] = wg_ref[0, 0].astype(_BF16)
        wu_bf[...] = wu_ref[0, 0].astype(_BF16)
        wd_bf[...] = wd_ref[0, 0].astype(_BF16)

    @pl.when(real > 0)
    def _():
        x = _load_rows_chunked(x_ref, (), MOE_ROWS, ROW_CHUNKS).astype(_BF16)
        g = jnp.dot(x, wg_bf[...], preferred_element_type=_F32)
        u = jnp.dot(x, wu_bf[...], preferred_element_type=_F32)
        a = (g * jax.nn.sigmoid(g) * u).astype(_BF16)
        _store_rows_chunked(y_ref, (), jnp.dot(a, wd_bf[...], preferred_element_type=_F32))

    @pl.when(real <= 0)
    def _():
        y_ref[...] = jnp.zeros_like(y_ref)


def _expert_ffn(btab, xb, w_gate, w_up, w_down, layer):
    ch = ROW_CHUNKS
    rows, d = xb.shape[0] // ch, xb.shape[1] * ch
    de = w_gate.shape[3]
    bm = MOE_ROWS
    w_map = lambda i, bt: (layer, bt[i, 0], 0, 0)
    grid_spec = pltpu.PrefetchScalarGridSpec(
        num_scalar_prefetch=1,
        grid=(rows // bm,),
        in_specs=[pl.BlockSpec((bm * ch, LANES), lambda i, bt: (i, 0)),
                  pl.BlockSpec((1, 1, d, de), w_map),
                  pl.BlockSpec((1, 1, d, de), w_map),
                  pl.BlockSpec((1, 1, de, d), w_map)],
        out_specs=pl.BlockSpec((bm * ch, LANES), lambda i, bt: (i, 0)),
        scratch_shapes=[pltpu.VMEM((d, de), _BF16), pltpu.VMEM((d, de), _BF16), pltpu.VMEM((de, d), _BF16)],
    )
    return pl.pallas_call(
        _ffn_kernel,
        grid_spec=grid_spec,
        out_shape=jax.ShapeDtypeStruct((rows * ch, LANES), _F32),
        compiler_params=_cparams("arbitrary"),
        name="expert_ffn",
    )(btab, xb, w_gate, w_up, w_down)


def _combine_kernel(dest_ref, info_ref, x_ref, mod_ref, g_ref, b_ref, yb_ref, xo_ref, buf, sem, *, alpha,
                    tiles_per_batch):
    ch = ROW_CHUNKS
    tm = x_ref.shape[1]
    i, j = pl.program_id(0), pl.program_id(1)
    nj = pl.num_programs(1)
    step = i * nj + j
    tile = i * tiles_per_batch + j

    def issue(at_tile, slot):
        base = at_tile * tm

        for r in range(tm):
            for k in range(TOP_K):
                d = dest_ref[(base + r) * TOP_K + k]
                pltpu.make_async_copy(yb_ref.at[pl.ds(pl.multiple_of(d * ch, ch), ch)],
                                      buf.at[slot, k, pl.ds(r * ch, ch)], sem.at[slot]).start()

    @pl.when(step == 0)
    def _():
        issue(tile, 0)

    @pl.when(step + 1 < pl.num_programs(0) * nj)
    def _():
        issue(jnp.where(j + 1 < nj, tile + 1, (i + 1) * tiles_per_batch), (step + 1) % 2)

    slot = step % 2
    for k in range(TOP_K):
        _row_copy_wait(yb_ref.at[pl.ds(0, tm * ch)], buf.at[slot, k], sem.at[slot])
    info = info_ref[0]
    y = (info[:, 2:3] * _load_rows_chunked(buf, (slot, 0), tm, ch)
         + info[:, 3:4] * _load_rows_chunked(buf, (slot, 1), tm, ch))
    m = mod_ref[0]
    xo_ref[0] = _layer_norm(alpha * x_ref[0] + m[5:6, :] * y, g_ref[...], b_ref[...])


def _combine(dest, info, yb, x, mods, ln_g, ln_b, alpha, n_lat_tiles, latent_only):
    b, p, d = x.shape
    tm = ROW_TILE
    n_tiles = n_lat_tiles if latent_only else p // tm
    grid_spec = pltpu.PrefetchScalarGridSpec(
        num_scalar_prefetch=1,
        grid=(b, n_tiles),
        in_specs=[pl.BlockSpec((1, tm, LANES), lambda i, j, dest: (i, j, 0)),
                  pl.BlockSpec((1, tm, d), lambda i, j, dest: (i, j, 0)),
                  pl.BlockSpec((1, 6, d), lambda i, j, dest: (jnp.where(j < n_lat_tiles, i, b), 0, 0)),
                  pl.BlockSpec(ln_g.shape, lambda i, j, dest: (0, 0)),
                  pl.BlockSpec(ln_b.shape, lambda i, j, dest: (0, 0)),
                  pl.BlockSpec(memory_space=pl.ANY)],
        out_specs=pl.BlockSpec((1, tm, d), lambda i, j, dest: (i, j, 0)),
        scratch_shapes=[pltpu.VMEM((2, TOP_K, tm * ROW_CHUNKS, LANES), _F32), pltpu.SemaphoreType.DMA((2,))],
    )
    return pl.pallas_call(
        functools.partial(_combine_kernel, alpha=alpha, tiles_per_batch=p // tm),
        grid_spec=grid_spec,
        out_shape=jax.ShapeDtypeStruct((b, n_tiles * tm, d), _F32),
        compiler_params=_cparams("arbitrary", "arbitrary", disable_bounds_checks=True),
        name="moe_combine_ln",
    )(dest, info.reshape(b, p, LANES), x, mods, ln_g, ln_b, yb)


def _moe_layer(h, logits, x, mods, ln_g, ln_b, w_gate, w_up, w_down, layer, alpha, n_lat_tiles, latent_only):
    t = logits.shape[0]
    n_blocks = (t * TOP_K) // MOE_ROWS + N_EXPERTS
    info, btab = _route(logits, n_blocks)
    dest = info[:, :TOP_K].astype(jnp.int32).reshape(t * TOP_K)
    xb = _dispatch(dest, btab, h, n_blocks)
    yb = _expert_ffn(btab, xb, w_gate, w_up, w_down, layer)
    return _combine(dest, info, yb, x, mods, ln_g, ln_b, alpha, n_lat_tiles, latent_only)


def _rope_tables(s_len, l_ctx, rot_dim, lane_lo, period):
    rows = s_len // GRID_W
    row = jnp.repeat(jnp.arange(rows, dtype=_F32), GRID_W)
    col = jnp.tile(jnp.arange(GRID_W, dtype=_F32), rows)
    axis_dim = rot_dim // 2
    half = axis_dim // 2
    inv_freq = ROPE_THETA ** (-jnp.arange(0, axis_dim, 2, dtype=_F32) / axis_dim)
    lane = jnp.arange(LANES)
    rel = (lane % period) - lane_lo
    active = (rel >= 0) & (rel < rot_dim)
    rel = jnp.clip(rel, 0, rot_dim - 1)
    use_col = rel >= axis_dim
    f = inv_freq[(rel % axis_dim) % half]
    ang = jnp.where(use_col[None, :], col[:, None], row[:, None]) * f[None, :]
    sign = jnp.where((rel % axis_dim) < half, -1.0, 1.0)
    cos = jnp.where(active[None, :], jnp.cos(ang), 1.0)
    sin = jnp.where(active[None, :], jnp.sin(ang) * sign[None, :], 0.0)
    ident = jnp.ones((l_ctx, LANES), _F32)
    return (jnp.concatenate([cos, ident], axis=0), jnp.concatenate([sin, 0.0 * ident], axis=0))


def _win_weights(w_in):
    d = w_in.shape[0]
    nq, nkv = A_HEADS * A_HEAD_DIM, A_KV_HEADS * A_HEAD_DIM
    k = w_in[:, nq:nq + nkv].reshape(d, A_KV_HEADS, 1, A_HEAD_DIM)
    k_dup = jnp.concatenate([k, k], axis=2).reshape(d, 2 * nkv)
    return jnp.concatenate([w_in[:, :nq], k_dup, w_in[:, nq + nkv:]], axis=1).astype(_BF16)


def _mla_weights(w_in, w_uq, w_ukv):
    d = w_in.shape[0]
    pad = LANES - B_NOPE - B_ROPE
    kr = jnp.concatenate([jnp.zeros((d, B_NOPE), _F32), w_in[:, B_Q_RANK + B_KV_RANK:],
                          jnp.zeros((d, pad), _F32)], axis=1)
    w_in_p = jnp.concatenate([w_in[:, :B_Q_RANK + B_KV_RANK], kr], axis=1).astype(_BF16)
    uq = w_uq.reshape(B_Q_RANK, B_HEADS, B_NOPE + B_ROPE)
    uq = jnp.pad(uq, ((0, 0), (0, 0), (0, pad))).reshape(B_Q_RANK, B_HEADS * LANES).astype(_BF16)
    ukv = w_ukv.reshape(B_KV_RANK, B_HEADS, B_NOPE + B_V)
    uk = jnp.pad(ukv[:, :, :B_NOPE], ((0, 0), (0, 0), (0, LANES - B_NOPE)))
    uk = uk.reshape(B_KV_RANK, B_HEADS * LANES).astype(_BF16)
    uv = ukv[:, :, B_NOPE:].reshape(B_KV_RANK, B_HEADS * B_V).astype(_BF16)
    return w_in_p, uq, uk, uv


def kernel(x, c, ctx, c_ctx, ada_w, ada_b, ln_mix_g, ln_mix_b, ln_ffn_g, ln_ffn_b, win_w_in, win_w_out, win_sink, mla_w_in, mla_q_norm, mla_kv_norm, mla_w_uq, mla_w_ukv, mla_w_out, dif_w_in, dif_lambda, dif_subln, dif_w_out, moe_w_grp, moe_b_grp, moe_w_rt, moe_b_rt, moe_w_gate, moe_w_up, moe_w_down):
    b, s_len, d = x.shape
    l_ctx = ctx.shape[1]
    depth = ada_w.shape[0]
    p = s_len + l_ctx
    assert s_len % ROW_TILE == 0 and l_ctx % ROW_TILE == 0 and s_len % GRID_W == 0
    assert s_len % (ATTN_Q_TILE * Q_TILES_PER_STEP) == 0 and l_ctx == ATTN_Q_TILE and s_len % l_ctx == 0
    assert d == ROW_CHUNKS * LANES and s_len % KEY_CHUNK == 0 and MOE_ROWS & (MOE_ROWS - 1) == 0 and (b * p) % ROUTE_TILE == 0
    alpha = (2.0 * depth) ** 0.25
    n_lat_tiles = s_len // ROW_TILE

    xs = jnp.concatenate([x, ctx], axis=1)
    cs = jnp.concatenate([c, c_ctx[None, :]], axis=0)
    mods = _ada_mods(cs, ada_w, ada_b).reshape(depth, b + 1, 6, d)
    cos64, sin64 = _rope_tables(s_len, l_ctx, A_HEAD_DIM, 0, A_HEAD_DIM)
    cos32, sin32 = _rope_tables(s_len, l_ctx, B_ROPE, B_NOPE, LANES)
    row = lambda v: v.reshape(1, -1)

    for i in range(depth):
        kind, slot = i % N_MIXERS, i // N_MIXERS
        if kind == 0:
            qk, vt = _proj_rope(xs, mods[i], _win_weights(win_w_in[slot]), cos64, sin64,
                                (A_HEADS + 2 * A_KV_HEADS) * A_HEAD_DIM, A_KV_HEADS * A_HEAD_DIM, n_lat_tiles,
                                n_q=A_HEADS * A_HEAD_DIM, q_scale=A_HEAD_DIM ** -0.5 * LOG2E)
            o = _attn_win(qk, vt, win_sink[slot], s_len, l_ctx)
            w_out = win_w_out[slot]
        elif kind == 1:
            w_in_p, uq, uk, uv = _mla_weights(mla_w_in[slot], mla_w_uq[slot], mla_w_ukv[slot])
            q, k, vt = _proj_mla(xs, mods[i], w_in_p, row(mla_q_norm[slot]), row(mla_kv_norm[slot]),
                                uq, uk, uv, cos32, sin32, n_lat_tiles)
            o = _attn_mla(q, k, vt, s_len)
            w_out = mla_w_out[slot]
        else:
            lambda_init = 0.8 - 0.6 * math.exp(-0.3 * i)
            qk, vt = _proj_rope(xs, mods[i], dif_w_in[slot].astype(_BF16), cos64, sin64,
                                2 * C_HEADS * 2 * C_HEAD_DIM, C_HEADS * C_V_DIM, n_lat_tiles,
                                n_q=C_HEADS * 2 * C_HEAD_DIM, q_scale=C_HEAD_DIM ** -0.5 * LOG2E)
            o = _attn_diff(qk, vt, dif_lambda[slot], dif_subln[slot], s_len, lambda_init)
            w_out = dif_w_out[slot]

        w_router = jnp.concatenate([moe_w_grp[i], moe_w_rt[i],
                                    jnp.zeros((d, LANES - N_GROUPS - N_EXPERTS), _F32)], axis=1)
        b_router = jnp.concatenate([moe_b_grp[i], moe_b_rt[i],
                                    jnp.zeros((LANES - N_GROUPS - N_EXPERTS,), _F32)])[None, :]
        w_hi = w_router.astype(_BF16)
        w_lo = (w_router - w_hi.astype(_F32)).astype(_BF16)
        xs, h, logits = _out_proj(o, w_out.astype(_BF16), xs, mods[i], row(ln_mix_g[i]), row(ln_mix_b[i]),
                                  jnp.concatenate([w_hi, w_lo], axis=1), b_router, alpha, n_lat_tiles)
        xs = _moe_layer(h.reshape(b * p * ROW_CHUNKS, LANES), logits.reshape(b * p, LANES), xs, mods[i], row(ln_ffn_g[i]),
                        row(ln_ffn_b[i]), moe_w_gate, moe_w_up, moe_w_down, i, alpha, n_lat_tiles,
                        latent_only=i == depth - 1)
    return xs
```

```python
import functools
import math

import jax
import jax.numpy as jnp
from jax import lax
from jax.experimental import pallas as pl
from jax.experimental.pallas import tpu as pltpu

LANES = 128
GRID_W = 64
ROPE_THETA = 10000.0
LN_EPS = 1e-5
RMS_EPS = 1e-6
LOG2E = math.log2(math.e)

A_HEADS, A_KV_HEADS, A_HEAD_DIM, A_WINDOW = 16, 4, 64, 128
A_GROUP = A_HEADS // A_KV_HEADS
B_HEADS, B_NOPE, B_ROPE, B_V, B_Q_RANK, B_KV_RANK = 16, 64, 32, 64, 512, 256
C_HEADS, C_HEAD_DIM = 8, 64
C_V_DIM = 2 * C_HEAD_DIM
N_GROUPS, EXPERTS_PER_GROUP, TOP_K = 4, 8, 2
N_EXPERTS = N_GROUPS * EXPERTS_PER_GROUP
N_MIXERS = 3

ROW_TILE = 256
ATTN_Q_TILE = 256
Q_TILES_PER_STEP = 4
WIN_BLOCK = 128
KEY_CHUNK = 1024
MOE_ROWS = 512
ROUTE_TILE = 512
ROW_CHUNKS = 8
VMEM_LIMIT = 56 * 1024 * 1024

_F32 = jnp.float32
_BF16 = jnp.bfloat16
_NT = (((1,), (1,)), ((), ()))


def _cparams(*sem, **kw):
    return pltpu.CompilerParams(dimension_semantics=sem, vmem_limit_bytes=VMEM_LIMIT, **kw)


def _lane_iota(shape):
    return lax.broadcasted_iota(jnp.int32, shape, len(shape) - 1)


def _rope(a, cos, sin_signed, half):
    lane = _lane_iota(a.shape)
    swapped = jnp.where(lane % (2 * half) < half,
                        pltpu.roll(a, a.shape[-1] - half, axis=1),
                        pltpu.roll(a, half, axis=1))
    return a * cos + swapped * sin_signed


def _store_rows_chunked(ref, lead, val):
    n, d = val.shape
    c = d // LANES
    for k in range(c):
        ref[lead + (pl.ds(k, n, stride=c), slice(None))] = val[:, k * LANES:(k + 1) * LANES]


def _load_rows_chunked(ref, lead, n, c):
    return jnp.concatenate([ref[lead + (pl.ds(k, n, stride=c), slice(None))] for k in range(c)], axis=1)


def _modulated(x_ref, mod_ref, shift_row):
    m = mod_ref[0]
    return x_ref[0] * (1.0 + m[shift_row + 1:shift_row + 2, :]) + m[shift_row:shift_row + 1, :]


def _ada_kernel(c_ref, w_ref, b_ref, o_ref):
    c = c_ref[...]
    s = c * jax.nn.sigmoid(c)
    o_ref[0] = jnp.dot(s, w_ref[0], preferred_element_type=_F32,
                       precision=lax.Precision.HIGHEST) + b_ref[0]


def _ada_mods(cs, ada_w, ada_b):
    depth, d, n = ada_w.shape
    r = cs.shape[0]
    tn = 1024
    return pl.pallas_call(
        _ada_kernel,
        grid=(depth, n // tn),
        in_specs=[pl.BlockSpec((r, d), lambda i, j: (0, 0)),
                  pl.BlockSpec((1, d, tn), lambda i, j: (i, 0, j)),
                  pl.BlockSpec((1, 1, tn), lambda i, j: (i, 0, j))],
        out_specs=pl.BlockSpec((1, r, tn), lambda i, j: (i, 0, j)),
        out_shape=jax.ShapeDtypeStruct((depth, r, n), _F32),
        compiler_params=_cparams("parallel", "parallel"),
        name="ada_mods",
    )(cs, ada_w, ada_b.reshape(depth, 1, n))


def _row_specs(b, p, d, n_lat_tiles, tm):
    x_spec = pl.BlockSpec((1, tm, d), lambda i, j: (i, j, 0))
    mod_spec = pl.BlockSpec((1, 6, d), lambda i, j: (jnp.where(j < n_lat_tiles, i, b), 0, 0))
    return x_spec, mod_spec


def _proj_rope_kernel(x_ref, mod_ref, w_ref, cos_ref, sin_ref, o_ref, *vt_ref, n_rope, n_q, q_scale):
    h = _modulated(x_ref, mod_ref, 0).astype(_BF16)
    acc = jnp.dot(h, w_ref[...], preferred_element_type=_F32)
    cos, sin = cos_ref[...], sin_ref[...]
    n = o_ref.shape[2]
    for j in range(n // LANES):
        a = acc[:, j * LANES:(j + 1) * LANES]
        if j * LANES < n_rope:
            a = _rope(a, cos, sin, 16)
        if j * LANES < n_q and q_scale != 1.0:
            a = a * q_scale
        o_ref[0, :, j * LANES:(j + 1) * LANES] = a.astype(_BF16)
    if vt_ref:
        vt_ref[0][0] = acc[:, n:].T.astype(_BF16)


def _proj_rope(x, mods, w, cos, sin, n_rope, n_t, n_lat_tiles, n_q=0, q_scale=1.0):
    b, p, d = x.shape
    n_all = w.shape[1]
    n = n_all - n_t
    tm = ROW_TILE
    x_spec, mod_spec = _row_specs(b, p, d, n_lat_tiles, tm)
    out_specs = [pl.BlockSpec((1, tm, n), lambda i, j: (i, j, 0))]
    out_shape = [jax.ShapeDtypeStruct((b, p, n), _BF16)]
    if n_t:
        out_specs.append(pl.BlockSpec((1, n_t, tm), lambda i, j: (i, 0, j)))
        out_shape.append(jax.ShapeDtypeStruct((b, n_t, p), _BF16))
    return pl.pallas_call(
        functools.partial(_proj_rope_kernel, n_rope=n_rope, n_q=n_q, q_scale=q_scale),
        grid=(b, p // tm),
        in_specs=[x_spec, mod_spec,
                  pl.BlockSpec((d, n_all), lambda i, j: (0, 0)),
                  pl.BlockSpec((tm, LANES), lambda i, j: (j, 0)),
                  pl.BlockSpec((tm, LANES), lambda i, j: (j, 0))],
        out_specs=out_specs,
        out_shape=out_shape,
        compiler_params=_cparams("parallel", "parallel"),
        name="proj_rope",
    )(x, mods, w, cos, sin)


def _rms(x, g):
    return x * lax.rsqrt(jnp.mean(x * x, axis=-1, keepdims=True) + RMS_EPS) * g


def _proj_mla_kernel(x_ref, mod_ref, win_ref, qn_ref, kvn_ref, wuq_ref, wuk_ref, wv_ref,
                     cos_ref, sin_ref, q_ref, k_ref, vt_ref, *, scale):
    h = _modulated(x_ref, mod_ref, 0).astype(_BF16)
    c = jnp.dot(h, win_ref[...], preferred_element_type=_F32)
    cq = _rms(c[:, :B_Q_RANK], qn_ref[...]).astype(_BF16)
    ckv = _rms(c[:, B_Q_RANK:B_Q_RANK + B_KV_RANK], kvn_ref[...]).astype(_BF16)
    cos, sin = cos_ref[...], sin_ref[...]
    k_rope = _rope(c[:, B_Q_RANK + B_KV_RANK:], cos, sin, 8)
    q = jnp.dot(cq, wuq_ref[...], preferred_element_type=_F32)
    k_nope = jnp.dot(ckv, wuk_ref[...], preferred_element_type=_F32)
    for hd in range(B_HEADS):
        sl = slice(hd * LANES, (hd + 1) * LANES)
        q_ref[0, :, sl] = (_rope(q[:, sl], cos, sin, 8) * scale).astype(_BF16)
        k_ref[0, :, sl] = (k_nope[:, sl] + k_rope).astype(_BF16)
    vt_ref[0] = jnp.dot(ckv, wv_ref[...], preferred_element_type=_F32).T.astype(_BF16)


def _proj_mla(x, mods, w_in, q_norm, kv_norm, w_uq, w_uk, w_v, cos, sin, n_lat_tiles):
    b, p, d = x.shape
    tm = ROW_TILE
    x_spec, mod_spec = _row_specs(b, p, d, n_lat_tiles, tm)
    full = lambda a: pl.BlockSpec(a.shape, lambda i, j: (0,) * a.ndim)
    nq, nv = w_uq.shape[1], w_v.shape[1]
    tile = lambda n: pl.BlockSpec((1, tm, n), lambda i, j: (i, j, 0))
    return pl.pallas_call(
        functools.partial(_proj_mla_kernel, scale=(B_NOPE + B_ROPE) ** -0.5 * LOG2E),
        grid=(b, p // tm),
        in_specs=[x_spec, mod_spec, full(w_in), full(q_norm), full(kv_norm), full(w_uq), full(w_uk),
                  full(w_v),
                  pl.BlockSpec((tm, LANES), lambda i, j: (j, 0)),
                  pl.BlockSpec((tm, LANES), lambda i, j: (j, 0))],
        out_specs=[tile(nq), tile(nq), pl.BlockSpec((1, nv, tm), lambda i, j: (i, 0, j))],
        out_shape=[jax.ShapeDtypeStruct((b, p, nq), _BF16), jax.ShapeDtypeStruct((b, p, nq), _BF16),
                   jax.ShapeDtypeStruct((b, nv, p), _BF16)],
        compiler_params=_cparams("parallel", "parallel"),
        name="proj_mla",
    )(x, mods, w_in, q_norm, kv_norm, w_uq, w_uk, w_v, cos, sin)


def _attn_win_kernel(sink_ref, q_ref, k0_ref, k1_ref, k2_ref, kc_ref, vt0_ref, vt1_ref, vt2_ref, vtc_ref,
                     o_ref, *, n_lat_tiles):
    j = pl.program_id(1)
    w = WIN_BLOCK
    n_ctx = kc_ref.shape[1]
    n_keys = 3 * w + n_ctx
    nq = A_GROUP * w
    row = lax.broadcasted_iota(jnp.int32, (n_keys, nq), 0)
    q_pos = j * w + lax.broadcasted_iota(jnp.int32, (n_keys, nq), 1) % w
    k_pos = (j - 1) * w + row
    n_latent_keys = jnp.where(j < n_lat_tiles, n_lat_tiles * w, 0)
    valid = (row >= 3 * w) | ((jnp.abs(q_pos - k_pos) <= A_WINDOW) & (k_pos >= 0) & (k_pos < n_latent_keys))
    low = _lane_iota((w, LANES)) < A_HEAD_DIM

    def scores(kvh):
        cs = slice(kvh * LANES, (kvh + 1) * LANES)
        kk = jnp.concatenate([k0_ref[0, :, cs], k1_ref[0, :, cs], k2_ref[0, :, cs], kc_ref[0, :, cs]], axis=0)
        qa = q_ref[0, :, 2 * kvh * LANES:(2 * kvh + 1) * LANES]
        qb = q_ref[0, :, (2 * kvh + 1) * LANES:(2 * kvh + 2) * LANES]
        zero = jnp.zeros_like(qa)
        q4 = jnp.concatenate([jnp.where(low, qa, zero), jnp.where(low, zero, qa),
                              jnp.where(low, qb, zero), jnp.where(low, zero, qb)], axis=0)
        return lax.dot_general(kk, q4, _NT, preferred_element_type=_F32)

    outs = []
    s_next = scores(0)
    for kvh in range(A_KV_HEADS):
        s = s_next
        if kvh + 1 < A_KV_HEADS:
            s_next = scores(kvh + 1)
        rs = slice(kvh * A_HEAD_DIM, (kvh + 1) * A_HEAD_DIM)
        vt = jnp.concatenate([vt0_ref[0, rs, :], vt1_ref[0, rs, :], vt2_ref[0, rs, :], vtc_ref[0, rs, :]], axis=1)
        sink = jnp.concatenate([jnp.full((1, w), sink_ref[kvh * A_GROUP + g] * LOG2E, _F32)
                                for g in range(A_GROUP)], axis=1)
        s = jnp.where(valid, s, -jnp.inf)
        m = jnp.maximum(jnp.max(s, axis=0, keepdims=True), sink)
        e = jnp.exp2(s - m)
        denom = jnp.sum(e, axis=0, keepdims=True) + jnp.exp2(sink - m)
        o = jnp.dot(vt, e.astype(_BF16), preferred_element_type=_F32) / denom
        outs += [o[:, g * w:(g + 1) * w] for g in range(A_GROUP)]
    o_ref[0] = jnp.concatenate(outs, axis=0).T.astype(_BF16)


def _attn_win(qk, vt, sink, s_len, l_ctx):
    b, p, _ = qk.shape
    w = WIN_BLOCK
    n_lat = s_len // w
    nq = A_HEADS * A_HEAD_DIM
    kw = A_KV_HEADS * LANES
    nv = A_KV_HEADS * A_HEAD_DIM
    ctx_blk = s_len // l_ctx

    def nb(off):
        return lambda i, j: (i, jnp.clip(j + off, 0, n_lat - 1), nq // kw)

    def nbt(off):
        return lambda i, j: (i, 0, jnp.clip(j + off, 0, n_lat - 1))

    return pl.pallas_call(
        functools.partial(_attn_win_kernel, n_lat_tiles=n_lat),
        grid=(b, p // w),
        in_specs=[pl.BlockSpec(memory_space=pltpu.SMEM),
                  pl.BlockSpec((1, w, nq), lambda i, j: (i, j, 0)),
                  pl.BlockSpec((1, w, kw), nb(-1)), pl.BlockSpec((1, w, kw), nb(0)),
                  pl.BlockSpec((1, w, kw), nb(1)),
                  pl.BlockSpec((1, l_ctx, kw), lambda i, j: (i, ctx_blk, nq // kw)),
                  pl.BlockSpec((1, nv, w), nbt(-1)), pl.BlockSpec((1, nv, w), nbt(0)),
                  pl.BlockSpec((1, nv, w), nbt(1)),
                  pl.BlockSpec((1, nv, l_ctx), lambda i, j: (i, 0, ctx_blk))],
        out_specs=pl.BlockSpec((1, w, nq), lambda i, j: (i, j, 0)),
        out_shape=jax.ShapeDtypeStruct((b, p, nq), _BF16),
        compiler_params=_cparams("parallel", "parallel"),
        name="attn_win",
    )(sink, qk, qk, qk, qk, qk, vt, vt, vt, vt)


def _key_chunks(s_len, l_ctx, latent_query):
    chunks = [(lo, KEY_CHUNK) for lo in range(0, s_len, KEY_CHUNK)] if latent_query else []
    return chunks + [(s_len, l_ctx)]


def _flash_keys_major(q, k_chunk, vt_chunk, chunks):
    def scores(c):
        return lax.dot_general(k_chunk(*chunks[c]), q, _NT, preferred_element_type=_F32)

    m = acc = l = None
    pending = None
    s_next = scores(0)

    def flush(acc):
        (lo, n), e, corr = pending
        pv = jnp.dot(vt_chunk(lo, n), e, preferred_element_type=_F32)
        return pv if acc is None else acc * corr + pv

    for c in range(len(chunks)):
        s = s_next
        if c + 1 < len(chunks):
            s_next = scores(c + 1)
        if pending is not None:
            acc = flush(acc)
        m_chunk = jnp.max(s, axis=0, keepdims=True)
        m_new = m_chunk if m is None else jnp.maximum(m, m_chunk)
        e = jnp.exp2(s - m_new)
        l_chunk = jnp.sum(e, axis=0, keepdims=True)
        corr = None if m is None else jnp.exp2(m - m_new)
        l = l_chunk if m is None else l * corr + l_chunk
        pending = (chunks[c], e.astype(_BF16), corr)
        m = m_new
    return flush(acc), l


def _per_step_tiles(tile, n_lat_steps):
    j = pl.program_id(2)

    @pl.when(j < n_lat_steps)
    def _():
        for t in range(Q_TILES_PER_STEP):
            tile(t, True)

    @pl.when(j >= n_lat_steps)
    def _():
        tile(0, False)


def _attn_mla_kernel(q_ref, k_ref, vt_ref, o_ref, *, n_lat_steps, s_len, l_ctx):
    tq = ATTN_Q_TILE

    def tile(t, latent_query):
        rows = slice(t * tq, (t + 1) * tq)
        q = q_ref[0, rows, :]
        first = _lane_iota(q.shape) < LANES
        zero = jnp.zeros_like(q)
        q2 = jnp.concatenate([jnp.where(first, q, zero), jnp.where(first, zero, q)], axis=0)
        acc, l = _flash_keys_major(q2, lambda lo, n: k_ref[0, lo:lo + n, :],
                                   lambda lo, n: vt_ref[0, :, lo:lo + n],
                                   _key_chunks(s_len, l_ctx, latent_query))
        o2 = acc / l
        o = jnp.concatenate([o2[:B_V, :tq], o2[B_V:, tq:]], axis=0)
        o_ref[0, rows, :] = o.T.astype(_BF16)

    _per_step_tiles(tile, n_lat_steps)


def _attn_mla(q, k, vt, s_len):
    b, p, _ = q.shape
    tq = ATTN_Q_TILE * Q_TILES_PER_STEP
    pairs = B_HEADS // 2
    return pl.pallas_call(
        functools.partial(_attn_mla_kernel, n_lat_steps=s_len // tq, s_len=s_len, l_ctx=p - s_len),
        grid=(b, pairs, pl.cdiv(p, tq)),
        in_specs=[pl.BlockSpec((1, tq, 2 * LANES), lambda i, c, j: (i, j, c)),
                  pl.BlockSpec((1, p, 2 * LANES), lambda i, c, j: (i, 0, c)),
                  pl.BlockSpec((1, 2 * B_V, p), lambda i, c, j: (i, c, 0))],
        out_specs=pl.BlockSpec((1, tq, 2 * B_V), lambda i, c, j: (i, j, c)),
        out_shape=jax.ShapeDtypeStruct((b, p, B_HEADS * B_V), _BF16),
        compiler_params=_cparams("parallel", "parallel", "parallel"),
        name="attn_mla",
    )(q, k, vt)


def _attn_diff_kernel(lam_ref, subln_ref, q_ref, k_ref, vt_ref, o_ref, *, n_lat_steps, s_len, l_ctx,
                      lambda_init):
    tq = ATTN_Q_TILE
    lp = lam_ref[...]
    lam = (jnp.exp(jnp.sum(lp[0:1] * lp[1:2], axis=-1, keepdims=True))
           - jnp.exp(jnp.sum(lp[2:3] * lp[3:4], axis=-1, keepdims=True)) + lambda_init)

    def tile(t, latent_query):
        rows = slice(t * tq, (t + 1) * tq)
        q = q_ref[0, rows, :]
        low = _lane_iota(q.shape) < C_HEAD_DIM
        zero = jnp.zeros_like(q)
        q2 = jnp.concatenate([jnp.where(low, q, zero), jnp.where(low, zero, q)], axis=0)
        acc, l = _flash_keys_major(q2, lambda lo, n: k_ref[0, lo:lo + n, :],
                                   lambda lo, n: vt_ref[0, :, lo:lo + n],
                                   _key_chunks(s_len, l_ctx, latent_query))
        o2 = acc / l
        o = o2[:, :tq] - lam * o2[:, tq:]
        o = o * lax.rsqrt(jnp.mean(o * o, axis=0, keepdims=True) + RMS_EPS) * subln_ref[...]
        o_ref[0, rows, :] = (o * (1.0 - lambda_init)).T.astype(_BF16)

    _per_step_tiles(tile, n_lat_steps)


def _attn_diff(qk, vt, lam_params, subln, s_len, lambda_init):
    b, p, _ = qk.shape
    tq = ATTN_Q_TILE * Q_TILES_PER_STEP
    return pl.pallas_call(
        functools.partial(_attn_diff_kernel, n_lat_steps=s_len // tq, s_len=s_len, l_ctx=p - s_len,
                          lambda_init=lambda_init),
        grid=(b, C_HEADS, pl.cdiv(p, tq)),
        in_specs=[pl.BlockSpec(lam_params.shape, lambda i, h, j: (0, 0)),
                  pl.BlockSpec((C_V_DIM, 1), lambda i, h, j: (0, 0)),
                  pl.BlockSpec((1, tq, LANES), lambda i, h, j: (i, j, h)),
                  pl.BlockSpec((1, p, LANES), lambda i, h, j: (i, 0, C_HEADS + h)),
                  pl.BlockSpec((1, C_V_DIM, p), lambda i, h, j: (i, h, 0))],
        out_specs=pl.BlockSpec((1, tq, LANES), lambda i, h, j: (i, j, h)),
        out_shape=jax.ShapeDtypeStruct((b, p, C_HEADS * C_V_DIM), _BF16),
        compiler_params=_cparams("parallel", "parallel", "parallel"),
        name="attn_diff",
    )(lam_params, subln.reshape(C_V_DIM, 1), qk, qk, vt)


def _layer_norm(z, g, b):
    mu = jnp.mean(z, axis=-1, keepdims=True)
    zc = z - mu
    var = jnp.mean(zc * zc, axis=-1, keepdims=True)
    return zc * lax.rsqrt(var + LN_EPS) * g + b


def _out_proj_kernel(o_ref, w_ref, x_ref, mod_ref, g_ref, b_ref, wr_ref, br_ref,
                     xo_ref, h_ref, lg_ref, *, alpha):
    y = jnp.dot(o_ref[0], w_ref[...], preferred_element_type=_F32)
    m = mod_ref[0]
    x = _layer_norm(alpha * x_ref[0] + m[2:3, :] * y, g_ref[...], b_ref[...])
    xo_ref[0] = x
    h = x * (1.0 + m[4:5, :]) + m[3:4, :]
    _store_rows_chunked(h_ref, (0,), h)
    tm = h.shape[0]
    h_hi = h.astype(_BF16)
    h_lo = (h - h_hi.astype(_F32)).astype(_BF16)
    t = jnp.dot(jnp.concatenate([h_hi, h_lo], axis=0), wr_ref[...], preferred_element_type=_F32)
    lg_ref[0] = t[:tm, :LANES] + t[tm:, :LANES] + t[:tm, LANES:] + br_ref[...]


def _out_proj(o, w_out, x, mods, ln_g, ln_b, w_router, b_router, alpha, n_lat_tiles):
    b, p, d = x.shape
    tm = ROW_TILE
    x_spec, mod_spec = _row_specs(b, p, d, n_lat_tiles, tm)
    full = lambda a: pl.BlockSpec(a.shape, lambda i, j: (0,) * a.ndim)
    n_o = o.shape[2]
    return pl.pallas_call(
        functools.partial(_out_proj_kernel, alpha=alpha),
        grid=(b, p // tm),
        in_specs=[pl.BlockSpec((1, tm, n_o), lambda i, j: (i, j, 0)), full(w_out), x_spec, mod_spec,
                  full(ln_g), full(ln_b), full(w_router), full(b_router)],
        out_specs=[x_spec, pl.BlockSpec((1, tm * (d // LANES), LANES), lambda i, j: (i, j, 0)),
                   pl.BlockSpec((1, tm, LANES), lambda i, j: (i, j, 0))],
        out_shape=[jax.ShapeDtypeStruct((b, p, d), _F32), jax.ShapeDtypeStruct((b, p * (d // LANES), LANES), _F32),
                   jax.ShapeDtypeStruct((b, p, LANES), _F32)],
        compiler_params=_cparams("parallel", "parallel"),
        name="out_proj_ln",
    )(o, w_out, x, mods, ln_g, ln_b, w_router, b_router)


def _route_kernel(lg_ref, info_ref, btab_ref, sel_ref, cnt_ref, start_ref, run_ref, *, block_rows):
    sweep, i = pl.program_id(0), pl.program_id(1)
    tm = lg_ref.shape[0]
    rows = pl.ds(pl.multiple_of(i * tm, tm), tm)
    lane = _lane_iota((tm, LANES))

    @pl.when((sweep == 0) & (i == 0))
    def _():
        cnt_ref[...] = jnp.zeros_like(cnt_ref)

    @pl.when(sweep == 0)
    def _():
        x = lg_ref[...]
        xg = jnp.where(lane < N_GROUPS, x, -jnp.inf)
        g_max = jnp.max(xg, axis=-1, keepdims=True)
        g_sel = jnp.min(jnp.where(xg == g_max, lane, LANES), axis=-1, keepdims=True)
        p_g = 1.0 / jnp.sum(jnp.exp(xg - g_max), axis=-1, keepdims=True)
        lo = N_GROUPS + g_sel * EXPERTS_PER_GROUP
        xe = jnp.where((lane >= lo) & (lane < lo + EXPERTS_PER_GROUP), x, -jnp.inf)
        v1 = jnp.max(xe, axis=-1, keepdims=True)
        i1 = jnp.min(jnp.where(xe == v1, lane, LANES), axis=-1, keepdims=True)
        xe = jnp.where(lane == i1, -jnp.inf, xe)
        v2 = jnp.max(xe, axis=-1, keepdims=True)
        i2 = jnp.min(jnp.where(xe == v2, lane, LANES), axis=-1, keepdims=True)
        t = jnp.exp(v2 - v1)
        w1 = p_g / (1.0 + t)
        sel_ref[rows, :] = jnp.where(lane == 0, i1.astype(_F32), jnp.where(lane == 1, i2.astype(_F32),
                                     jnp.where(lane == 2, w1, jnp.where(lane == 3, w1 * t, 0.0))))
        onehot = ((lane == i1) | (lane == i2)).astype(_F32)
        cnt_ref[...] += jnp.sum(onehot, axis=0, keepdims=True)

    @pl.when((sweep == 1) & (i == 0))
    def _():
        cnt = cnt_ref[...]
        padded = jnp.floor((cnt + (block_rows - 1)) * (1.0 / block_rows)) * block_rows
        r = lax.broadcasted_iota(jnp.int32, (LANES, LANES), 0)
        c = lax.broadcasted_iota(jnp.int32, (LANES, LANES), 1)
        before = (r < c).astype(_F32)
        start = jnp.dot(jnp.broadcast_to(padded, (8, LANES)), before, preferred_element_type=_F32,
                        precision=lax.Precision.HIGHEST)[0:1]
        start_ref[...] = start
        run_ref[...] = jnp.zeros_like(run_ref)
        nb = btab_ref.shape[0]
        row0 = (lax.broadcasted_iota(jnp.int32, (nb, LANES), 0) * block_rows).astype(_F32)
        is_e = (_lane_iota((nb, LANES)) >= N_GROUPS) & (_lane_iota((nb, LANES)) < N_GROUPS + N_EXPERTS)
        end = start + padded
        owner = jnp.sum(jnp.where(is_e & (end <= row0), 1.0, 0.0), axis=-1, keepdims=True)
        inside = is_e & (start <= row0) & (row0 < end)
        real = jnp.sum(jnp.where(inside, jnp.clip(cnt - (row0 - start), 0.0, block_rows), 0.0),
                       axis=-1, keepdims=True)
        bl = _lane_iota((nb, LANES))
        btab_ref[...] = jnp.where(bl == 0, jnp.minimum(owner, N_EXPERTS - 1.0),
                                  jnp.where(bl == 1, real, 0.0)).astype(jnp.int32)

    @pl.when(sweep == 1)
    def _():
        sel = sel_ref[rows, :]
        hit1 = lane == sel[:, 0:1].astype(jnp.int32)
        hit2 = lane == sel[:, 1:2].astype(jnp.int32)
        onehot = (hit1 | hit2).astype(_F32)
        row = lax.broadcasted_iota(jnp.int32, (tm, tm), 0)
        col = lax.broadcasted_iota(jnp.int32, (tm, tm), 1)
        earlier = (col < row).astype(_BF16)
        rank = jnp.dot(earlier, onehot.astype(_BF16), preferred_element_type=_F32) + run_ref[...]
        pos = rank + start_ref[...]
        d1 = jnp.sum(jnp.where(hit1, pos, 0.0), axis=-1, keepdims=True)
        d2 = jnp.sum(jnp.where(hit2, pos, 0.0), axis=-1, keepdims=True)
        run_ref[...] += jnp.sum(onehot, axis=0, keepdims=True)
        info_ref[...] = jnp.where(lane == 0, d1, jnp.where(lane == 1, d2, sel))


def _route(logits, n_blocks):
    t = logits.shape[0]
    tm = ROUTE_TILE
    nb_pad = -(-n_blocks // 8) * 8
    return pl.pallas_call(
        functools.partial(_route_kernel, block_rows=MOE_ROWS),
        grid=(2, t // tm),
        in_specs=[pl.BlockSpec((tm, LANES), lambda s, i: (i * (1 - s), 0))],
        out_specs=[pl.BlockSpec((tm, LANES), lambda s, i: (i * s, 0)),
                   pl.BlockSpec((nb_pad, LANES), lambda s, i: (0, 0))],
        out_shape=[jax.ShapeDtypeStruct((t, LANES), _F32), jax.ShapeDtypeStruct((nb_pad, LANES), jnp.int32)],
        scratch_shapes=[pltpu.VMEM((t, LANES), _F32)] + [pltpu.VMEM((1, LANES), _F32)] * 3,
        compiler_params=_cparams("arbitrary", "arbitrary"),
        name="moe_route",
    )(logits)


def _row_copy_wait(src_rows, dst_rows, sem):
    pltpu.make_async_copy(src_rows, dst_rows, sem).wait()


def _dispatch_kernel(dest_ref, btab_ref, h_ref, xb_ref, zeros, sem, zero_sem, *, n_blocks):
    ch = ROW_CHUNKS
    tm = h_ref.shape[0] // ch
    bm = MOE_ROWS
    base = pl.program_id(0) * tm

    def tail_rows(blk):
        real = btab_ref[blk, 1]
        return jnp.where(real > 0, bm - real, 0)

    def tail_copy(row):
        return pltpu.make_async_copy(zeros.at[pl.ds(0, ch)], xb_ref.at[pl.ds(pl.multiple_of(row * ch, ch), ch)],
                                     zero_sem)

    def block_copy(blk):
        return pltpu.make_async_copy(zeros, xb_ref.at[pl.ds(blk * bm * ch, bm * ch)], zero_sem)

    @pl.when(pl.program_id(0) == 0)
    def _():
        zeros[...] = jnp.zeros_like(zeros)
        for blk in range(n_blocks):
            first = blk * bm + btab_ref[blk, 1]
            lax.fori_loop(0, tail_rows(blk), lambda r, c: (tail_copy(first + r).start(), c)[1], 0)
            pl.when(btab_ref[blk, 1] == 0)(lambda: block_copy(blk).start())

    for r in range(tm):
        for k in range(TOP_K):
            d = dest_ref[(base + r) * TOP_K + k]
            pltpu.make_async_copy(h_ref.at[pl.ds(r * ch, ch)], xb_ref.at[pl.ds(pl.multiple_of(d * ch, ch), ch)],
                                  sem).start()
    for k in range(TOP_K):
        _row_copy_wait(h_ref, xb_ref.at[pl.ds(0, tm * ch)], sem)

    @pl.when(pl.program_id(0) == pl.num_programs(0) - 1)
    def _():
        for blk in range(n_blocks):
            lax.fori_loop(0, tail_rows(blk), lambda r, c: (tail_copy(0).wait(), c)[1], 0)
            pl.when(btab_ref[blk, 1] == 0)(lambda: block_copy(blk).wait())


def _dispatch(dest, btab, h, n_blocks):
    ch = ROW_CHUNKS
    t, d = h.shape[0] // ch, h.shape[1]
    tm = ROW_TILE
    grid_spec = pltpu.PrefetchScalarGridSpec(
        num_scalar_prefetch=2,
        grid=(t // tm,),
        in_specs=[pl.BlockSpec((tm * ch, d), lambda i, dest, bt: (i, 0))],
        out_specs=pl.BlockSpec(memory_space=pl.ANY),
        scratch_shapes=[pltpu.VMEM((MOE_ROWS * ch, d), h.dtype), pltpu.SemaphoreType.DMA(()),
                        pltpu.SemaphoreType.DMA(())],
    )
    return pl.pallas_call(
        functools.partial(_dispatch_kernel, n_blocks=n_blocks),
        grid_spec=grid_spec,
        out_shape=jax.ShapeDtypeStruct((n_blocks * MOE_ROWS * ch, d), h.dtype),
        compiler_params=_cparams("arbitrary", disable_bounds_checks=True),
        name="moe_dispatch",
    )(dest, btab, h)


def _ffn_kernel(btab_ref, x_ref, wg_ref, wu_ref, wd_ref, y_ref, wg_bf, wu_bf, wd_bf):
    i = pl.program_id(0)
    expert, real = btab_ref[i, 0], btab_ref[i, 1]
    prev = btab_ref[jnp.maximum(i - 1, 0), 0]

    @pl.when((i == 0) | (expert != prev))
    def _():
        wg_bf[...] = wg_ref[0, 0].astype(_BF16)
        wu_bf[...] = wu_ref[0, 0].astype(_BF16)
        wd_bf[...] = wd_ref[0, 0].astype(_BF16)

    @pl.when(real > 0)
    def _():
        x = _load_rows_chunked(x_ref, (), MOE_ROWS, ROW_CHUNKS).astype(_BF16)
        g = jnp.dot(x, wg_bf[...], preferred_element_type=_F32)
        u = jnp.dot(x, wu_bf[...], preferred_element_type=_F32)
        a = (g * jax.nn.sigmoid(g) * u).astype(_BF16)
        _store_rows_chunked(y_ref, (), jnp.dot(a, wd_bf[...], preferred_element_type=_F32))

    @pl.when(real <= 0)
    def _():
        y_ref[...] = jnp.zeros_like(y_ref)


def _expert_ffn(btab, xb, w_gate, w_up, w_down, layer):
    ch = ROW_CHUNKS
    rows, d = xb.shape[0] // ch, xb.shape[1] * ch
    de = w_gate.shape[3]
    bm = MOE_ROWS
    w_map = lambda i, bt: (layer, bt[i, 0], 0, 0)
    grid_spec = pltpu.PrefetchScalarGridSpec(
        num_scalar_prefetch=1,
        grid=(rows // bm,),
        in_specs=[pl.BlockSpec((bm * ch, LANES), lambda i, bt: (i, 0)),
                  pl.BlockSpec((1, 1, d, de), w_map),
                  pl.BlockSpec((1, 1, d, de), w_map),
                  pl.BlockSpec((1, 1, de, d), w_map)],
        out_specs=pl.BlockSpec((bm * ch, LANES), lambda i, bt: (i, 0)),
        scratch_shapes=[pltpu.VMEM((d, de), _BF16), pltpu.VMEM((d, de), _BF16), pltpu.VMEM((de, d), _BF16)],
    )
    return pl.pallas_call(
        _ffn_kernel,
        grid_spec=grid_spec,
        out_shape=jax.ShapeDtypeStruct((rows * ch, LANES), _F32),
        compiler_params=_cparams("arbitrary"),
        name="expert_ffn",
    )(btab, xb, w_gate, w_up, w_down)


def _combine_kernel(dest_ref, info_ref, x_ref, mod_ref, g_ref, b_ref, yb_ref, xo_ref, buf, sem, *, alpha,
                    tiles_per_batch):
    ch = ROW_CHUNKS
    tm = x_ref.shape[1]
    i, j = pl.program_id(0), pl.program_id(1)
    nj = pl.num_programs(1)
    step = i * nj + j
    tile = i * tiles_per_batch + j

    def issue(at_tile, slot):
        base = at_tile * tm

        for r in range(tm):
            for k in range(TOP_K):
                d = dest_ref[(base + r) * TOP_K + k]
                pltpu.make_async_copy(yb_ref.at[pl.ds(pl.multiple_of(d * ch, ch), ch)],
                                      buf.at[slot, k, pl.ds(r * ch, ch)], sem.at[slot]).start()

    @pl.when(step == 0)
    def _():
        issue(tile, 0)

    @pl.when(step + 1 < pl.num_programs(0) * nj)
    def _():
        issue(jnp.where(j + 1 < nj, tile + 1, (i + 1) * tiles_per_batch), (step + 1) % 2)

    slot = step % 2
    for k in range(TOP_K):
        _row_copy_wait(yb_ref.at[pl.ds(0, tm * ch)], buf.at[slot, k], sem.at[slot])
    info = info_ref[0]
    y = (info[:, 2:3] * _load_rows_chunked(buf, (slot, 0), tm, ch)
         + info[:, 3:4] * _load_rows_chunked(buf, (slot, 1), tm, ch))
    m = mod_ref[0]
    xo_ref[0] = _layer_norm(alpha * x_ref[0] + m[5:6, :] * y, g_ref[...], b_ref[...])


def _combine(dest, info, yb, x, mods, ln_g, ln_b, alpha, n_lat_tiles, latent_only):
    b, p, d = x.shape
    tm = ROW_TILE
    n_tiles = n_lat_tiles if latent_only else p // tm
    grid_spec = pltpu.PrefetchScalarGridSpec(
        num_scalar_prefetch=1,
        grid=(b, n_tiles),
        in_specs=[pl.BlockSpec((1, tm, LANES), lambda i, j, dest: (i, j, 0)),
                  pl.BlockSpec((1, tm, d), lambda i, j, dest: (i, j, 0)),
                  pl.BlockSpec((1, 6, d), lambda i, j, dest: (jnp.where(j < n_lat_tiles, i, b), 0, 0)),
                  pl.BlockSpec(ln_g.shape, lambda i, j, dest: (0, 0)),
                  pl.BlockSpec(ln_b.shape, lambda i, j, dest: (0, 0)),
                  pl.BlockSpec(memory_space=pl.ANY)],
        out_specs=pl.BlockSpec((1, tm, d), lambda i, j, dest: (i, j, 0)),
        scratch_shapes=[pltpu.VMEM((2, TOP_K, tm * ROW_CHUNKS, LANES), _F32), pltpu.SemaphoreType.DMA((2,))],
    )
    return pl.pallas_call(
        functools.partial(_combine_kernel, alpha=alpha, tiles_per_batch=p // tm),
        grid_spec=grid_spec,
        out_shape=jax.ShapeDtypeStruct((b, n_tiles * tm, d), _F32),
        compiler_params=_cparams("arbitrary", "arbitrary", disable_bounds_checks=True),
        name="moe_combine_ln",
    )(dest, info.reshape(b, p, LANES), x, mods, ln_g, ln_b, yb)


def _moe_layer(h, logits, x, mods, ln_g, ln_b, w_gate, w_up, w_down, layer, alpha, n_lat_tiles, latent_only):
    t = logits.shape[0]
    n_blocks = (t * TOP_K) // MOE_ROWS + N_EXPERTS
    info, btab = _route(logits, n_blocks)
    dest = info[:, :TOP_K].astype(jnp.int32).reshape(t * TOP_K)
    xb = _dispatch(dest, btab, h, n_blocks)
    yb = _expert_ffn(btab, xb, w_gate, w_up, w_down, layer)
    return _combine(dest, info, yb, x, mods, ln_g, ln_b, alpha, n_lat_tiles, latent_only)


def _rope_tables(s_len, l_ctx, rot_dim, lane_lo, period):
    rows = s_len // GRID_W
    row = jnp.repeat(jnp.arange(rows, dtype=_F32), GRID_W)
    col = jnp.tile(jnp.arange(GRID_W, dtype=_F32), rows)
    axis_dim = rot_dim // 2
    half = axis_dim // 2
    inv_freq = ROPE_THETA ** (-jnp.arange(0, axis_dim, 2, dtype=_F32) / axis_dim)
    lane = jnp.arange(LANES)
    rel = (lane % period) - lane_lo
    active = (rel >= 0) & (rel < rot_dim)
    rel = jnp.clip(rel, 0, rot_dim - 1)
    use_col = rel >= axis_dim
    f = inv_freq[(rel % axis_dim) % half]
    ang = jnp.where(use_col[None, :], col[:, None], row[:, None]) * f[None, :]
    sign = jnp.where((rel % axis_dim) < half, -1.0, 1.0)
    cos = jnp.where(active[None, :], jnp.cos(ang), 1.0)
    sin = jnp.where(active[None, :], jnp.sin(ang) * sign[None, :], 0.0)
    ident = jnp.ones((l_ctx, LANES), _F32)
    return (jnp.concatenate([cos, ident], axis=0), jnp.concatenate([sin, 0.0 * ident], axis=0))


def _win_weights(w_in):
    d = w_in.shape[0]
    nq, nkv = A_HEADS * A_HEAD_DIM, A_KV_HEADS * A_HEAD_DIM
    k = w_in[:, nq:nq + nkv].reshape(d, A_KV_HEADS, 1, A_HEAD_DIM)
    k_dup = jnp.concatenate([k, k], axis=2).reshape(d, 2 * nkv)
    return jnp.concatenate([w_in[:, :nq], k_dup, w_in[:, nq + nkv:]], axis=1).astype(_BF16)


def _mla_weights(w_in, w_uq, w_ukv):
    d = w_in.shape[0]
    pad = LANES - B_NOPE - B_ROPE
    kr = jnp.concatenate([jnp.zeros((d, B_NOPE), _F32), w_in[:, B_Q_RANK + B_KV_RANK:],
                          jnp.zeros((d, pad), _F32)], axis=1)
    w_in_p = jnp.concatenate([w_in[:, :B_Q_RANK + B_KV_RANK], kr], axis=1).astype(_BF16)
    uq = w_uq.reshape(B_Q_RANK, B_HEADS, B_NOPE + B_ROPE)
    uq = jnp.pad(uq, ((0, 0), (0, 0), (0, pad))).reshape(B_Q_RANK, B_HEADS * LANES).astype(_BF16)
    ukv = w_ukv.reshape(B_KV_RANK, B_HEADS, B_NOPE + B_V)
    uk = jnp.pad(ukv[:, :, :B_NOPE], ((0, 0), (0, 0), (0, LANES - B_NOPE)))
    uk = uk.reshape(B_KV_RANK, B_HEADS * LANES).astype(_BF16)
    uv = ukv[:, :, B_NOPE:].reshape(B_KV_RANK, B_HEADS * B_V).astype(_BF16)
    return w_in_p, uq, uk, uv


def kernel(x, c, ctx, c_ctx, ada_w, ada_b, ln_mix_g, ln_mix_b, ln_ffn_g, ln_ffn_b, win_w_in, win_w_out, win_sink, mla_w_in, mla_q_norm, mla_kv_norm, mla_w_uq, mla_w_ukv, mla_w_out, dif_w_in, dif_lambda, dif_subln, dif_w_out, moe_w_grp, moe_b_grp, moe_w_rt, moe_b_rt, moe_w_gate, moe_w_up, moe_w_down):
    b, s_len, d = x.shape
    l_ctx = ctx.shape[1]
    depth = ada_w.shape[0]
    p = s_len + l_ctx
    assert s_len % ROW_TILE == 0 and l_ctx % ROW_TILE == 0 and s_len % GRID_W == 0
    assert s_len % (ATTN_Q_TILE * Q_TILES_PER_STEP) == 0 and l_ctx == ATTN_Q_TILE and s_len % l_ctx == 0
    assert d == ROW_CHUNKS * LANES and s_len % KEY_CHUNK == 0 and MOE_ROWS & (MOE_ROWS - 1) == 0 and (b * p) % ROUTE_TILE == 0
    alpha = (2.0 * depth) ** 0.25
    n_lat_tiles = s_len // ROW_TILE

    xs = jnp.concatenate([x, ctx], axis=1)
    cs = jnp.concatenate([c, c_ctx[None, :]], axis=0)
    mods = _ada_mods(cs, ada_w, ada_b).reshape(depth, b + 1, 6, d)
    cos64, sin64 = _rope_tables(s_len, l_ctx, A_HEAD_DIM, 0, A_HEAD_DIM)
    cos32, sin32 = _rope_tables(s_len, l_ctx, B_ROPE, B_NOPE, LANES)
    row = lambda v: v.reshape(1, -1)

    for i in range(depth):
        kind, slot = i % N_MIXERS, i // N_MIXERS
        if kind == 0:
            qk, vt = _proj_rope(xs, mods[i], _win_weights(win_w_in[slot]), cos64, sin64,
                                (A_HEADS + 2 * A_KV_HEADS) * A_HEAD_DIM, A_KV_HEADS * A_HEAD_DIM, n_lat_tiles,
                                n_q=A_HEADS * A_HEAD_DIM, q_scale=A_HEAD_DIM ** -0.5 * LOG2E)
            o = _attn_win(qk, vt, win_sink[slot], s_len, l_ctx)
            w_out = win_w_out[slot]
        elif kind == 1:
            w_in_p, uq, uk, uv = _mla_weights(mla_w_in[slot], mla_w_uq[slot], mla_w_ukv[slot])
            q, k, vt = _proj_mla(xs, mods[i], w_in_p, row(mla_q_norm[slot]), row(mla_kv_norm[slot]),
                                uq, uk, uv, cos32, sin32, n_lat_tiles)
            o = _attn_mla(q, k, vt, s_len)
            w_out = mla_w_out[slot]
        else:
            lambda_init = 0.8 - 0.6 * math.exp(-0.3 * i)
            qk, vt = _proj_rope(xs, mods[i], dif_w_in[slot].astype(_BF16), cos64, sin64,
                                2 * C_HEADS * 2 * C_HEAD_DIM, C_HEADS * C_V_DIM, n_lat_tiles,
                                n_q=C_HEADS * 2 * C_HEAD_DIM, q_scale=C_HEAD_DIM ** -0.5 * LOG2E)
            o = _attn_diff(qk, vt, dif_lambda[slot], dif_subln[slot], s_len, lambda_init)
            w_out = dif_w_out[slot]

        w_router = jnp.concatenate([moe_w_grp[i], moe_w_rt[i],
                                    jnp.zeros((d, LANES - N_GROUPS - N_EXPERTS), _F32)], axis=1)
        b_router = jnp.concatenate([moe_b_grp[i], moe_b_rt[i],
                                    jnp.zeros((LANES - N_GROUPS - N_EXPERTS,), _F32)])[None, :]
        w_hi = w_router.astype(_BF16)
        w_lo = (w_router - w_hi.astype(_F32)).astype(_BF16)
        xs, h, logits = _out_proj(o, w_out.astype(_BF16), xs, mods[i], row(ln_mix_g[i]), row(ln_mix_b[i]),
                                  jnp.concatenate([w_hi, w_lo], axis=1), b_router, alpha, n_lat_tiles)
        xs = _moe_layer(h.reshape(b * p * ROW_CHUNKS, LANES), logits.reshape(b * p, LANES), xs, mods[i], row(ln_ffn_g[i]),
                        row(ln_ffn_b[i]), moe_w_gate, moe_w_up, moe_w_down, i, alpha, n_lat_tiles,
                        latent_only=i == depth - 1)
    return xs
```

```python
import functools
import math

import jax
import jax.numpy as jnp
from jax import lax
from jax.experimental import pallas as pl
from jax.experimental.pallas import tpu as pltpu

LANES = 128
GRID_W = 64
ROPE_THETA = 10000.0
LN_EPS = 1e-5
RMS_EPS = 1e-6
LOG2E = math.log2(math.e)

A_HEADS, A_KV_HEADS, A_HEAD_DIM, A_WINDOW = 16, 4, 64, 128
A_GROUP = A_HEADS // A_KV_HEADS
B_HEADS, B_NOPE, B_ROPE, B_V, B_Q_RANK, B_KV_RANK = 16, 64, 32, 64, 512, 256
C_HEADS, C_HEAD_DIM = 8, 64
C_V_DIM = 2 * C_HEAD_DIM
N_GROUPS, EXPERTS_PER_GROUP, TOP_K = 4, 8, 2
N_EXPERTS = N_GROUPS * EXPERTS_PER_GROUP
N_MIXERS = 3

ROW_TILE = 256
ATTN_Q_TILE = 256
Q_TILES_PER_STEP = 2
WIN_BLOCK = 128
KEY_CHUNK = 1024
MOE_ROWS = 256
ROUTE_TILE = 512
ROW_CHUNKS = 8
VMEM_LIMIT = 56 * 1024 * 1024

_F32 = jnp.float32
_BF16 = jnp.bfloat16
_NT = (((1,), (1,)), ((), ()))


def _cparams(*sem, **kw):
    return pltpu.CompilerParams(dimension_semantics=sem, vmem_limit_bytes=VMEM_LIMIT, **kw)


def _lane_iota(shape):
    return lax.broadcasted_iota(jnp.int32, shape, len(shape) - 1)


def _rope(a, cos, sin_signed, half):
    lane = _lane_iota(a.shape)
    swapped = jnp.where(lane % (2 * half) < half,
                        pltpu.roll(a, a.shape[-1] - half, axis=1),
                        pltpu.roll(a, half, axis=1))
    return a * cos + swapped * sin_signed


def _store_rows_chunked(ref, lead, val):
    n, d = val.shape
    c = d // LANES
    for k in range(c):
        ref[lead + (pl.ds(k, n, stride=c), slice(None))] = val[:, k * LANES:(k + 1) * LANES]


def _load_rows_chunked(ref, lead, n, c):
    return jnp.concatenate([ref[lead + (pl.ds(k, n, stride=c), slice(None))] for k in range(c)], axis=1)


def _modulated(x_ref, mod_ref, shift_row):
    m = mod_ref[0]
    return x_ref[0] * (1.0 + m[shift_row + 1:shift_row + 2, :]) + m[shift_row:shift_row + 1, :]


def _ada_kernel(c_ref, w_ref, b_ref, o_ref):
    c = c_ref[...]
    s = c * jax.nn.sigmoid(c)
    o_ref[0] = jnp.dot(s, w_ref[0], preferred_element_type=_F32,
                       precision=lax.Precision.HIGHEST) + b_ref[0]


def _ada_mods(cs, ada_w, ada_b):
    depth, d, n = ada_w.shape
    r = cs.shape[0]
    tn = 1024
    return pl.pallas_call(
        _ada_kernel,
        grid=(depth, n // tn),
        in_specs=[pl.BlockSpec((r, d), lambda i, j: (0, 0)),
                  pl.BlockSpec((1, d, tn), lambda i, j: (i, 0, j)),
                  pl.BlockSpec((1, 1, tn), lambda i, j: (i, 0, j))],
        out_specs=pl.BlockSpec((1, r, tn), lambda i, j: (i, 0, j)),
        out_shape=jax.ShapeDtypeStruct((depth, r, n), _F32),
        compiler_params=_cparams("parallel", "parallel"),
        name="ada_mods",
    )(cs, ada_w, ada_b.reshape(depth, 1, n))


def _row_specs(b, p, d, n_lat_tiles, tm):
    x_spec = pl.BlockSpec((1, tm, d), lambda i, j: (i, j, 0))
    mod_spec = pl.BlockSpec((1, 6, d), lambda i, j: (jnp.where(j < n_lat_tiles, i, b), 0, 0))
    return x_spec, mod_spec


def _proj_rope_kernel(x_ref, mod_ref, w_ref, cos_ref, sin_ref, o_ref, *vt_ref, n_rope, n_q, q_scale):
    h = _modulated(x_ref, mod_ref, 0).astype(_BF16)
    acc = jnp.dot(h, w_ref[...], preferred_element_type=_F32)
    cos, sin = cos_ref[...], sin_ref[...]
    n = o_ref.shape[2]
    for j in range(n // LANES):
        a = acc[:, j * LANES:(j + 1) * LANES]
        if j * LANES < n_rope:
            a = _rope(a, cos, sin, 16)
        if j * LANES < n_q and q_scale != 1.0:
            a = a * q_scale
        o_ref[0, :, j * LANES:(j + 1) * LANES] = a.astype(_BF16)
    if vt_ref:
        vt_ref[0][0] = acc[:, n:].T.astype(_BF16)


def _proj_rope(x, mods, w, cos, sin, n_rope, n_t, n_lat_tiles, n_q=0, q_scale=1.0):
    b, p, d = x.shape
    n_all = w.shape[1]
    n = n_all - n_t
    tm = ROW_TILE
    x_spec, mod_spec = _row_specs(b, p, d, n_lat_tiles, tm)
    out_specs = [pl.BlockSpec((1, tm, n), lambda i, j: (i, j, 0))]
    out_shape = [jax.ShapeDtypeStruct((b, p, n), _BF16)]
    if n_t:
        out_specs.append(pl.BlockSpec((1, n_t, tm), lambda i, j: (i, 0, j)))
        out_shape.append(jax.ShapeDtypeStruct((b, n_t, p), _BF16))
    return pl.pallas_call(
        functools.partial(_proj_rope_kernel, n_rope=n_rope, n_q=n_q, q_scale=q_scale),
        grid=(b, p // tm),
        in_specs=[x_spec, mod_spec,
                  pl.BlockSpec((d, n_all), lambda i, j: (0, 0)),
                  pl.BlockSpec((tm, LANES), lambda i, j: (j, 0)),
                  pl.BlockSpec((tm, LANES), lambda i, j: (j, 0))],
        out_specs=out_specs,
        out_shape=out_shape,
        compiler_params=_cparams("parallel", "parallel"),
        name="proj_rope",
    )(x, mods, w, cos, sin)


def _rms(x, g):
    return x * lax.rsqrt(jnp.mean(x * x, axis=-1, keepdims=True) + RMS_EPS) * g


def _proj_mla_kernel(x_ref, mod_ref, win_ref, qn_ref, kvn_ref, wuq_ref, wuk_ref, wv_ref,
                     cos_ref, sin_ref, q_ref, k_ref, vt_ref, *, scale):
    h = _modulated(x_ref, mod_ref, 0).astype(_BF16)
    c = jnp.dot(h, win_ref[...], preferred_element_type=_F32)
    cq = _rms(c[:, :B_Q_RANK], qn_ref[...]).astype(_BF16)
    ckv = _rms(c[:, B_Q_RANK:B_Q_RANK + B_KV_RANK], kvn_ref[...]).astype(_BF16)
    cos, sin = cos_ref[...], sin_ref[...]
    k_rope = _rope(c[:, B_Q_RANK + B_KV_RANK:], cos, sin, 8)
    q = jnp.dot(cq, wuq_ref[...], preferred_element_type=_F32)
    k_nope = jnp.dot(ckv, wuk_ref[...], preferred_element_type=_F32)
    for hd in range(B_HEADS):
        sl = slice(hd * LANES, (hd + 1) * LANES)
        q_ref[0, :, sl] = (_rope(q[:, sl], cos, sin, 8) * scale).astype(_BF16)
        k_ref[0, :, sl] = (k_nope[:, sl] + k_rope).astype(_BF16)
    vt_ref[0] = jnp.dot(ckv, wv_ref[...], preferred_element_type=_F32).T.astype(_BF16)


def _proj_mla(x, mods, w_in, q_norm, kv_norm, w_uq, w_uk, w_v, cos, sin, n_lat_tiles):
    b, p, d = x.shape
    tm = ROW_TILE
    x_spec, mod_spec = _row_specs(b, p, d, n_lat_tiles, tm)
    full = lambda a: pl.BlockSpec(a.shape, lambda i, j: (0,) * a.ndim)
    nq, nv = w_uq.shape[1], w_v.shape[1]
    tile = lambda n: pl.BlockSpec((1, tm, n), lambda i, j: (i, j, 0))
    return pl.pallas_call(
        functools.partial(_proj_mla_kernel, scale=(B_NOPE + B_ROPE) ** -0.5 * LOG2E),
        grid=(b, p // tm),
        in_specs=[x_spec, mod_spec, full(w_in), full(q_norm), full(kv_norm), full(w_uq), full(w_uk),
                  full(w_v),
                  pl.BlockSpec((tm, LANES), lambda i, j: (j, 0)),
                  pl.BlockSpec((tm, LANES), lambda i, j: (j, 0))],
        out_specs=[tile(nq), tile(nq), pl.BlockSpec((1, nv, tm), lambda i, j: (i, 0, j))],
        out_shape=[jax.ShapeDtypeStruct((b, p, nq), _BF16), jax.ShapeDtypeStruct((b, p, nq), _BF16),
                   jax.ShapeDtypeStruct((b, nv, p), _BF16)],
        compiler_params=_cparams("parallel", "parallel"),
        name="proj_mla",
    )(x, mods, w_in, q_norm, kv_norm, w_uq, w_uk, w_v, cos, sin)


def _attn_win_kernel(sink_ref, q_ref, k0_ref, k1_ref, k2_ref, kc_ref, vt0_ref, vt1_ref, vt2_ref, vtc_ref,
                     o_ref, *, n_lat_tiles):
    j = pl.program_id(1)
    w = WIN_BLOCK
    n_ctx = kc_ref.shape[1]
    n_keys = 3 * w + n_ctx
    nq = A_GROUP * w
    row = lax.broadcasted_iota(jnp.int32, (n_keys, nq), 0)
    q_pos = j * w + lax.broadcasted_iota(jnp.int32, (n_keys, nq), 1) % w
    k_pos = (j - 1) * w + row
    n_latent_keys = jnp.where(j < n_lat_tiles, n_lat_tiles * w, 0)
    valid = (row >= 3 * w) | ((jnp.abs(q_pos - k_pos) <= A_WINDOW) & (k_pos >= 0) & (k_pos < n_latent_keys))
    low = _lane_iota((w, LANES)) < A_HEAD_DIM

    def scores(kvh):
        cs = slice(kvh * LANES, (kvh + 1) * LANES)
        kk = jnp.concatenate([k0_ref[0, :, cs], k1_ref[0, :, cs], k2_ref[0, :, cs], kc_ref[0, :, cs]], axis=0)
        qa = q_ref[0, :, 2 * kvh * LANES:(2 * kvh + 1) * LANES]
        qb = q_ref[0, :, (2 * kvh + 1) * LANES:(2 * kvh + 2) * LANES]
        zero = jnp.zeros_like(qa)
        q4 = jnp.concatenate([jnp.where(low, qa, zero), jnp.where(low, zero, qa),
                              jnp.where(low, qb, zero), jnp.where(low, zero, qb)], axis=0)
        return lax.dot_general(kk, q4, _NT, preferred_element_type=_F32)

    outs = []
    s_next = scores(0)
    for kvh in range(A_KV_HEADS):
        s = s_next
        if kvh + 1 < A_KV_HEADS:
            s_next = scores(kvh + 1)
        rs = slice(kvh * A_HEAD_DIM, (kvh + 1) * A_HEAD_DIM)
        vt = jnp.concatenate([vt0_ref[0, rs, :], vt1_ref[0, rs, :], vt2_ref[0, rs, :], vtc_ref[0, rs, :]], axis=1)
        sink = jnp.concatenate([jnp.full((1, w), sink_ref[kvh * A_GROUP + g] * LOG2E, _F32)
                                for g in range(A_GROUP)], axis=1)
        s = jnp.where(valid, s, -jnp.inf)
        m = jnp.maximum(jnp.max(s, axis=0, keepdims=True), sink)
        e = jnp.exp2(s - m)
        denom = jnp.sum(e, axis=0, keepdims=True) + jnp.exp2(sink - m)
        o = jnp.dot(vt, e.astype(_BF16), preferred_element_type=_F32) / denom
        outs += [o[:, g * w:(g + 1) * w] for g in range(A_GROUP)]
    o_ref[0] = jnp.concatenate(outs, axis=0).T.astype(_BF16)


def _attn_win(qk, vt, sink, s_len, l_ctx):
    b, p, _ = qk.shape
    w = WIN_BLOCK
    n_lat = s_len // w
    nq = A_HEADS * A_HEAD_DIM
    kw = A_KV_HEADS * LANES
    nv = A_KV_HEADS * A_HEAD_DIM
    ctx_blk = s_len // l_ctx

    def nb(off):
        return lambda i, j: (i, jnp.clip(j + off, 0, n_lat - 1), nq // kw)

    def nbt(off):
        return lambda i, j: (i, 0, jnp.clip(j + off, 0, n_lat - 1))

    return pl.pallas_call(
        functools.partial(_attn_win_kernel, n_lat_tiles=n_lat),
        grid=(b, p // w),
        in_specs=[pl.BlockSpec(memory_space=pltpu.SMEM),
                  pl.BlockSpec((1, w, nq), lambda i, j: (i, j, 0)),
                  pl.BlockSpec((1, w, kw), nb(-1)), pl.BlockSpec((1, w, kw), nb(0)),
                  pl.BlockSpec((1, w, kw), nb(1)),
                  pl.BlockSpec((1, l_ctx, kw), lambda i, j: (i, ctx_blk, nq // kw)),
                  pl.BlockSpec((1, nv, w), nbt(-1)), pl.BlockSpec((1, nv, w), nbt(0)),
                  pl.BlockSpec((1, nv, w), nbt(1)),
                  pl.BlockSpec((1, nv, l_ctx), lambda i, j: (i, 0, ctx_blk))],
        out_specs=pl.BlockSpec((1, w, nq), lambda i, j: (i, j, 0)),
        out_shape=jax.ShapeDtypeStruct((b, p, nq), _BF16),
        compiler_params=_cparams("parallel", "parallel"),
        name="attn_win",
    )(sink, qk, qk, qk, qk, qk, vt, vt, vt, vt)


def _key_chunks(s_len, l_ctx, latent_query):
    chunks = [(lo, KEY_CHUNK) for lo in range(0, s_len, KEY_CHUNK)] if latent_query else []
    return chunks + [(s_len, l_ctx)]


def _flash_keys_major(q, k_chunk, vt_chunk, chunks):
    def scores(c):
        return lax.dot_general(k_chunk(*chunks[c]), q, _NT, preferred_element_type=_F32)

    m = acc = l = None
    pending = None
    s_next = scores(0)

    def flush(acc):
        (lo, n), e, corr = pending
        pv = jnp.dot(vt_chunk(lo, n), e, preferred_element_type=_F32)
        return pv if acc is None else acc * corr + pv

    for c in range(len(chunks)):
        s = s_next
        if c + 1 < len(chunks):
            s_next = scores(c + 1)
        if pending is not None:
            acc = flush(acc)
        m_chunk = jnp.max(s, axis=0, keepdims=True)
        m_new = m_chunk if m is None else jnp.maximum(m, m_chunk)
        e = jnp.exp2(s - m_new)
        l_chunk = jnp.sum(e, axis=0, keepdims=True)
        corr = None if m is None else jnp.exp2(m - m_new)
        l = l_chunk if m is None else l * corr + l_chunk
        pending = (chunks[c], e.astype(_BF16), corr)
        m = m_new
    return flush(acc), l


def _per_step_tiles(tile, n_lat_steps):
    j = pl.program_id(2)

    @pl.when(j < n_lat_steps)
    def _():
        for t in range(Q_TILES_PER_STEP):
            tile(t, True)

    @pl.when(j >= n_lat_steps)
    def _():
        tile(0, False)


def _attn_mla_kernel(q_ref, k_ref, vt_ref, o_ref, *, n_lat_steps, s_len, l_ctx):
    tq = ATTN_Q_TILE

    def tile(t, latent_query):
        rows = slice(t * tq, (t + 1) * tq)
        q = q_ref[0, rows, :]
        first = _lane_iota(q.shape) < LANES
        zero = jnp.zeros_like(q)
        q2 = jnp.concatenate([jnp.where(first, q, zero), jnp.where(first, zero, q)], axis=0)
        acc, l = _flash_keys_major(q2, lambda lo, n: k_ref[0, lo:lo + n, :],
                                   lambda lo, n: vt_ref[0, :, lo:lo + n],
                                   _key_chunks(s_len, l_ctx, latent_query))
        o2 = acc / l
        o = jnp.concatenate([o2[:B_V, :tq], o2[B_V:, tq:]], axis=0)
        o_ref[0, rows, :] = o.T.astype(_BF16)

    _per_step_tiles(tile, n_lat_steps)


def _attn_mla(q, k, vt, s_len):
    b, p, _ = q.shape
    tq = ATTN_Q_TILE * Q_TILES_PER_STEP
    pairs = B_HEADS // 2
    return pl.pallas_call(
        functools.partial(_attn_mla_kernel, n_lat_steps=s_len // tq, s_len=s_len, l_ctx=p - s_len),
        grid=(b, pairs, pl.cdiv(p, tq)),
        in_specs=[pl.BlockSpec((1, tq, 2 * LANES), lambda i, c, j: (i, j, c)),
                  pl.BlockSpec((1, p, 2 * LANES), lambda i, c, j: (i, 0, c)),
                  pl.BlockSpec((1, 2 * B_V, p), lambda i, c, j: (i, c, 0))],
        out_specs=pl.BlockSpec((1, tq, 2 * B_V), lambda i, c, j: (i, j, c)),
        out_shape=jax.ShapeDtypeStruct((b, p, B_HEADS * B_V), _BF16),
        compiler_params=_cparams("parallel", "parallel", "parallel"),
        name="attn_mla",
    )(q, k, vt)


def _attn_diff_kernel(lam_ref, subln_ref, q_ref, k_ref, vt_ref, o_ref, *, n_lat_steps, s_len, l_ctx,
                      lambda_init):
    tq = ATTN_Q_TILE
    lp = lam_ref[...]
    lam = (jnp.exp(jnp.sum(lp[0:1] * lp[1:2], axis=-1, keepdims=True))
           - jnp.exp(jnp.sum(lp[2:3] * lp[3:4], axis=-1, keepdims=True)) + lambda_init)

    def tile(t, latent_query):
        rows = slice(t * tq, (t + 1) * tq)
        q = q_ref[0, rows, :]
        low = _lane_iota(q.shape) < C_HEAD_DIM
        zero = jnp.zeros_like(q)
        q2 = jnp.concatenate([jnp.where(low, q, zero), jnp.where(low, zero, q)], axis=0)
        acc, l = _flash_keys_major(q2, lambda lo, n: k_ref[0, lo:lo + n, :],
                                   lambda lo, n: vt_ref[0, :, lo:lo + n],
                                   _key_chunks(s_len, l_ctx, latent_query))
        o2 = acc / l
        o = o2[:, :tq] - lam * o2[:, tq:]
        o = o * lax.rsqrt(jnp.mean(o * o, axis=0, keepdims=True) + RMS_EPS) * subln_ref[...]
        o_ref[0, rows, :] = (o * (1.0 - lambda_init)).T.astype(_BF16)

    _per_step_tiles(tile, n_lat_steps)


def _attn_diff(qk, vt, lam_params, subln, s_len, lambda_init):
    b, p, _ = qk.shape
    tq = ATTN_Q_TILE * Q_TILES_PER_STEP
    return pl.pallas_call(
        functools.partial(_attn_diff_kernel, n_lat_steps=s_len // tq, s_len=s_len, l_ctx=p - s_len,
                          lambda_init=lambda_init),
        grid=(b, C_HEADS, pl.cdiv(p, tq)),
        in_specs=[pl.BlockSpec(lam_params.shape, lambda i, h, j: (0, 0)),
                  pl.BlockSpec((C_V_DIM, 1), lambda i, h, j: (0, 0)),
                  pl.BlockSpec((1, tq, LANES), lambda i, h, j: (i, j, h)),
                  pl.BlockSpec((1, p, LANES), lambda i, h, j: (i, 0, C_HEADS + h)),
                  pl.BlockSpec((1, C_V_DIM, p), lambda i, h, j: (i, h, 0))],
        out_specs=pl.BlockSpec((1, tq, LANES), lambda i, h, j: (i, j, h)),
        out_shape=jax.ShapeDtypeStruct((b, p, C_HEADS * C_V_DIM), _BF16),
        compiler_params=_cparams("parallel", "parallel", "parallel"),
        name="attn_diff",
    )(lam_params, subln.reshape(C_V_DIM, 1), qk, qk, vt)


def _layer_norm(z, g, b):
    mu = jnp.mean(z, axis=-1, keepdims=True)
    zc = z - mu
    var = jnp.mean(zc * zc, axis=-1, keepdims=True)
    return zc * lax.rsqrt(var + LN_EPS) * g + b


def _out_proj_kernel(o_ref, w_ref, x_ref, mod_ref, g_ref, b_ref, wr_ref, br_ref,
                     xo_ref, h_ref, lg_ref, *, alpha):
    y = jnp.dot(o_ref[0], w_ref[...], preferred_element_type=_F32)
    m = mod_ref[0]
    x = _layer_norm(alpha * x_ref[0] + m[2:3, :] * y, g_ref[...], b_ref[...])
    xo_ref[0] = x
    h = x * (1.0 + m[4:5, :]) + m[3:4, :]
    _store_rows_chunked(h_ref, (0,), h)
    tm = h.shape[0]
    h_hi = h.astype(_BF16)
    h_lo = (h - h_hi.astype(_F32)).astype(_BF16)
    t = jnp.dot(jnp.concatenate([h_hi, h_lo], axis=0), wr_ref[...], preferred_element_type=_F32)
    lg_ref[0] = t[:tm, :LANES] + t[tm:, :LANES] + t[:tm, LANES:] + br_ref[...]


def _out_proj(o, w_out, x, mods, ln_g, ln_b, w_router, b_router, alpha, n_lat_tiles):
    b, p, d = x.shape
    tm = ROW_TILE
    x_spec, mod_spec = _row_specs(b, p, d, n_lat_tiles, tm)
    full = lambda a: pl.BlockSpec(a.shape, lambda i, j: (0,) * a.ndim)
    n_o = o.shape[2]
    return pl.pallas_call(
        functools.partial(_out_proj_kernel, alpha=alpha),
        grid=(b, p // tm),
        in_specs=[pl.BlockSpec((1, tm, n_o), lambda i, j: (i, j, 0)), full(w_out), x_spec, mod_spec,
                  full(ln_g), full(ln_b), full(w_router), full(b_router)],
        out_specs=[x_spec, pl.BlockSpec((1, tm * (d // LANES), LANES), lambda i, j: (i, j, 0)),
                   pl.BlockSpec((1, tm, LANES), lambda i, j: (i, j, 0))],
        out_shape=[jax.ShapeDtypeStruct((b, p, d), _F32), jax.ShapeDtypeStruct((b, p * (d // LANES), LANES), _F32),
                   jax.ShapeDtypeStruct((b, p, LANES), _F32)],
        compiler_params=_cparams("parallel", "parallel"),
        name="out_proj_ln",
    )(o, w_out, x, mods, ln_g, ln_b, w_router, b_router)


def _route_kernel(lg_ref, info_ref, btab_ref, sel_ref, cnt_ref, start_ref, run_ref, *, block_rows):
    sweep, i = pl.program_id(0), pl.program_id(1)
    tm = lg_ref.shape[0]
    rows = pl.ds(pl.multiple_of(i * tm, tm), tm)
    lane = _lane_iota((tm, LANES))

    @pl.when((sweep == 0) & (i == 0))
    def _():
        cnt_ref[...] = jnp.zeros_like(cnt_ref)

    @pl.when(sweep == 0)
    def _():
        x = lg_ref[...]
        xg = jnp.where(lane < N_GROUPS, x, -jnp.inf)
        g_max = jnp.max(xg, axis=-1, keepdims=True)
        g_sel = jnp.min(jnp.where(xg == g_max, lane, LANES), axis=-1, keepdims=True)
        p_g = 1.0 / jnp.sum(jnp.exp(xg - g_max), axis=-1, keepdims=True)
        lo = N_GROUPS + g_sel * EXPERTS_PER_GROUP
        xe = jnp.where((lane >= lo) & (lane < lo + EXPERTS_PER_GROUP), x, -jnp.inf)
        v1 = jnp.max(xe, axis=-1, keepdims=True)
        i1 = jnp.min(jnp.where(xe == v1, lane, LANES), axis=-1, keepdims=True)
        xe = jnp.where(lane == i1, -jnp.inf, xe)
        v2 = jnp.max(xe, axis=-1, keepdims=True)
        i2 = jnp.min(jnp.where(xe == v2, lane, LANES), axis=-1, keepdims=True)
        t = jnp.exp(v2 - v1)
        w1 = p_g / (1.0 + t)
        sel_ref[rows, :] = jnp.where(lane == 0, i1.astype(_F32), jnp.where(lane == 1, i2.astype(_F32),
                                     jnp.where(lane == 2, w1, jnp.where(lane == 3, w1 * t, 0.0))))
        onehot = ((lane == i1) | (lane == i2)).astype(_F32)
        cnt_ref[...] += jnp.sum(onehot, axis=0, keepdims=True)

    @pl.when((sweep == 1) & (i == 0))
    def _():
        cnt = cnt_ref[...]
        padded = jnp.floor((cnt + (block_rows - 1)) * (1.0 / block_rows)) * block_rows
        r = lax.broadcasted_iota(jnp.int32, (LANES, LANES), 0)
        c = lax.broadcasted_iota(jnp.int32, (LANES, LANES), 1)
        before = (r < c).astype(_F32)
        start = jnp.dot(jnp.broadcast_to(padded, (8, LANES)), before, preferred_element_type=_F32,
                        precision=lax.Precision.HIGHEST)[0:1]
        start_ref[...] = start
        run_ref[...] = jnp.zeros_like(run_ref)
        nb = btab_ref.shape[0]
        row0 = (lax.broadcasted_iota(jnp.int32, (nb, LANES), 0) * block_rows).astype(_F32)
        is_e = (_lane_iota((nb, LANES)) >= N_GROUPS) & (_lane_iota((nb, LANES)) < N_GROUPS + N_EXPERTS)
        end = start + padded
        owner = jnp.sum(jnp.where(is_e & (end <= row0), 1.0, 0.0), axis=-1, keepdims=True)
        inside = is_e & (start <= row0) & (row0 < end)
        real = jnp.sum(jnp.where(inside, jnp.clip(cnt - (row0 - start), 0.0, block_rows), 0.0),
                       axis=-1, keepdims=True)
        bl = _lane_iota((nb, LANES))
        btab_ref[...] = jnp.where(bl == 0, jnp.minimum(owner, N_EXPERTS - 1.0),
                                  jnp.where(bl == 1, real, 0.0)).astype(jnp.int32)

    @pl.when(sweep == 1)
    def _():
        sel = sel_ref[rows, :]
        hit1 = lane == sel[:, 0:1].astype(jnp.int32)
        hit2 = lane == sel[:, 1:2].astype(jnp.int32)
        onehot = (hit1 | hit2).astype(_F32)
        row = lax.broadcasted_iota(jnp.int32, (tm, tm), 0)
        col = lax.broadcasted_iota(jnp.int32, (tm, tm), 1)
        earlier = (col < row).astype(_BF16)
        rank = jnp.dot(earlier, onehot.astype(_BF16), preferred_element_type=_F32) + run_ref[...]
        pos = rank + start_ref[...]
        d1 = jnp.sum(jnp.where(hit1, pos, 0.0), axis=-1, keepdims=True)
        d2 = jnp.sum(jnp.where(hit2, pos, 0.0), axis=-1, keepdims=True)
        run_ref[...] += jnp.sum(onehot, axis=0, keepdims=True)
        info_ref[...] = jnp.where(lane == 0, d1, jnp.where(lane == 1, d2, sel))


def _route(logits, n_blocks):
    t = logits.shape[0]
    tm = ROUTE_TILE
    nb_pad = -(-n_blocks // 8) * 8
    return pl.pallas_call(
        functools.partial(_route_kernel, block_rows=MOE_ROWS),
        grid=(2, t // tm),
        in_specs=[pl.BlockSpec((tm, LANES), lambda s, i: (i * (1 - s), 0))],
        out_specs=[pl.BlockSpec((tm, LANES), lambda s, i: (i * s, 0)),
                   pl.BlockSpec((nb_pad, LANES), lambda s, i: (0, 0))],
        out_shape=[jax.ShapeDtypeStruct((t, LANES), _F32), jax.ShapeDtypeStruct((nb_pad, LANES), jnp.int32)],
        scratch_shapes=[pltpu.VMEM((t, LANES), _F32)] + [pltpu.VMEM((1, LANES), _F32)] * 3,
        compiler_params=_cparams("arbitrary", "arbitrary"),
        name="moe_route",
    )(logits)


def _row_copy_wait(src_rows, dst_rows, sem):
    pltpu.make_async_copy(src_rows, dst_rows, sem).wait()


def _dispatch_kernel(dest_ref, btab_ref, h_ref, xb_ref, zeros, sem, zero_sem, *, n_blocks):
    ch = ROW_CHUNKS
    tm = h_ref.shape[0] // ch
    bm = MOE_ROWS
    base = pl.program_id(0) * tm

    def tail_rows(blk):
        real = btab_ref[blk, 1]
        return jnp.where(real > 0, bm - real, 0)

    def tail_copy(row):
        return pltpu.make_async_copy(zeros.at[pl.ds(0, ch)], xb_ref.at[pl.ds(pl.multiple_of(row * ch, ch), ch)],
                                     zero_sem)

    def block_copy(blk):
        return pltpu.make_async_copy(zeros, xb_ref.at[pl.ds(blk * bm * ch, bm * ch)], zero_sem)

    @pl.when(pl.program_id(0) == 0)
    def _():
        zeros[...] = jnp.zeros_like(zeros)
        for blk in range(n_blocks):
            first = blk * bm + btab_ref[blk, 1]
            lax.fori_loop(0, tail_rows(blk), lambda r, c: (tail_copy(first + r).start(), c)[1], 0)
            pl.when(btab_ref[blk, 1] == 0)(lambda: block_copy(blk).start())

    for r in range(tm):
        for k in range(TOP_K):
            d = dest_ref[(base + r) * TOP_K + k]
            pltpu.make_async_copy(h_ref.at[pl.ds(r * ch, ch)], xb_ref.at[pl.ds(pl.multiple_of(d * ch, ch), ch)],
                                  sem).start(priority=(r * TOP_K + k) % 2)
    for k in range(TOP_K):
        _row_copy_wait(h_ref, xb_ref.at[pl.ds(0, tm * ch)], sem)

    @pl.when(pl.program_id(0) == pl.num_programs(0) - 1)
    def _():
        for blk in range(n_blocks):
            lax.fori_loop(0, tail_rows(blk), lambda r, c: (tail_copy(0).wait(), c)[1], 0)
            pl.when(btab_ref[blk, 1] == 0)(lambda: block_copy(blk).wait())


def _dispatch(dest, btab, h, n_blocks):
    ch = ROW_CHUNKS
    t, d = h.shape[0] // ch, h.shape[1]
    tm = ROW_TILE
    grid_spec = pltpu.PrefetchScalarGridSpec(
        num_scalar_prefetch=2,
        grid=(t // tm,),
        in_specs=[pl.BlockSpec((tm * ch, d), lambda i, dest, bt: (i, 0))],
        out_specs=pl.BlockSpec(memory_space=pl.ANY),
        scratch_shapes=[pltpu.VMEM((MOE_ROWS * ch, d), h.dtype), pltpu.SemaphoreType.DMA(()),
                        pltpu.SemaphoreType.DMA(())],
    )
    return pl.pallas_call(
        functools.partial(_dispatch_kernel, n_blocks=n_blocks),
        grid_spec=grid_spec,
        out_shape=jax.ShapeDtypeStruct((n_blocks * MOE_ROWS * ch, d), h.dtype),
        compiler_params=_cparams("arbitrary", disable_bounds_checks=True),
        name="moe_dispatch",
    )(dest, btab, h)


def _ffn_kernel(btab_ref, x_ref, wg_ref, wu_ref, wd_ref, y_ref, wg_bf, wu_bf, wd_bf):
    i = pl.program_id(0)
    expert, real = btab_ref[i, 0], btab_ref[i, 1]
    prev = btab_ref[jnp.maximum(i - 1, 0), 0]

    @pl.when((i == 0) | (expert != prev))
    def _():
        wg_bf[...] = wg_ref[0, 0].astype(_BF16)
        wu_bf[...] = wu_ref[0, 0].astype(_BF16)
        wd_bf[...] = wd_ref[0, 0].astype(_BF16)

    @pl.when(real > 0)
    def _():
        x = _load_rows_chunked(x_ref, (), MOE_ROWS, ROW_CHUNKS).astype(_BF16)
        g = jnp.dot(x, wg_bf[...], preferred_element_type=_F32)
        u = jnp.dot(x, wu_bf[...], preferred_element_type=_F32)
        a = (g * jax.nn.sigmoid(g) * u).astype(_BF16)
        _store_rows_chunked(y_ref, (), jnp.dot(a, wd_bf[...], preferred_element_type=_F32))

    @pl.when(real <= 0)
    def _():
        y_ref[...] = jnp.zeros_like(y_ref)


def _expert_ffn(btab, xb, w_gate, w_up, w_down, layer):
    ch = ROW_CHUNKS
    rows, d = xb.shape[0] // ch, xb.shape[1] * ch
    de = w_gate.shape[3]
    bm = MOE_ROWS
    w_map = lambda i, bt: (layer, bt[i, 0], 0, 0)
    grid_spec = pltpu.PrefetchScalarGridSpec(
        num_scalar_prefetch=1,
        grid=(rows // bm,),
        in_specs=[pl.BlockSpec((bm * ch, LANES), lambda i, bt: (i, 0)),
                  pl.BlockSpec((1, 1, d, de), w_map),
                  pl.BlockSpec((1, 1, d, de), w_map),
                  pl.BlockSpec((1, 1, de, d), w_map)],
        out_specs=pl.BlockSpec((bm * ch, LANES), lambda i, bt: (i, 0)),
        scratch_shapes=[pltpu.VMEM((d, de), _BF16), pltpu.VMEM((d, de), _BF16), pltpu.VMEM((de, d), _BF16)],
    )
    return pl.pallas_call(
        _ffn_kernel,
        grid_spec=grid_spec,
        out_shape=jax.ShapeDtypeStruct((rows * ch, LANES), _F32),
        compiler_params=_cparams("arbitrary"),
        name="expert_ffn",
    )(btab, xb, w_gate, w_up, w_down)


def _combine_kernel(dest_ref, info_ref, x_ref, mod_ref, g_ref, b_ref, yb_ref, xo_ref, buf, sem, *, alpha,
                    tiles_per_batch):
    ch = ROW_CHUNKS
    tm = x_ref.shape[1]
    i, j = pl.program_id(0), pl.program_id(1)
    nj = pl.num_programs(1)
    step = i * nj + j
    tile = i * tiles_per_batch + j

    def issue(at_tile, slot):
        base = at_tile * tm

        for r in range(tm):
            for k in range(TOP_K):
                d = dest_ref[(base + r) * TOP_K + k]
                pltpu.make_async_copy(yb_ref.at[pl.ds(pl.multiple_of(d * ch, ch), ch)],
                                      buf.at[slot, k, pl.ds(r * ch, ch)], sem.at[slot]).start(priority=k)

    @pl.when(step == 0)
    def _():
        issue(tile, 0)

    @pl.when(step + 1 < pl.num_programs(0) * nj)
    def _():
        issue(jnp.where(j + 1 < nj, tile + 1, (i + 1) * tiles_per_batch), (step + 1) % 2)

    slot = step % 2
    for k in range(TOP_K):
        _row_copy_wait(yb_ref.at[pl.ds(0, tm * ch)], buf.at[slot, k], sem.at[slot])
    info = info_ref[0]
    y = (info[:, 2:3] * _load_rows_chunked(buf, (slot, 0), tm, ch)
         + info[:, 3:4] * _load_rows_chunked(buf, (slot, 1), tm, ch))
    m = mod_ref[0]
    xo_ref[0] = _layer_norm(alpha * x_ref[0] + m[5:6, :] * y, g_ref[...], b_ref[...])


def _combine(dest, info, yb, x, mods, ln_g, ln_b, alpha, n_lat_tiles, latent_only):
    b, p, d = x.shape
    tm = ROW_TILE
    n_tiles = n_lat_tiles if latent_only else p // tm
    grid_spec = pltpu.PrefetchScalarGridSpec(
        num_scalar_prefetch=1,
        grid=(b, n_tiles),
        in_specs=[pl.BlockSpec((1, tm, LANES), lambda i, j, dest: (i, j, 0)),
                  pl.BlockSpec((1, tm, d), lambda i, j, dest: (i, j, 0)),
                  pl.BlockSpec((1, 6, d), lambda i, j, dest: (jnp.where(j < n_lat_tiles, i, b), 0, 0)),
                  pl.BlockSpec(ln_g.shape, lambda i, j, dest: (0, 0)),
                  pl.BlockSpec(ln_b.shape, lambda i, j, dest: (0, 0)),
                  pl.BlockSpec(memory_space=pl.ANY)],
        out_specs=pl.BlockSpec((1, tm, d), lambda i, j, dest: (i, j, 0)),
        scratch_shapes=[pltpu.VMEM((2, TOP_K, tm * ROW_CHUNKS, LANES), _F32), pltpu.SemaphoreType.DMA((2,))],
    )
    return pl.pallas_call(
        functools.partial(_combine_kernel, alpha=alpha, tiles_per_batch=p // tm),
        grid_spec=grid_spec,
        out_shape=jax.ShapeDtypeStruct((b, n_tiles * tm, d), _F32),
        compiler_params=_cparams("arbitrary", "arbitrary", disable_bounds_checks=True),
        name="moe_combine_ln",
    )(dest, info.reshape(b, p, LANES), x, mods, ln_g, ln_b, yb)


def _moe_layer(h, logits, x, mods, ln_g, ln_b, w_gate, w_up, w_down, layer, alpha, n_lat_tiles, latent_only):
    t = logits.shape[0]
    n_blocks = (t * TOP_K) // MOE_ROWS + N_EXPERTS
    info, btab = _route(logits, n_blocks)
    dest = info[:, :TOP_K].astype(jnp.int32).reshape(t * TOP_K)
    xb = _dispatch(dest, btab, h, n_blocks)
    yb = _expert_ffn(btab, xb, w_gate, w_up, w_down, layer)
    return _combine(dest, info, yb, x, mods, ln_g, ln_b, alpha, n_lat_tiles, latent_only)


def _rope_tables(s_len, l_ctx, rot_dim, lane_lo, period):
    rows = s_len // GRID_W
    row = jnp.repeat(jnp.arange(rows, dtype=_F32), GRID_W)
    col = jnp.tile(jnp.arange(GRID_W, dtype=_F32), rows)
    axis_dim = rot_dim // 2
    half = axis_dim // 2
    inv_freq = ROPE_THETA ** (-jnp.arange(0, axis_dim, 2, dtype=_F32) / axis_dim)
    lane = jnp.arange(LANES)
    rel = (lane % period) - lane_lo
    active = (rel >= 0) & (rel < rot_dim)
    rel = jnp.clip(rel, 0, rot_dim - 1)
    use_col = rel >= axis_dim
    f = inv_freq[(rel % axis_dim) % half]
    ang = jnp.where(use_col[None, :], col[:, None], row[:, None]) * f[None, :]
    sign = jnp.where((rel % axis_dim) < half, -1.0, 1.0)
    cos = jnp.where(active[None, :], jnp.cos(ang), 1.0)
    sin = jnp.where(active[None, :], jnp.sin(ang) * sign[None, :], 0.0)
    ident = jnp.ones((l_ctx, LANES), _F32)
    return (jnp.concatenate([cos, ident], axis=0), jnp.concatenate([sin, 0.0 * ident], axis=0))


def _win_weights(w_in):
    d = w_in.shape[0]
    nq, nkv = A_HEADS * A_HEAD_DIM, A_KV_HEADS * A_HEAD_DIM
    k = w_in[:, nq:nq + nkv].reshape(d, A_KV_HEADS, 1, A_HEAD_DIM)
    k_dup = jnp.concatenate([k, k], axis=2).reshape(d, 2 * nkv)
    return jnp.concatenate([w_in[:, :nq], k_dup, w_in[:, nq + nkv:]], axis=1).astype(_BF16)


def _mla_weights(w_in, w_uq, w_ukv):
    d = w_in.shape[0]
    pad = LANES - B_NOPE - B_ROPE
    kr = jnp.concatenate([jnp.zeros((d, B_NOPE), _F32), w_in[:, B_Q_RANK + B_KV_RANK:],
                          jnp.zeros((d, pad), _F32)], axis=1)
    w_in_p = jnp.concatenate([w_in[:, :B_Q_RANK + B_KV_RANK], kr], axis=1).astype(_BF16)
    uq = w_uq.reshape(B_Q_RANK, B_HEADS, B_NOPE + B_ROPE)
    uq = jnp.pad(uq, ((0, 0), (0, 0), (0, pad))).reshape(B_Q_RANK, B_HEADS * LANES).astype(_BF16)
    ukv = w_ukv.reshape(B_KV_RANK, B_HEADS, B_NOPE + B_V)
    uk = jnp.pad(ukv[:, :, :B_NOPE], ((0, 0), (0, 0), (0, LANES - B_NOPE)))
    uk = uk.reshape(B_KV_RANK, B_HEADS * LANES).astype(_BF16)
    uv = ukv[:, :, B_NOPE:].reshape(B_KV_RANK, B_HEADS * B_V).astype(_BF16)
    return w_in_p, uq, uk, uv


def kernel(x, c, ctx, c_ctx, ada_w, ada_b, ln_mix_g, ln_mix_b, ln_ffn_g, ln_ffn_b, win_w_in, win_w_out, win_sink, mla_w_in, mla_q_norm, mla_kv_norm, mla_w_uq, mla_w_ukv, mla_w_out, dif_w_in, dif_lambda, dif_subln, dif_w_out, moe_w_grp, moe_b_grp, moe_w_rt, moe_b_rt, moe_w_gate, moe_w_up, moe_w_down):
    b, s_len, d = x.shape
    l_ctx = ctx.shape[1]
    depth = ada_w.shape[0]
    p = s_len + l_ctx
    assert s_len % ROW_TILE == 0 and l_ctx % ROW_TILE == 0 and s_len % GRID_W == 0
    assert s_len % (ATTN_Q_TILE * Q_TILES_PER_STEP) == 0 and l_ctx == ATTN_Q_TILE and s_len % l_ctx == 0
    assert d == ROW_CHUNKS * LANES and s_len % KEY_CHUNK == 0 and MOE_ROWS & (MOE_ROWS - 1) == 0 and (b * p) % ROUTE_TILE == 0
    alpha = (2.0 * depth) ** 0.25
    n_lat_tiles = s_len // ROW_TILE

    xs = jnp.concatenate([x, ctx], axis=1)
    cs = jnp.concatenate([c, c_ctx[None, :]], axis=0)
    mods = _ada_mods(cs, ada_w, ada_b).reshape(depth, b + 1, 6, d)
    cos64, sin64 = _rope_tables(s_len, l_ctx, A_HEAD_DIM, 0, A_HEAD_DIM)
    cos32, sin32 = _rope_tables(s_len, l_ctx, B_ROPE, B_NOPE, LANES)
    row = lambda v: v.reshape(1, -1)

    for i in range(depth):
        kind, slot = i % N_MIXERS, i // N_MIXERS
        if kind == 0:
            qk, vt = _proj_rope(xs, mods[i], _win_weights(win_w_in[slot]), cos64, sin64,
                                (A_HEADS + 2 * A_KV_HEADS) * A_HEAD_DIM, A_KV_HEADS * A_HEAD_DIM, n_lat_tiles,
                                n_q=A_HEADS * A_HEAD_DIM, q_scale=A_HEAD_DIM ** -0.5 * LOG2E)
            o = _attn_win(qk, vt, win_sink[slot], s_len, l_ctx)
            w_out = win_w_out[slot]
        elif kind == 1:
            w_in_p, uq, uk, uv = _mla_weights(mla_w_in[slot], mla_w_uq[slot], mla_w_ukv[slot])
            q, k, vt = _proj_mla(xs, mods[i], w_in_p, row(mla_q_norm[slot]), row(mla_kv_norm[slot]),
                                uq, uk, uv, cos32, sin32, n_lat_tiles)
            o = _attn_mla(q, k, vt, s_len)
            w_out = mla_w_out[slot]
        else:
            lambda_init = 0.8 - 0.6 * math.exp(-0.3 * i)
            qk, vt = _proj_rope(xs, mods[i], dif_w_in[slot].astype(_BF16), cos64, sin64,
                                2 * C_HEADS * 2 * C_HEAD_DIM, C_HEADS * C_V_DIM, n_lat_tiles,
                                n_q=C_HEADS * 2 * C_HEAD_DIM, q_scale=C_HEAD_DIM ** -0.5 * LOG2E)
            o = _attn_diff(qk, vt, dif_lambda[slot], dif_subln[slot], s_len, lambda_init)
            w_out = dif_w_out[slot]

        w_router = jnp.concatenate([moe_w_grp[i], moe_w_rt[i],
                                    jnp.zeros((d, LANES - N_GROUPS - N_EXPERTS), _F32)], axis=1)
        b_router = jnp.concatenate([moe_b_grp[i], moe_b_rt[i],
                                    jnp.zeros((LANES - N_GROUPS - N_EXPERTS,), _F32)])[None, :]
        w_hi = w_router.astype(_BF16)
        w_lo = (w_router - w_hi.astype(_F32)).astype(_BF16)
        xs, h, logits = _out_proj(o, w_out.astype(_BF16), xs, mods[i], row(ln_mix_g[i]), row(ln_mix_b[i]),
                                  jnp.concatenate([w_hi, w_lo], axis=1), b_router, alpha, n_lat_tiles)
        xs = _moe_layer(h.reshape(b * p * ROW_CHUNKS, LANES), logits.reshape(b * p, LANES), xs, mods[i], row(ln_ffn_g[i]),
                        row(ln_ffn_b[i]), moe_w_gate, moe_w_up, moe_w_down, i, alpha, n_lat_tiles,
                        latent_only=i == depth - 1)
    return xs
```
